```python
import math
import jax
import jax.numpy as jnp
from jax import lax
import numpy as np

D_MODEL = 1024
BATCH = 8
SEQ = 4096
DEPTH = 2

N_MIXERS = 2
N_MEM = 256
HEAD_DIM = 64
MIX_HEADS = 12
MEM_HEADS = 4
MIX_WIDTH = MIX_HEADS * HEAD_DIM
MEM_WIDTH = MEM_HEADS * HEAD_DIM
GATE_WIDTH = MIX_WIDTH + MEM_WIDTH
CONV_WIDTH = 4
DELTA_CHUNK = 64
MOBA_BLOCK = 256
MOBA_TOPK = 3
MOBA_QCHUNK = 32
ROPE_THETA = 10000.0
NORM_EPS = 1e-6
MASK_VALUE = -1e30
DELTA_IN = 3 * MIX_WIDTH + GATE_WIDTH + MEM_WIDTH + 2 * MIX_HEADS
MOBA_IN = 3 * MIX_WIDTH + GATE_WIDTH + MEM_WIDTH

kernel_name = 'hybrid_deltanet_moba_memory'


def rmsnorm(x, g):
    xf = x.astype(jnp.float32)
    y = xf * lax.rsqrt(jnp.mean(xf * xf, axis=-1, keepdims=True) + NORM_EPS)
    return (y * g.astype(jnp.float32)).astype(x.dtype)


def l2norm(x):
    xf = x.astype(jnp.float32)
    return xf * lax.rsqrt(jnp.sum(xf * xf, axis=-1, keepdims=True) + NORM_EPS)


def apply_rope(x, positions):
    d = x.shape[-1]
    half = d // 2
    inv_freq = ROPE_THETA ** (-jnp.arange(half, dtype=jnp.float32) * (2.0 / d))
    ang = positions.astype(jnp.float32)[..., None] * inv_freq
    cos = jnp.cos(ang)[:, :, None, :]
    sin = jnp.sin(ang)[:, :, None, :]
    xf = x.astype(jnp.float32)
    x1, x2 = xf[..., :half], xf[..., half:]
    return jnp.concatenate([x1 * cos - x2 * sin, x2 * cos + x1 * sin], axis=-1).astype(x.dtype)


def causal_short_conv(x, w):
    c = x.shape[-1]
    return lax.conv_general_dilated(
        x, w.astype(x.dtype)[:, None, :], window_strides=(1,),
        padding=[(CONV_WIDTH - 1, 0)], dimension_numbers=('NWC', 'WIO', 'NWC'),
        feature_group_count=c)


def gated_delta_rule(q, k, v, log_decay, beta):
    B_, S_, H_, dk = q.shape
    dv = v.shape[-1]
    C = DELTA_CHUNK
    N = S_ // C

    def to_chunks(t):
        t = t.reshape((B_, N, C, H_) + t.shape[3:])
        return jnp.moveaxis(t, (1, 3), (0, 2))

    qc = to_chunks(q * (dk ** -0.5))
    kc = to_chunks(k)
    vc = to_chunks(v)
    gcum = jnp.cumsum(to_chunks(log_decay), axis=-1)
    bc = to_chunks(beta)
    tri_incl = jnp.tril(jnp.ones((C, C), dtype=bool))
    tri_strict = jnp.tril(jnp.ones((C, C), dtype=bool), -1)
    decay = jnp.exp(jnp.where(tri_incl, gcum[..., :, None] - gcum[..., None, :], -jnp.inf))
    kb = kc * bc[..., None]
    lower = jnp.where(tri_strict, jnp.einsum('nbhid,nbhjd->nbhij', kb, kc) * decay, 0.0)
    eye = jnp.eye(C, dtype=jnp.float32)
    rhs = jnp.concatenate([vc * bc[..., None], kb * jnp.exp(gcum)[..., None]], axis=-1)
    sol = lax.linalg.triangular_solve(eye + lower, rhs, left_side=True, lower=True,
                                      unit_diagonal=True)
    u, w = sol[..., :dv], sol[..., dv:]
    attn_intra = jnp.where(tri_incl, jnp.einsum('nbhid,nbhjd->nbhij', qc, kc) * decay, 0.0)

    def step(state, xs):
        q_i, k_i, u_i, w_i, g_i, a_i = xs
        v_new = u_i - jnp.einsum('bhcd,bhde->bhce', w_i, state)
        o = (jnp.einsum('bhcd,bhde->bhce', q_i * jnp.exp(g_i)[..., None], state)
             + jnp.einsum('bhij,bhje->bhie', a_i, v_new))
        g_last = g_i[..., -1:]
        state = (state * jnp.exp(g_last)[..., None]
                 + jnp.einsum('bhcd,bhce->bhde', k_i * jnp.exp(g_last - g_i)[..., None], v_new))
        return state, o

    s0 = jnp.zeros((B_, H_, dk, dv), jnp.float32)
    _, o = lax.scan(step, s0, (qc, kc, u, w, gcum, attn_intra))
    return jnp.moveaxis(o, (0, 2), (1, 3)).reshape(B_, S_, H_, dv)


def moba_attention(q, k, v):
    B_, S_, H_, d = q.shape
    nb = max(-(-S_ // MOBA_BLOCK), MOBA_TOPK)
    sp = nb * MOBA_BLOCK
    pad = ((0, 0), (0, sp - S_), (0, 0), (0, 0))
    kblk = jnp.moveaxis(jnp.pad(k, pad), 2, 1).reshape(B_, H_, nb, MOBA_BLOCK, d)
    vblk = jnp.moveaxis(jnp.pad(v, pad), 2, 1).reshape(B_, H_, nb, MOBA_BLOCK, d)
    kmean = jnp.mean(kblk.astype(jnp.float32), axis=3)
    nq = S_ // MOBA_QCHUNK
    qch = jnp.moveaxis(q, 2, 1).reshape(B_, H_, nq, MOBA_QCHUNK, d).transpose(2, 0, 1, 3, 4)
    scale = d ** -0.5
    bidx = jnp.arange(B_)[:, None, None]
    hidx = jnp.arange(H_)[None, :, None]
    blocks = jnp.arange(nb)

    def one_chunk(args):
        c, qc = args
        qpos = c * MOBA_QCHUNK + jnp.arange(MOBA_QCHUNK)
        own = (c * MOBA_QCHUNK) // MOBA_BLOCK
        gate = jnp.einsum('bhqd,bhnd->bhqn', qc.astype(jnp.float32), kmean)
        gate = jnp.where(blocks < own, gate, -jnp.inf)
        top_val, top_idx = lax.top_k(gate, MOBA_TOPK)
        sel_valid = jnp.isfinite(top_val)
        kown = lax.dynamic_index_in_dim(kblk, own, axis=2, keepdims=False)
        vown = lax.dynamic_index_in_dim(vblk, own, axis=2, keepdims=False)
        kpos = own * MOBA_BLOCK + jnp.arange(MOBA_BLOCK)
        causal = kpos[None, :] <= qpos[:, None]
        logit_own = jnp.einsum('bhqd,bhkd->bhqk', qc, kown).astype(jnp.float32) * scale
        logits = [jnp.where(causal, logit_own, MASK_VALUE)]
        for s in range(MOBA_TOPK):
            ks = kblk[bidx, hidx, top_idx[..., s]]
            ls = jnp.einsum('bhqd,bhqkd->bhqk', qc, ks).astype(jnp.float32) * scale
            logits.append(jnp.where(sel_valid[..., s, None], ls, MASK_VALUE))
        p = jax.nn.softmax(jnp.concatenate(logits, axis=-1), axis=-1).astype(v.dtype)
        out = jnp.einsum('bhqk,bhkd->bhqd', p[..., :MOBA_BLOCK], vown)
        for s in range(MOBA_TOPK):
            vs = vblk[bidx, hidx, top_idx[..., s]]
            ps = p[..., (s + 1) * MOBA_BLOCK:(s + 2) * MOBA_BLOCK]
            out = out + jnp.einsum('bhqk,bhqkd->bhqd', ps, vs)
        return out

    out = lax.map(one_chunk, (jnp.arange(nq), qch))
    return out.transpose(1, 0, 3, 2, 4).reshape(B_, S_, H_, d)


def memory_attention(mq, mem, mem_norm_g, w_mem_kv):
    B_, S_, _ = mq.shape
    kv = rmsnorm(mem, mem_norm_g) @ w_mem_kv
    mk, mv = jnp.split(kv, 2, axis=-1)
    mk = mk.reshape(B_, -1, MEM_HEADS, HEAD_DIM)
    mv = mv.reshape(B_, -1, MEM_HEADS, HEAD_DIM)
    q = mq.reshape(B_, S_, MEM_HEADS, HEAD_DIM)
    logits = jnp.einsum('bshd,bmhd->bhsm', q, mk).astype(jnp.float32) * (HEAD_DIM ** -0.5)
    p = jax.nn.softmax(logits, axis=-1).astype(mv.dtype)
    return jnp.einsum('bhsm,bmhd->bshd', p, mv).reshape(B_, S_, MEM_WIDTH)


def gated_output(mix, memo, z, w_out):
    y = jnp.concatenate([mix, memo.astype(mix.dtype)], axis=-1) * jax.nn.silu(z)
    return y @ w_out


def delta_layer(h, mem, norm_g, w_in, conv_w, a_log, dt_bias, o_norm, mem_norm_g, w_mem_kv, w_out):
    B_, S_, _ = h.shape
    proj = rmsnorm(h, norm_g) @ w_in
    i1 = 3 * MIX_WIDTH
    i2 = i1 + GATE_WIDTH
    i3 = i2 + MEM_WIDTH
    qkv, z, mq, ba = jnp.split(proj, [i1, i2, i3], axis=-1)
    qkv = jax.nn.silu(causal_short_conv(qkv, conv_w))
    q, k, v = [t.reshape(B_, S_, MIX_HEADS, HEAD_DIM) for t in jnp.split(qkv, 3, axis=-1)]
    b_raw, a_raw = jnp.split(ba.astype(jnp.float32), 2, axis=-1)
    beta = jax.nn.sigmoid(b_raw)
    log_decay = -jnp.exp(a_log.astype(jnp.float32)) * jax.nn.softplus(a_raw + dt_bias.astype(jnp.float32))
    o = gated_delta_rule(l2norm(q), l2norm(k), v.astype(jnp.float32), log_decay, beta)
    o = rmsnorm(o, o_norm).reshape(B_, S_, MIX_WIDTH).astype(h.dtype)
    m = memory_attention(mq, mem, mem_norm_g, w_mem_kv)
    return gated_output(o, m, z, w_out)


def moba_layer(h, mem, positions, norm_g, w_in, mem_norm_g, w_mem_kv, w_out):
    B_, S_, _ = h.shape
    proj = rmsnorm(h, norm_g) @ w_in
    q, k, v, z, mq = jnp.split(proj, [MIX_WIDTH, 2 * MIX_WIDTH, 3 * MIX_WIDTH,
                                      3 * MIX_WIDTH + GATE_WIDTH], axis=-1)
    q = apply_rope(q.reshape(B_, S_, MIX_HEADS, HEAD_DIM), positions)
    k = apply_rope(k.reshape(B_, S_, MIX_HEADS, HEAD_DIM), positions)
    v = v.reshape(B_, S_, MIX_HEADS, HEAD_DIM)
    o = moba_attention(q, k, v).reshape(B_, S_, MIX_WIDTH)
    m = memory_attention(mq, mem, mem_norm_g, w_mem_kv)
    return gated_output(o, m, z, w_out)


def setup_inputs(seed: int = 0) -> dict:
    key = jax.random.key(seed)
    ks = jax.random.split(key, 24)
    f32 = jnp.float32

    def dense(k, fan_in, fan_out):
        return jax.random.normal(k, (fan_in, fan_out), f32) * fan_in ** -0.5

    def gain(k, n):
        return 1.0 + 0.05 * jax.random.normal(k, (n,), f32)

    x = jax.random.normal(ks[0], (BATCH, SEQ, D_MODEL), f32)
    mem = jax.random.normal(ks[1], (BATCH, N_MEM, D_MODEL), f32)
    start = jax.random.randint(ks[2], (BATCH, 1), 0, 1024, dtype=jnp.int32)
    positions = start + jnp.arange(SEQ, dtype=jnp.int32)[None, :]
    dt = jnp.exp(jax.random.uniform(ks[7], (MIX_HEADS,), f32,
                                    minval=math.log(1e-3), maxval=math.log(1e-1)))
    return {
        'x': x,
        'mem': mem,
        'positions': positions,
        'norm_0': gain(ks[3], D_MODEL),
        'w_in_0': dense(ks[4], D_MODEL, DELTA_IN),
        'conv_w_0': jax.random.normal(ks[5], (CONV_WIDTH, 3 * MIX_WIDTH), f32) * CONV_WIDTH ** -0.5,
        'a_log_0': jnp.log(jax.random.uniform(ks[6], (MIX_HEADS,), f32, minval=1.0, maxval=16.0)),
        'dt_bias_0': dt + jnp.log(-jnp.expm1(-dt)),
        'o_norm_0': gain(ks[8], HEAD_DIM),
        'mem_norm_0': gain(ks[9], D_MODEL),
        'w_mem_kv_0': dense(ks[10], D_MODEL, 2 * MEM_WIDTH),
        'w_out_0': dense(ks[11], GATE_WIDTH, D_MODEL),
        'norm_1': gain(ks[12], D_MODEL),
        'w_in_1': dense(ks[13], D_MODEL, MOBA_IN),
        'mem_norm_1': gain(ks[14], D_MODEL),
        'w_mem_kv_1': dense(ks[15], D_MODEL, 2 * MEM_WIDTH),
        'w_out_1': dense(ks[16], GATE_WIDTH, D_MODEL),
        'final_norm': gain(ks[17], D_MODEL),
    }


def reference(x, mem, positions, norm_0, w_in_0, conv_w_0, a_log_0, dt_bias_0, o_norm_0,
              mem_norm_0, w_mem_kv_0, w_out_0, norm_1, w_in_1, mem_norm_1, w_mem_kv_1,
              w_out_1, final_norm):
    layers = [
        ('delta', (norm_0, w_in_0, conv_w_0, a_log_0, dt_bias_0, o_norm_0,
                   mem_norm_0, w_mem_kv_0, w_out_0)),
        ('moba', (norm_1, w_in_1, mem_norm_1, w_mem_kv_1, w_out_1)),
    ]
    h = x
    for i in range(DEPTH):
        kind, params = layers[i]
        if i % N_MIXERS == 0:
            h = h + delta_layer(h, mem, *params)
        else:
            h = h + moba_layer(h, mem, positions, *params)
    return rmsnorm(h, final_norm)
```

```python
import functools

import jax
import jax.numpy as jnp
from jax import lax
from jax.experimental import pallas as pl
from jax.experimental.pallas import tpu as pltpu

D_MODEL = 1024
HEAD_DIM = 64
MIX_HEADS = 12
MEM_HEADS = 4
MIX_WIDTH = MIX_HEADS * HEAD_DIM
MEM_WIDTH = MEM_HEADS * HEAD_DIM
GATE_WIDTH = MIX_WIDTH + MEM_WIDTH
N_MEM = 256
CONV_WIDTH = 4
MOBA_BLOCK = 256
MOBA_TOPK = 3
ROPE_THETA = 10000.0
NORM_EPS = 1e-6
MASK_VALUE = -1e30

LANES = 128
SUBLANES = 8
PAIR = 2 * HEAD_DIM
N_PAIRS = MIX_HEADS // 2
DELTA_CHUNK = 128
INV_BASE = 16
QK_SCALE = HEAD_DIM ** -0.5
VMEM_LIMIT = 56 * 1024 * 1024

BF16 = jnp.bfloat16
F32 = jnp.float32

_NT = (((1,), (1,)), ((), ()))


def _dot(a, b):
    return jnp.dot(a.astype(BF16), b.astype(BF16), preferred_element_type=F32)


def _dot_nt(a, b):
    return lax.dot_general(a.astype(BF16), b.astype(BF16), _NT, preferred_element_type=F32)


def _dot_hi(a, b):
    return jnp.dot(a, b, precision=lax.Precision.HIGHEST, preferred_element_type=F32)


def _dot_nt_hi(a, b):
    return lax.dot_general(a, b, _NT, precision=lax.Precision.HIGHEST,
                           preferred_element_type=F32)


def _rms_scale(x):
    return lax.rsqrt(jnp.mean(x * x, axis=-1, keepdims=True) + NORM_EPS)


def _sigmoid(x):
    return 1.0 / (1.0 + jnp.exp(-x))


def _softplus(x):
    return jnp.maximum(x, 0.0) + jnp.log(1.0 + jnp.exp(-jnp.abs(x)))


def _head_masks():
    lane = lax.broadcasted_iota(jnp.int32, (1, PAIR), 1)
    return [((lane >> 6) == h).astype(F32) for h in range(2)]


def _memkv_kernel(mem_ref, g0_ref, w0_ref, g1_ref, w1_ref, o0_ref, o1_ref):
    m = mem_ref[0]
    mn = m * _rms_scale(m)
    o0_ref[0] = _dot(mn * g0_ref[...], w0_ref[...])
    o1_ref[0] = _dot(mn * g1_ref[...], w1_ref[...])


def _memkv(mem, g0, w0, g1, w1):
    b = mem.shape[0]
    full = lambda shp: pl.BlockSpec(shp, lambda i: (0,) * len(shp))
    return pl.pallas_call(
        _memkv_kernel,
        grid=(b,),
        in_specs=[pl.BlockSpec((1, N_MEM, D_MODEL), lambda i: (i, 0, 0)),
                  full((1, D_MODEL)), full((D_MODEL, 2 * MEM_WIDTH)),
                  full((1, D_MODEL)), full((D_MODEL, 2 * MEM_WIDTH))],
        out_specs=[pl.BlockSpec((1, N_MEM, 2 * MEM_WIDTH), lambda i: (i, 0, 0))] * 2,
        out_shape=[jax.ShapeDtypeStruct((b, N_MEM, 2 * MEM_WIDTH), F32)] * 2,
        compiler_params=pltpu.CompilerParams(vmem_limit_bytes=VMEM_LIMIT),
        name="memkv",
    )(mem, g0, w0, g1, w1)


def _inproj0_kernel(x_ref, g_ref, wqkv_ref, wz_ref, wmq_ref, wba_ref,
                    qkv_ref, z_ref, mq_ref, ba_ref):
    x = x_ref[0]
    xn = (x * _rms_scale(x) * g_ref[...]).astype(BF16)
    for w_ref, o_ref in ((wqkv_ref, qkv_ref), (wz_ref, z_ref),
                         (wmq_ref, mq_ref), (wba_ref, ba_ref)):
        o_ref[0] = jnp.dot(xn, w_ref[...], preferred_element_type=F32)


def _inproj0(x, g, wqkv, wz, wmq, wba, tm):
    b, s, _ = x.shape
    widths = (3 * MIX_WIDTH, GATE_WIDTH, MEM_WIDTH, LANES)
    wspec = lambda n: pl.BlockSpec((D_MODEL, n), lambda i, j: (0, 0))
    return pl.pallas_call(
        _inproj0_kernel,
        grid=(b, s // tm),
        in_specs=[pl.BlockSpec((1, tm, D_MODEL), lambda i, j: (i, j, 0)),
                  pl.BlockSpec((1, D_MODEL), lambda i, j: (0, 0))]
                 + [wspec(n) for n in widths],
        out_specs=[pl.BlockSpec((1, tm, n), lambda i, j: (i, j, 0)) for n in widths],
        out_shape=[jax.ShapeDtypeStruct((b, s, n), F32) for n in widths],
        compiler_params=pltpu.CompilerParams(vmem_limit_bytes=VMEM_LIMIT),
        name="inproj0",
    )(x, g, wqkv, wz, wmq, wba)


def _inproj1_kernel(x_ref, g_ref, pos_ref, invf_ref, wq_ref, wk_ref, wv_ref, wz_ref, wmq_ref,
                    q_ref, k_ref, v_ref, z_ref, mq_ref):
    x = x_ref[0]
    xn = (x * _rms_scale(x) * g_ref[...]).astype(BF16)
    ang = pos_ref[0].astype(F32) * invf_ref[...]
    cos = jnp.cos(ang)
    sin = jnp.sin(ang)
    lane = lax.broadcasted_iota(jnp.int32, ang.shape, 1)
    first_half = (lane & (HEAD_DIM - 1)) < (HEAD_DIM // 2)
    sin_signed = jnp.where(first_half, -sin, sin)
    for w_ref, o_ref in ((wq_ref, q_ref), (wk_ref, k_ref)):
        y = jnp.dot(xn, w_ref[...], preferred_element_type=F32)
        for sl in range(MIX_WIDTH // LANES):
            ys = y[:, sl * LANES:(sl + 1) * LANES]
            rot = jnp.where(first_half,
                            pltpu.roll(ys, LANES - HEAD_DIM // 2, 1),
                            pltpu.roll(ys, HEAD_DIM // 2, 1))
            o_ref[0, :, sl * LANES:(sl + 1) * LANES] = ys * cos + rot * sin_signed
    for w_ref, o_ref in ((wv_ref, v_ref), (wz_ref, z_ref), (wmq_ref, mq_ref)):
        o_ref[0] = jnp.dot(xn, w_ref[...], preferred_element_type=F32)


def _inproj1(x, g, pos3, invf, wq, wk, wv, wz, wmq, tm):
    b, s, _ = x.shape
    widths = (MIX_WIDTH, MIX_WIDTH, MIX_WIDTH, GATE_WIDTH, MEM_WIDTH)
    wspec = lambda n: pl.BlockSpec((D_MODEL, n), lambda i, j: (0, 0))
    return pl.pallas_call(
        _inproj1_kernel,
        grid=(b, s // tm),
        in_specs=[pl.BlockSpec((1, tm, D_MODEL), lambda i, j: (i, j, 0)),
                  pl.BlockSpec((1, D_MODEL), lambda i, j: (0, 0)),
                  pl.BlockSpec((1, tm, 1), lambda i, j: (i, j, 0)),
                  pl.BlockSpec((1, LANES), lambda i, j: (0, 0))]
                 + [wspec(n) for n in widths],
        out_specs=[pl.BlockSpec((1, tm, n), lambda i, j: (i, j, 0)) for n in widths],
        out_shape=[jax.ShapeDtypeStruct((b, s, n), F32) for n in widths],
        compiler_params=pltpu.CompilerParams(vmem_limit_bytes=VMEM_LIMIT),
        name="inproj1",
    )(x, g, pos3, invf, wq, wk, wv, wz, wmq)


def _unit_lower_inverse(low, row, col, eye):
    c = low.shape[0]
    shift = INV_BASE.bit_length() - 1
    n1 = jnp.where((row >> shift) == (col >> shift), -low, 0.0)
    x = eye + n1
    power = 2
    while power < INV_BASE:
        n1 = _dot(n1, n1)
        x = x + _dot(x, n1)
        power *= 2
    size = INV_BASE
    while size < c:
        sh = size.bit_length() - 1
        sel = (((row >> (sh + 1)) == (col >> (sh + 1)))
               & (((row >> sh) & 1) == 1) & (((col >> sh) & 1) == 0))
        off = jnp.where(sel, low, 0.0)
        x = x - _dot(_dot(x, off), x)
        size *= 2
    return x


def _delta_kernel(q_ref, k_ref, v_ref, ba_ref, cwq_ref, cwk_ref, cwv_ref,
                  alog_ref, dt_ref, onorm_ref, o_ref, ext_ref, s_ref):
    c = DELTA_CHUNK
    seq = q_ref.shape[1]
    pair = pl.program_id(1)
    masks = _head_masks()
    row = lax.broadcasted_iota(jnp.int32, (c, c), 0)
    col = lax.broadcasted_iota(jnp.int32, (c, c), 1)
    tri_incl = row >= col
    tri_strict = row > col
    eye = (row == col).astype(F32)
    tril_f = tri_incl.astype(F32)
    head_block = ((row >> 6) == (col >> 6)).astype(F32)
    head_of_lane = col >> 6
    e_b = (row == 2 * pair + head_of_lane).astype(F32)
    e_a = (row == MIX_HEADS + 2 * pair + head_of_lane).astype(F32)
    neg_a = -jnp.exp(alog_ref[...])
    dt_row = dt_ref[...]
    onorm = onorm_ref[...]

    ext_ref[:, 0:SUBLANES, :] = jnp.zeros((3, SUBLANES, PAIR), F32)
    s_ref[...] = jnp.zeros((PAIR, PAIR), F32)

    def conv_silu(idx, x_ref, cw_ref, t0):
        raw = x_ref[0, pl.ds(t0, c), :]
        ext_ref[idx, SUBLANES:SUBLANES + c, :] = raw
        acc = raw * cw_ref[CONV_WIDTH - 1:CONV_WIDTH, :]
        for tap in range(CONV_WIDTH - 1):
            back = CONV_WIDTH - 1 - tap
            acc = acc + ext_ref[idx, SUBLANES - back:SUBLANES - back + c, :] * cw_ref[tap:tap + 1, :]
        ext_ref[idx, 0:SUBLANES, :] = raw[c - SUBLANES:c, :]
        return acc * _sigmoid(acc)

    def l2norm(x):
        return x * lax.rsqrt(_dot_hi(x * x, head_block) + NORM_EPS)

    def chunk_step(ci, carry):
        t0 = pl.multiple_of(ci * c, c)
        qn = l2norm(conv_silu(0, q_ref, cwq_ref, t0))
        kn = l2norm(conv_silu(1, k_ref, cwk_ref, t0))
        vc = conv_silu(2, v_ref, cwv_ref, t0)
        ba = ba_ref[0, pl.ds(t0, c), :]
        beta = _sigmoid(_dot_hi(ba, e_b))
        g = neg_a * _softplus(_dot_hi(ba, e_a) + dt_row)
        gc = _dot_hi(tril_f, g)
        gc_t = gc.T
        eg = jnp.exp(gc)
        g_last = gc[c - 1:c, :]
        kb = kn * beta
        u = jnp.zeros((c, PAIR), F32)
        w = jnp.zeros((c, PAIR), F32)
        aqk = []
        for h in range(2):
            mh = masks[h]
            gcol = jnp.broadcast_to(gc[:, h * HEAD_DIM:h * HEAD_DIM + 1], (c, c))
            grow = gc_t[h * HEAD_DIM:h * HEAD_DIM + 1, :]
            decay = jnp.exp(jnp.where(tri_incl, gcol - grow, -jnp.inf))
            kbh = kb * mh
            low = jnp.where(tri_strict, _dot_nt(kbh, kn) * decay, 0.0)
            aqk.append(jnp.where(tri_incl, _dot_nt(qn * (mh * QK_SCALE), kn) * decay, 0.0))
            tinv = _unit_lower_inverse(low, row, col, eye)
            u = u + _dot(tinv, vc * beta * mh)
            w = w + _dot(tinv, kbh * eg)
        state = s_ref[...]
        v_new = u - _dot(w, state)
        o = _dot(qn * QK_SCALE * eg, state)
        for h in range(2):
            o = o + _dot(aqk[h], v_new * masks[h])
        kd = kn * jnp.exp(g_last - gc)
        s_ref[...] = state * jnp.exp(g_last) + head_block * _dot(kd.T, v_new)
        ms = _dot_hi(o * o, head_block) * (1.0 / HEAD_DIM)
        o_ref[0, pl.ds(t0, c), :] = o * lax.rsqrt(ms + NORM_EPS) * onorm
        return carry

    lax.fori_loop(0, seq // c, chunk_step, 0)


def _delta_mixer(qkv, ba, conv_w, alog_row, dt_row, onorm_row):
    b, s, _ = qkv.shape
    seq_spec = lambda off: pl.BlockSpec((1, s, PAIR), lambda i, p, off=off: (i, 0, off + p))
    cw_spec = lambda off: pl.BlockSpec((CONV_WIDTH, PAIR), lambda i, p, off=off: (0, off + p))
    row_spec = pl.BlockSpec((1, PAIR), lambda i, p: (0, p))
    return pl.pallas_call(
        _delta_kernel,
        grid=(b, N_PAIRS),
        in_specs=[seq_spec(0), seq_spec(N_PAIRS), seq_spec(2 * N_PAIRS),
                  pl.BlockSpec((1, s, LANES), lambda i, p: (i, 0, 0)),
                  cw_spec(0), cw_spec(N_PAIRS), cw_spec(2 * N_PAIRS),
                  row_spec, row_spec, pl.BlockSpec((1, PAIR), lambda i, p: (0, 0))],
        out_specs=pl.BlockSpec((1, s, PAIR), lambda i, p: (i, 0, p)),
        out_shape=jax.ShapeDtypeStruct((b, s, MIX_WIDTH), F32),
        scratch_shapes=[pltpu.VMEM((3, SUBLANES + DELTA_CHUNK, PAIR), F32),
                        pltpu.VMEM((PAIR, PAIR), F32)],
        compiler_params=pltpu.CompilerParams(vmem_limit_bytes=VMEM_LIMIT),
        name="delta_mixer",
    )(qkv, qkv, qkv, ba, conv_w, conv_w, conv_w, alog_row, dt_row, onorm_row)


def _moba_kernel(q_ref, k_ref, v_ref, o_ref, kbf_ref, vt_ref, kmean_ref, sel_ref):
    blk = MOBA_BLOCK
    seq = k_ref.shape[1]
    nb = seq // blk
    qi = pl.program_id(2)
    masks = _head_masks()

    @pl.when(qi == 0)
    def _prepare():
        def body(j, carry):
            r0 = pl.multiple_of(j * blk, blk)
            kj = k_ref[0, pl.ds(r0, blk), :]
            vj = v_ref[0, pl.ds(r0, blk), :]
            kbf_ref[pl.ds(r0, blk), :] = kj.astype(BF16)
            for t in range(blk // LANES):
                vt_ref[:, pl.ds(pl.multiple_of(r0 + t * LANES, LANES), LANES)] = (
                    vj[t * LANES:(t + 1) * LANES, :].T.astype(BF16))
            kmean_ref[pl.ds(j, 1), :] = jnp.sum(kj, axis=0, keepdims=True) * (1.0 / blk)
            return carry
        lax.fori_loop(0, nb, body, 0)

    q = q_ref[0]
    kmean = kmean_ref[...]
    blk_id = lax.broadcasted_iota(jnp.int32, (nb, blk), 0)
    valid = blk_id < qi
    qs = []
    for h in range(2):
        gate = _dot_nt_hi(kmean * masks[h], q)
        gm = jnp.where(valid, gate, -jnp.inf)
        rank = jnp.zeros((nb, blk), jnp.int32)
        for j2 in range(nb):
            gj = gm[j2:j2 + 1, :]
            beats = (gj > gm) | ((gj == gm) & (blk_id > j2))
            rank = rank + beats.astype(jnp.int32)
        sel_ref[h] = (valid & (rank < MOBA_TOPK)).astype(F32)
        qs.append((q * (masks[h] * QK_SCALE)).astype(BF16))

    key_pos = lax.broadcasted_iota(jnp.int32, (blk, blk), 0)
    qry_pos = lax.broadcasted_iota(jnp.int32, (blk, blk), 1)
    causal = key_pos <= qry_pos
    own0 = pl.multiple_of(qi * blk, blk)
    k_own = kbf_ref[pl.ds(own0, blk), :]
    vt_own = vt_ref[:, pl.ds(own0, blk)]
    init = []
    for h in range(2):
        st = jnp.where(causal, _dot_nt(k_own, qs[h]), MASK_VALUE)
        m = jnp.max(st, axis=0, keepdims=True)
        p = jnp.exp(st - m)
        l = jnp.sum(p, axis=0, keepdims=True)
        acc = jnp.dot(vt_own, p.astype(BF16), preferred_element_type=F32)
        init += [m, l, acc]

    def body(j, carry):
        r0 = pl.multiple_of(j * blk, blk)
        kj = kbf_ref[pl.ds(r0, blk), :]
        vtj = vt_ref[:, pl.ds(r0, blk)]
        out = []
        for h in range(2):
            m, l, acc = carry[3 * h:3 * h + 3]
            chosen = sel_ref[h, pl.ds(j, 1), :] > 0.5
            st = jnp.where(chosen, _dot_nt(kj, qs[h]), MASK_VALUE)
            m_new = jnp.maximum(m, jnp.max(st, axis=0, keepdims=True))
            alpha = jnp.exp(m - m_new)
            p = jnp.exp(st - m_new)
            l = l * alpha + jnp.sum(p, axis=0, keepdims=True)
            acc = acc * alpha + jnp.dot(vtj, p.astype(BF16), preferred_element_type=F32)
            out += [m_new, l, acc]
        return tuple(out)

    fin = lax.fori_loop(0, qi, body, tuple(init))
    head_row = lax.broadcasted_iota(jnp.int32, (PAIR, blk), 0) >> 6
    out_t = jnp.zeros((PAIR, blk), F32)
    for h in range(2):
        m, l, acc = fin[3 * h:3 * h + 3]
        out_t = out_t + jnp.where(head_row == h, acc / l, 0.0)
    o_ref[0] = out_t.T


def _moba(q, k, v):
    b, s, _ = q.shape
    nb = s // MOBA_BLOCK
    kv_spec = pl.BlockSpec((1, s, PAIR), lambda i, p, j: (i, 0, p))
    return pl.pallas_call(
        _moba_kernel,
        grid=(b, N_PAIRS, nb),
        in_specs=[pl.BlockSpec((1, MOBA_BLOCK, PAIR), lambda i, p, j: (i, j, p)),
                  kv_spec, kv_spec],
        out_specs=pl.BlockSpec((1, MOBA_BLOCK, PAIR), lambda i, p, j: (i, j, p)),
        out_shape=jax.ShapeDtypeStruct((b, s, MIX_WIDTH), F32),
        scratch_shapes=[pltpu.VMEM((s, PAIR), BF16),
                        pltpu.VMEM((PAIR, s), BF16),
                        pltpu.VMEM((nb, PAIR), F32),
                        pltpu.VMEM((2, nb, MOBA_BLOCK), F32)],
        compiler_params=pltpu.CompilerParams(
            dimension_semantics=("arbitrary", "arbitrary", "arbitrary"),
            vmem_limit_bytes=VMEM_LIMIT),
        name="moba",
    )(q, k, v)


def _outproj_kernel(mix_ref, z_ref, mq_ref, mkv_ref, h_ref, wmix_ref, wmem_ref, fg_ref,
                    o_ref, *, final_norm):
    mq = mq_ref[0]
    mkv = mkv_ref[0]
    mk = mkv[:, :MEM_WIDTH].astype(BF16)
    mv = mkv[:, MEM_WIDTH:].astype(BF16)
    lane = lax.broadcasted_iota(jnp.int32, (1, MEM_WIDTH), 1)
    memo = jnp.zeros(mq.shape, F32)
    for h in range(MEM_HEADS):
        mh = ((lane >> 6) == h).astype(F32)
        logits = _dot_nt(mq * (mh * QK_SCALE), mk)
        e = jnp.exp(logits - jnp.max(logits, axis=-1, keepdims=True))
        p = e / jnp.sum(e, axis=-1, keepdims=True)
        memo = memo + jnp.dot(p.astype(BF16), mv, preferred_element_type=F32) * mh
    z = z_ref[0]
    gate = z * _sigmoid(z)
    y_mix = mix_ref[0] * gate[:, :MIX_WIDTH]
    y_mem = memo * gate[:, MIX_WIDTH:]
    out = h_ref[0] + _dot(y_mix, wmix_ref[...]) + _dot(y_mem, wmem_ref[...])
    if final_norm:
        out = out * _rms_scale(out) * fg_ref[...]
    o_ref[0] = out


def _outproj(mix, z, mq, mkv, h, wmix, wmem, fg, tm, final_norm):
    b, s, _ = h.shape
    tile = lambda n: pl.BlockSpec((1, tm, n), lambda i, j: (i, j, 0))
    return pl.pallas_call(
        functools.partial(_outproj_kernel, final_norm=final_norm),
        grid=(b, s // tm),
        in_specs=[tile(MIX_WIDTH), tile(GATE_WIDTH), tile(MEM_WIDTH),
                  pl.BlockSpec((1, N_MEM, 2 * MEM_WIDTH), lambda i, j: (i, 0, 0)),
                  tile(D_MODEL),
                  pl.BlockSpec((MIX_WIDTH, D_MODEL), lambda i, j: (0, 0)),
                  pl.BlockSpec((MEM_WIDTH, D_MODEL), lambda i, j: (0, 0)),
                  pl.BlockSpec((1, D_MODEL), lambda i, j: (0, 0))],
        out_specs=tile(D_MODEL),
        out_shape=jax.ShapeDtypeStruct((b, s, D_MODEL), F32),
        compiler_params=pltpu.CompilerParams(vmem_limit_bytes=VMEM_LIMIT),
        name="outproj_final" if final_norm else "outproj",
    )(mix, z, mq, mkv, h, wmix, wmem, fg)


def _row(v):
    return v.reshape(1, -1).astype(F32)


def kernel(x, mem, positions, norm_0, w_in_0, conv_w_0, a_log_0, dt_bias_0, o_norm_0,
           mem_norm_0, w_mem_kv_0, w_out_0, norm_1, w_in_1, mem_norm_1, w_mem_kv_1,
           w_out_1, final_norm):
    b, s, _ = x.shape
    assert s % MOBA_BLOCK == 0 and s // MOBA_BLOCK >= MOBA_TOPK
    tm = 256

    mkv0, mkv1 = _memkv(mem, _row(mem_norm_0), w_mem_kv_0.astype(BF16),
                        _row(mem_norm_1), w_mem_kv_1.astype(BF16))

    i1 = 3 * MIX_WIDTH
    i2 = i1 + GATE_WIDTH
    i3 = i2 + MEM_WIDTH
    w0 = w_in_0.astype(BF16)
    wba = jnp.pad(w0[:, i3:], ((0, 0), (0, LANES - 2 * MIX_HEADS)))
    qkv, z0, mq0, ba = _inproj0(x, _row(norm_0), w0[:, :i1], w0[:, i1:i2], w0[:, i2:i3], wba, tm)
    o0 = _delta_mixer(qkv, ba, conv_w_0.astype(F32),
                      _row(jnp.repeat(a_log_0, HEAD_DIM)),
                      _row(jnp.repeat(dt_bias_0, HEAD_DIM)),
                      _row(jnp.tile(o_norm_0, 2)))
    wo0 = w_out_0.astype(BF16)
    h1 = _outproj(o0, z0, mq0, mkv0, x, wo0[:MIX_WIDTH], wo0[MIX_WIDTH:], _row(final_norm),
                  tm, final_norm=False)

    w1 = w_in_1.astype(BF16)
    half = HEAD_DIM // 2
    inv_freq = ROPE_THETA ** (-jnp.arange(half, dtype=F32) * (2.0 / HEAD_DIM))
    invf = _row(jnp.tile(inv_freq, LANES // half))
    q1, k1, v1, z1, mq1 = _inproj1(
        h1, _row(norm_1), positions.reshape(b, s, 1), invf,
        w1[:, :MIX_WIDTH], w1[:, MIX_WIDTH:2 * MIX_WIDTH], w1[:, 2 * MIX_WIDTH:i1],
        w1[:, i1:i2], w1[:, i2:], tm)
    o1 = _moba(q1, k1, v1)
    wo1 = w_out_1.astype(BF16)
    return _outproj(o1, z1, mq1, mkv1, h1, wo1[:MIX_WIDTH], wo1[MIX_WIDTH:], _row(final_norm),
                    tm, final_norm=True)
```

```python
import functools

import jax
import jax.numpy as jnp
from jax import lax
from jax.experimental import pallas as pl
from jax.experimental.pallas import tpu as pltpu

D_MODEL = 1024
HEAD_DIM = 64
MIX_HEADS = 12
MEM_HEADS = 4
MIX_WIDTH = MIX_HEADS * HEAD_DIM
MEM_WIDTH = MEM_HEADS * HEAD_DIM
GATE_WIDTH = MIX_WIDTH + MEM_WIDTH
N_MEM = 256
CONV_WIDTH = 4
MOBA_BLOCK = 256
MOBA_TOPK = 3
ROPE_THETA = 10000.0
NORM_EPS = 1e-6
MASK_VALUE = -1e30

LANES = 128
SUBLANES = 8
PAIR = 2 * HEAD_DIM
N_PAIRS = MIX_HEADS // 2
DELTA_CHUNK = 128
DELTA_GROUP = 4
INV_BASE = 16
QK_SCALE = HEAD_DIM ** -0.5
LOG2E = 1.4426950408889634
VT_ROWS = HEAD_DIM + 16
VMEM_LIMIT = 56 * 1024 * 1024

BF16 = jnp.bfloat16
F32 = jnp.float32

_NT = (((1,), (1,)), ((), ()))


def _dot(a, b):
    return jnp.dot(a.astype(BF16), b.astype(BF16), preferred_element_type=F32)


def _dot_nt(a, b):
    return lax.dot_general(a.astype(BF16), b.astype(BF16), _NT, preferred_element_type=F32)


def _dot_hi(a, b):
    return jnp.dot(a, b, precision=lax.Precision.HIGHEST, preferred_element_type=F32)


def _dot_nt_hi(a, b):
    return lax.dot_general(a, b, _NT, precision=lax.Precision.HIGHEST,
                           preferred_element_type=F32)


def _rms_scale(x):
    return lax.rsqrt(jnp.mean(x * x, axis=-1, keepdims=True) + NORM_EPS)


def _sigmoid(x):
    return 1.0 / (1.0 + jnp.exp(-x))


def _softplus(x):
    return jnp.maximum(x, 0.0) + jnp.log(1.0 + jnp.exp(-jnp.abs(x)))


def _head_masks():
    lane = lax.broadcasted_iota(jnp.int32, (1, PAIR), 1)
    return [((lane >> 6) == h).astype(F32) for h in range(2)]


def _memkv_kernel(mem_ref, g0_ref, w0_ref, g1_ref, w1_ref, o0_ref, o1_ref):
    m = mem_ref[0]
    mn = m * _rms_scale(m)
    o0_ref[0] = _dot(mn * g0_ref[...], w0_ref[...])
    o1_ref[0] = _dot(mn * g1_ref[...], w1_ref[...])


def _memkv(mem, g0, w0, g1, w1):
    b = mem.shape[0]
    full = lambda shp: pl.BlockSpec(shp, lambda i: (0,) * len(shp))
    return pl.pallas_call(
        _memkv_kernel,
        grid=(b,),
        in_specs=[pl.BlockSpec((1, N_MEM, D_MODEL), lambda i: (i, 0, 0)),
                  full((1, D_MODEL)), full((D_MODEL, 2 * MEM_WIDTH)),
                  full((1, D_MODEL)), full((D_MODEL, 2 * MEM_WIDTH))],
        out_specs=[pl.BlockSpec((1, N_MEM, 2 * MEM_WIDTH), lambda i: (i, 0, 0))] * 2,
        out_shape=[jax.ShapeDtypeStruct((b, N_MEM, 2 * MEM_WIDTH), F32)] * 2,
        compiler_params=pltpu.CompilerParams(vmem_limit_bytes=VMEM_LIMIT),
        name="memkv",
    )(mem, g0, w0, g1, w1)


def _inproj0_kernel(x_ref, g_ref, wqkv_ref, wz_ref, wmq_ref, wba_ref,
                    qkv_ref, z_ref, mq_ref, ba_ref):
    x = x_ref[0]
    xn = (x * _rms_scale(x) * g_ref[...]).astype(BF16)
    for w_ref, o_ref in ((wqkv_ref, qkv_ref), (wz_ref, z_ref),
                         (wmq_ref, mq_ref), (wba_ref, ba_ref)):
        o_ref[0] = jnp.dot(xn, w_ref[...], preferred_element_type=F32)


def _inproj0(x, g, wqkv, wz, wmq, wba, tm):
    b, s, _ = x.shape
    widths = (3 * MIX_WIDTH, GATE_WIDTH, MEM_WIDTH, LANES)
    wspec = lambda n: pl.BlockSpec((D_MODEL, n), lambda i, j: (0, 0))
    return pl.pallas_call(
        _inproj0_kernel,
        grid=(b, s // tm),
        in_specs=[pl.BlockSpec((1, tm, D_MODEL), lambda i, j: (i, j, 0)),
                  pl.BlockSpec((1, D_MODEL), lambda i, j: (0, 0))]
                 + [wspec(n) for n in widths],
        out_specs=[pl.BlockSpec((1, tm, n), lambda i, j: (i, j, 0)) for n in widths],
        out_shape=[jax.ShapeDtypeStruct((b, s, n), F32) for n in widths],
        compiler_params=pltpu.CompilerParams(vmem_limit_bytes=VMEM_LIMIT),
        name="inproj0",
    )(x, g, wqkv, wz, wmq, wba)


def _inproj1_kernel(x_ref, g_ref, pos_ref, invf_ref, wq_ref, wk_ref, wv_ref, wz_ref, wmq_ref,
                    q_ref, k_ref, v_ref, z_ref, mq_ref):
    x = x_ref[0]
    xn = (x * _rms_scale(x) * g_ref[...]).astype(BF16)
    ang = pos_ref[0].astype(F32) * invf_ref[...]
    cos = jnp.cos(ang)
    sin = jnp.sin(ang)
    lane = lax.broadcasted_iota(jnp.int32, ang.shape, 1)
    first_half = (lane & (HEAD_DIM - 1)) < (HEAD_DIM // 2)
    sin_signed = jnp.where(first_half, -sin, sin)
    for w_ref, o_ref in ((wq_ref, q_ref), (wk_ref, k_ref)):
        y = jnp.dot(xn, w_ref[...], preferred_element_type=F32)
        for sl in range(MIX_WIDTH // LANES):
            ys = y[:, sl * LANES:(sl + 1) * LANES]
            rot = jnp.where(first_half,
                            pltpu.roll(ys, LANES - HEAD_DIM // 2, 1),
                            pltpu.roll(ys, HEAD_DIM // 2, 1))
            o_ref[0, :, sl * LANES:(sl + 1) * LANES] = ys * cos + rot * sin_signed
    for w_ref, o_ref in ((wv_ref, v_ref), (wz_ref, z_ref), (wmq_ref, mq_ref)):
        o_ref[0] = jnp.dot(xn, w_ref[...], preferred_element_type=F32)


def _inproj1(x, g, pos3, invf, wq, wk, wv, wz, wmq, tm):
    b, s, _ = x.shape
    widths = (MIX_WIDTH, MIX_WIDTH, MIX_WIDTH, GATE_WIDTH, MEM_WIDTH)
    wspec = lambda n: pl.BlockSpec((D_MODEL, n), lambda i, j: (0, 0))
    return pl.pallas_call(
        _inproj1_kernel,
        grid=(b, s // tm),
        in_specs=[pl.BlockSpec((1, tm, D_MODEL), lambda i, j: (i, j, 0)),
                  pl.BlockSpec((1, D_MODEL), lambda i, j: (0, 0)),
                  pl.BlockSpec((1, tm, 1), lambda i, j: (i, j, 0)),
                  pl.BlockSpec((1, LANES), lambda i, j: (0, 0))]
                 + [wspec(n) for n in widths],
        out_specs=[pl.BlockSpec((1, tm, n), lambda i, j: (i, j, 0)) for n in widths],
        out_shape=[jax.ShapeDtypeStruct((b, s, n), F32) for n in widths],
        compiler_params=pltpu.CompilerParams(vmem_limit_bytes=VMEM_LIMIT),
        name="inproj1",
    )(x, g, pos3, invf, wq, wk, wv, wz, wmq)


def _unit_lower_inverses(lows, row, col, eye):
    c = lows[0].shape[0]
    shift = INV_BASE.bit_length() - 1
    base = (row >> shift) == (col >> shift)
    ns = [jnp.where(base, -low, 0.0) for low in lows]
    xs = [eye + n for n in ns]
    power = 2
    while power < INV_BASE:
        ns = [_dot(n, n) for n in ns]
        xs = [x + _dot(x, n) for x, n in zip(xs, ns)]
        power *= 2
    size = INV_BASE
    while size < c:
        sh = size.bit_length() - 1
        sel = (((row >> (sh + 1)) == (col >> (sh + 1)))
               & (((row >> sh) & 1) == 1) & (((col >> sh) & 1) == 0))
        ys = [_dot(x, jnp.where(sel, low, 0.0)) for x, low in zip(xs, lows)]
        xs = [x - _dot(y, x) for x, y in zip(xs, ys)]
        size *= 2
    return xs


def _delta_kernel(q_ref, k_ref, v_ref, ba_ref, cwq_ref, cwk_ref, cwv_ref,
                  alog_ref, dt_ref, onorm_ref, o_ref,
                  ext_ref, p_ref, qp_ref, n_ref, op_ref, egl_ref):
    c = DELTA_CHUNK
    grp = DELTA_GROUP
    rows = grp * c
    seq = q_ref.shape[1]
    pair = pl.program_id(1)
    masks = _head_masks()
    row = lax.broadcasted_iota(jnp.int32, (c, c), 0)
    col = lax.broadcasted_iota(jnp.int32, (c, c), 1)
    tri_incl = row >= col
    tri_strict = row > col
    eye = (row == col).astype(F32)
    tril_b = tri_incl.astype(BF16)
    head_block = ((row >> 6) == (col >> 6)).astype(F32)
    head_block_b = head_block.astype(BF16)
    row2 = lax.broadcasted_iota(jnp.int32, (LANES, 2 * PAIR), 0)
    col2 = lax.broadcasted_iota(jnp.int32, (LANES, 2 * PAIR), 1)
    expand = (row2 == jnp.where(col2 < PAIR, 0, MIX_HEADS) + 2 * pair + ((col2 >> 6) & 1)).astype(BF16)
    lane_n = lax.broadcasted_iota(jnp.int32, (1, LANES), 1)
    is_beta = lane_n < MIX_HEADS
    neg_a = -jnp.exp(alog_ref[...])
    dt_row = dt_ref[...]
    onorm = onorm_ref[...]

    ext_ref[:, 0:SUBLANES, :] = jnp.zeros((3, SUBLANES, PAIR), F32)

    def conv_silu(idx, x_ref, cw_ref, t0):
        raw = x_ref[0, pl.ds(t0, rows), :]
        ext_ref[idx, SUBLANES:SUBLANES + rows, :] = raw
        acc = raw * cw_ref[CONV_WIDTH - 1:CONV_WIDTH, :]
        for tap in range(CONV_WIDTH - 1):
            back = CONV_WIDTH - 1 - tap
            acc = acc + ext_ref[idx, SUBLANES - back:SUBLANES - back + rows, :] * cw_ref[tap:tap + 1, :]
        ext_ref[idx, 0:SUBLANES, :] = raw[rows - SUBLANES:rows, :]
        return acc * _sigmoid(acc)

    def l2norm(x):
        return x * lax.rsqrt(_dot(x * x, head_block_b) + NORM_EPS)

    def build_group(gi, carry):
        t0 = pl.multiple_of(gi * rows, rows)
        qn = l2norm(conv_silu(0, q_ref, cwq_ref, t0))
        kn = l2norm(conv_silu(1, k_ref, cwk_ref, t0))
        vc = conv_silu(2, v_ref, cwv_ref, t0)
        ba = ba_ref[0, pl.ds(t0, rows), :]
        act = jnp.where(is_beta, _sigmoid(ba), neg_a * _softplus(ba + dt_row))
        bg = jnp.dot(act.astype(BF16), expand, preferred_element_type=F32)
        beta = bg[:, :PAIR]
        g = bg[:, PAIR:]
        kb = kn * beta
        vb = vc * beta
        sls = [slice(ci * c, (ci + 1) * c) for ci in range(grp)]
        gcs = [jnp.dot(tril_b, g[sl].astype(BF16), preferred_element_type=F32) for sl in sls]
        gcts = [gc.T for gc in gcs]
        egs = [jnp.exp(gc) for gc in gcs]
        probs = [(ci, h) for ci in range(grp) for h in range(2)]
        decays = []
        for ci, h in probs:
            gcol = jnp.broadcast_to(gcs[ci][:, h * HEAD_DIM:h * HEAD_DIM + 1], (c, c))
            grow = gcts[ci][h * HEAD_DIM:h * HEAD_DIM + 1, :]
            decays.append(jnp.exp(jnp.where(tri_incl, gcol - grow, -jnp.inf)))
        kbhs = [kb[sls[ci]] * masks[h] for ci, h in probs]
        kks = [_dot_nt(kbh, kn[sls[ci]]) for kbh, (ci, h) in zip(kbhs, probs)]
        qks = [_dot_nt(qn[sls[ci]] * (masks[h] * QK_SCALE), kn[sls[ci]]) for ci, h in probs]
        lows = [jnp.where(tri_strict, kk * d, 0.0) for kk, d in zip(kks, decays)]
        aqks = [jnp.where(tri_incl, qk * d, 0.0) for qk, d in zip(qks, decays)]
        tinvs = _unit_lower_inverses(lows, row, col, eye)
        wus = [_dot(t, jnp.concatenate([kbh * egs[ci], vb[sls[ci]] * masks[h]], axis=1))
               for t, kbh, (ci, h) in zip(tinvs, kbhs, probs)]
        aus = [_dot(a, wu) for a, wu in zip(aqks, wus)]
        kds = [kn[sls[ci]] * jnp.exp(gcs[ci][c - 1:c, :] - gcs[ci]) for ci in range(grp)]
        pns = [_dot(kds[ci].T, wus[2 * ci] + wus[2 * ci + 1]) for ci in range(grp)]
        for ci in range(grp):
            cidx = gi * grp + ci
            au = aus[2 * ci] + aus[2 * ci + 1]
            p_ref[cidx] = (head_block * pns[ci][:, :PAIR]).astype(BF16)
            n_ref[cidx] = head_block * pns[ci][:, PAIR:]
            qp_ref[cidx] = (qn[sls[ci]] * (QK_SCALE * egs[ci]) - au[:, :PAIR]).astype(BF16)
            op_ref[cidx] = au[:, PAIR:]
            egl_ref[pl.ds(cidx, 1), :] = jnp.exp(gcs[ci][c - 1:c, :])
        return carry

    lax.fori_loop(0, seq // rows, build_group, 0)

    def scan_step(ci, state):
        sb = state.astype(BF16)
        o = jnp.dot(qp_ref[ci], sb, preferred_element_type=F32) + op_ref[ci]
        new_state = (state * egl_ref[pl.ds(ci, 1), :]
                     - jnp.dot(p_ref[ci], sb, preferred_element_type=F32) + n_ref[ci])
        ms = _dot(o * o, head_block_b) * (1.0 / HEAD_DIM)
        o_ref[0, pl.ds(pl.multiple_of(ci * c, c), c), :] = o * lax.rsqrt(ms + NORM_EPS) * onorm
        return new_state

    lax.fori_loop(0, seq // c, scan_step, jnp.zeros((PAIR, PAIR), F32))


def _delta_mixer(qkv, ba, conv_w, alog_row, dt_row, onorm_row):
    b, s, _ = qkv.shape
    n_chunks = s // DELTA_CHUNK
    seq_spec = lambda off: pl.BlockSpec((1, s, PAIR), lambda i, p, off=off: (i, 0, off + p))
    cw_spec = lambda off: pl.BlockSpec((CONV_WIDTH, PAIR), lambda i, p, off=off: (0, off + p))
    row_spec = pl.BlockSpec((1, LANES), lambda i, p: (0, 0))
    return pl.pallas_call(
        _delta_kernel,
        grid=(b, N_PAIRS),
        in_specs=[seq_spec(0), seq_spec(N_PAIRS), seq_spec(2 * N_PAIRS),
                  pl.BlockSpec((1, s, LANES), lambda i, p: (i, 0, 0)),
                  cw_spec(0), cw_spec(N_PAIRS), cw_spec(2 * N_PAIRS),
                  row_spec, row_spec, row_spec],
        out_specs=pl.BlockSpec((1, s, PAIR), lambda i, p: (i, 0, p)),
        out_shape=jax.ShapeDtypeStruct((b, s, MIX_WIDTH), F32),
        scratch_shapes=[pltpu.VMEM((3, SUBLANES + DELTA_GROUP * DELTA_CHUNK, PAIR), F32),
                        pltpu.VMEM((n_chunks, PAIR, PAIR), BF16),
                        pltpu.VMEM((n_chunks, DELTA_CHUNK, PAIR), BF16),
                        pltpu.VMEM((n_chunks, PAIR, PAIR), F32),
                        pltpu.VMEM((n_chunks, DELTA_CHUNK, PAIR), F32),
                        pltpu.VMEM((n_chunks, PAIR), F32)],
        compiler_params=pltpu.CompilerParams(vmem_limit_bytes=VMEM_LIMIT),
        name="delta_mixer",
    )(qkv, qkv, qkv, ba, conv_w, conv_w, conv_w, alog_row, dt_row, onorm_row)


def _moba_kernel(q_ref, k_ref, v_ref, o_ref, kaug_ref, vt_ref, kmean_ref, sa_ref, sb_ref):
    blk = MOBA_BLOCK
    seq = k_ref.shape[1]
    nb = seq // blk
    qi = pl.program_id(2)
    masks = _head_masks()

    @pl.when(qi == 0)
    def _prepare():
        lane = lax.broadcasted_iota(jnp.int32, (blk, LANES), 1)
        ones_row = (lax.broadcasted_iota(jnp.int32, (VT_ROWS - HEAD_DIM, LANES), 0) == 0).astype(BF16)

        def body(j, carry):
            r0 = pl.multiple_of(j * blk, blk)
            kj = k_ref[0, pl.ds(r0, blk), :]
            vj = v_ref[0, pl.ds(r0, blk), :]
            kaug_ref[pl.ds(r0, blk), 0:LANES] = kj.astype(BF16)
            kaug_ref[pl.ds(r0, blk), LANES:2 * LANES] = (lane == j).astype(BF16)
            for t in range(blk // LANES):
                c0 = pl.multiple_of(r0 + t * LANES, LANES)
                vt = vj[t * LANES:(t + 1) * LANES, :].T.astype(BF16)
                for h in range(2):
                    vt_ref[h, 0:HEAD_DIM, pl.ds(c0, LANES)] = vt[h * HEAD_DIM:(h + 1) * HEAD_DIM, :]
                    vt_ref[h, HEAD_DIM:VT_ROWS, pl.ds(c0, LANES)] = ones_row
            kmean_ref[pl.ds(j, 1), :] = jnp.sum(kj, axis=0, keepdims=True) * (1.0 / blk)
            return carry
        lax.fori_loop(0, nb, body, 0)

    qw = 2 * blk
    q = q_ref[0]
    kmean = kmean_ref[...]
    blk_id = lax.broadcasted_iota(jnp.int32, (nb, qw), 0)
    own_blk = 2 * qi + (lax.broadcasted_iota(jnp.int32, (nb, qw), 1) >> (blk.bit_length() - 1))
    valid = blk_id < own_blk
    qs = []
    for h in range(2):
        gate = _dot_nt_hi(kmean * masks[h], q)
        gm = jnp.where(valid, gate, -jnp.inf)
        rank = jnp.zeros((nb, qw), jnp.int32)
        for j2 in range(nb):
            gj = gm[j2:j2 + 1, :]
            beats = (gj > gm) | ((gj == gm) & (blk_id > j2))
            rank = rank + beats.astype(jnp.int32)
        open_blk = (valid & (rank < MOBA_TOPK)) | (blk_id == own_blk)
        bias = jnp.where(open_blk, 0.0, MASK_VALUE)
        bias = jnp.concatenate([bias, jnp.zeros((LANES - nb, qw), F32)], axis=0)
        bias_t = jnp.concatenate([bias[:, t * LANES:(t + 1) * LANES].T for t in range(qw // LANES)],
                                 axis=0)
        qs.append(jnp.concatenate([(q * (masks[h] * (QK_SCALE * LOG2E))).astype(BF16),
                                   bias_t.astype(BF16)], axis=1))

    key_pos = lax.broadcasted_iota(jnp.int32, (blk, blk), 0)
    qry_pos = lax.broadcasted_iota(jnp.int32, (blk, blk), 1)
    always = jnp.int32(1 << 20)

    def scores(t, s_ref):
        r0 = pl.multiple_of(t * (2 * blk), 2 * blk)
        k2 = kaug_ref[pl.ds(r0, 2 * blk), :]
        sts = [lax.dot_general(k2, qs[h], _NT, preferred_element_type=F32) for h in range(2)]
        causal = key_pos <= qry_pos + jnp.where(t == qi, 0, always)
        maxima = []
        for h, st in enumerate(sts):
            st = jnp.concatenate(
                [jnp.concatenate([jnp.where(causal, st[:blk, :blk], MASK_VALUE), st[:blk, blk:]], axis=1),
                 jnp.concatenate([st[blk:, :blk], jnp.where(causal, st[blk:, blk:], MASK_VALUE)], axis=1)],
                axis=0)
            s_ref[h] = st
            maxima.append(jnp.max(st, axis=0, keepdims=True))
        return maxima

    def update(t, state, maxima, s_ref):
        r0 = pl.multiple_of(t * (2 * blk), 2 * blk)
        m_new = [jnp.maximum(state[2 * h], maxima[h]) for h in range(2)]
        ps = [jnp.exp2(s_ref[h] - m_new[h]).astype(BF16) for h in range(2)]
        pvs = [jnp.dot(vt_ref[h, :, pl.ds(r0, 2 * blk)], ps[h], preferred_element_type=F32)
               for h in range(2)]
        out = []
        for h in range(2):
            alpha = jnp.exp2(state[2 * h] - m_new[h])
            out += [m_new[h], state[2 * h + 1] * alpha + pvs[h]]
        return out

    init = []
    for h in range(2):
        init += [jnp.full((1, qw), MASK_VALUE, F32), jnp.zeros((VT_ROWS, qw), F32)]

    def two_pairs(u, carry):
        t0 = 2 * u
        max_b = scores(t0 + 1, sb_ref)
        state = update(t0, carry[:4], carry[4:], sa_ref)
        max_a = scores(t0 + 2, sa_ref)
        state = update(t0 + 1, state, max_b, sb_ref)
        return tuple(state + max_a)

    carry = lax.fori_loop(0, qi // 2, two_pairs, tuple(init + scores(0, sa_ref)))

    def odd_tail(carry):
        max_b = scores(qi, sb_ref)
        state = update(qi - 1, carry[:4], carry[4:], sa_ref)
        return tuple(update(qi, state, max_b, sb_ref))

    def even_tail(carry):
        return tuple(update(qi, carry[:4], carry[4:], sa_ref))

    fin = lax.cond((qi & 1) == 1, odd_tail, even_tail, carry)

    out_t = jnp.concatenate(
        [fin[2 * h + 1][:HEAD_DIM] / fin[2 * h + 1][HEAD_DIM:HEAD_DIM + 1] for h in range(2)], axis=0)
    o_ref[0] = out_t.T


def _moba(q, k, v):
    b, s, _ = q.shape
    nb = s // MOBA_BLOCK
    kv_spec = pl.BlockSpec((1, s, PAIR), lambda i, p, j: (i, 0, p))
    return pl.pallas_call(
        _moba_kernel,
        grid=(b, N_PAIRS, nb // 2),
        in_specs=[pl.BlockSpec((1, 2 * MOBA_BLOCK, PAIR), lambda i, p, j: (i, j, p)),
                  kv_spec, kv_spec],
        out_specs=pl.BlockSpec((1, 2 * MOBA_BLOCK, PAIR), lambda i, p, j: (i, j, p)),
        out_shape=jax.ShapeDtypeStruct((b, s, MIX_WIDTH), F32),
        scratch_shapes=[pltpu.VMEM((s, 2 * LANES), BF16),
                        pltpu.VMEM((2, VT_ROWS, s), BF16),
                        pltpu.VMEM((nb, PAIR), F32),
                        pltpu.VMEM((2, 2 * MOBA_BLOCK, 2 * MOBA_BLOCK), F32),
                        pltpu.VMEM((2, 2 * MOBA_BLOCK, 2 * MOBA_BLOCK), F32)],
        compiler_params=pltpu.CompilerParams(
            dimension_semantics=("arbitrary", "arbitrary", "arbitrary"),
            vmem_limit_bytes=VMEM_LIMIT),
        name="moba",
    )(q, k, v)


def _outproj_kernel(mix_ref, z_ref, mq_ref, mkv_ref, h_ref, wmix_ref, wmem_ref, fg_ref,
                    o_ref, *, final_norm):
    mq = mq_ref[0]
    mkv = mkv_ref[0]
    mk = mkv[:, :MEM_WIDTH].astype(BF16)
    mv = mkv[:, MEM_WIDTH:].astype(BF16)
    lane = lax.broadcasted_iota(jnp.int32, (1, MEM_WIDTH), 1)
    memo = jnp.zeros(mq.shape, F32)
    for h in range(MEM_HEADS):
        mh = ((lane >> 6) == h).astype(F32)
        logits = _dot_nt(mq * (mh * QK_SCALE), mk)
        e = jnp.exp(logits - jnp.max(logits, axis=-1, keepdims=True))
        p = e / jnp.sum(e, axis=-1, keepdims=True)
        memo = memo + jnp.dot(p.astype(BF16), mv, preferred_element_type=F32) * mh
    z = z_ref[0]
    gate = z * _sigmoid(z)
    y_mix = mix_ref[0] * gate[:, :MIX_WIDTH]
    y_mem = memo * gate[:, MIX_WIDTH:]
    out = h_ref[0] + _dot(y_mix, wmix_ref[...]) + _dot(y_mem, wmem_ref[...])
    if final_norm:
        out = out * _rms_scale(out) * fg_ref[...]
    o_ref[0] = out


def _outproj(mix, z, mq, mkv, h, wmix, wmem, fg, tm, final_norm):
    b, s, _ = h.shape
    tile = lambda n: pl.BlockSpec((1, tm, n), lambda i, j: (i, j, 0))
    return pl.pallas_call(
        functools.partial(_outproj_kernel, final_norm=final_norm),
        grid=(b, s // tm),
        in_specs=[tile(MIX_WIDTH), tile(GATE_WIDTH), tile(MEM_WIDTH),
                  pl.BlockSpec((1, N_MEM, 2 * MEM_WIDTH), lambda i, j: (i, 0, 0)),
                  tile(D_MODEL),
                  pl.BlockSpec((MIX_WIDTH, D_MODEL), lambda i, j: (0, 0)),
                  pl.BlockSpec((MEM_WIDTH, D_MODEL), lambda i, j: (0, 0)),
                  pl.BlockSpec((1, D_MODEL), lambda i, j: (0, 0))],
        out_specs=tile(D_MODEL),
        out_shape=jax.ShapeDtypeStruct((b, s, D_MODEL), F32),
        compiler_params=pltpu.CompilerParams(vmem_limit_bytes=VMEM_LIMIT),
        name="outproj_final" if final_norm else "outproj",
    )(mix, z, mq, mkv, h, wmix, wmem, fg)


def _row(v):
    return v.reshape(1, -1).astype(F32)


def kernel(x, mem, positions, norm_0, w_in_0, conv_w_0, a_log_0, dt_bias_0, o_norm_0,
           mem_norm_0, w_mem_kv_0, w_out_0, norm_1, w_in_1, mem_norm_1, w_mem_kv_1,
           w_out_1, final_norm):
    b, s, _ = x.shape
    assert s % (2 * MOBA_BLOCK) == 0 and s // MOBA_BLOCK >= MOBA_TOPK
    assert s % (DELTA_GROUP * DELTA_CHUNK) == 0
    tm = 256

    mkv0, mkv1 = _memkv(mem, _row(mem_norm_0), w_mem_kv_0.astype(BF16),
                        _row(mem_norm_1), w_mem_kv_1.astype(BF16))

    i1 = 3 * MIX_WIDTH
    i2 = i1 + GATE_WIDTH
    i3 = i2 + MEM_WIDTH
    w0 = w_in_0.astype(BF16)
    wba = jnp.pad(w0[:, i3:], ((0, 0), (0, LANES - 2 * MIX_HEADS)))
    qkv, z0, mq0, ba = _inproj0(x, _row(norm_0), w0[:, :i1], w0[:, i1:i2], w0[:, i2:i3], wba, tm)
    ba_cols = lambda v: _row(jnp.pad(v, (MIX_HEADS, LANES - 2 * MIX_HEADS)))
    o0 = _delta_mixer(qkv, ba, conv_w_0.astype(F32), ba_cols(a_log_0), ba_cols(dt_bias_0),
                      _row(jnp.tile(o_norm_0, 2)))
    wo0 = w_out_0.astype(BF16)
    h1 = _outproj(o0, z0, mq0, mkv0, x, wo0[:MIX_WIDTH], wo0[MIX_WIDTH:], _row(final_norm),
                  tm, final_norm=False)

    w1 = w_in_1.astype(BF16)
    half = HEAD_DIM // 2
    inv_freq = ROPE_THETA ** (-jnp.arange(half, dtype=F32) * (2.0 / HEAD_DIM))
    invf = _row(jnp.tile(inv_freq, LANES // half))
    q1, k1, v1, z1, mq1 = _inproj1(
        h1, _row(norm_1), positions.reshape(b, s, 1), invf,
        w1[:, :MIX_WIDTH], w1[:, MIX_WIDTH:2 * MIX_WIDTH], w1[:, 2 * MIX_WIDTH:i1],
        w1[:, i1:i2], w1[:, i2:], tm)
    o1 = _moba(q1, k1, v1)
    wo1 = w_out_1.astype(BF16)
    return _outproj(o1, z1, mq1, mkv1, h1, wo1[:MIX_WIDTH], wo1[MIX_WIDTH:], _row(final_norm),
                    tm, final_norm=True)
```

```python
import functools

import jax
import jax.numpy as jnp
from jax import lax
from jax.experimental import pallas as pl
from jax.experimental.pallas import tpu as pltpu

D_MODEL = 1024
HEAD_DIM = 64
MIX_HEADS = 12
MEM_HEADS = 4
MIX_WIDTH = MIX_HEADS * HEAD_DIM
MEM_WIDTH = MEM_HEADS * HEAD_DIM
GATE_WIDTH = MIX_WIDTH + MEM_WIDTH
N_MEM = 256
CONV_WIDTH = 4
MOBA_BLOCK = 256
MOBA_TOPK = 3
ROPE_THETA = 10000.0
NORM_EPS = 1e-6
MASK_VALUE = -1e30

LANES = 128
SUBLANES = 8
PAIR = 2 * HEAD_DIM
N_PAIRS = MIX_HEADS // 2
DELTA_CHUNK = 128
DELTA_GROUP = 4
INV_BASE = 16
QK_SCALE = HEAD_DIM ** -0.5
LOG2E = 1.4426950408889634
MOBA_PAIRS = 2
NB_PAD = 16
VT_ROWS = HEAD_DIM + 16
VMEM_LIMIT = 56 * 1024 * 1024

BF16 = jnp.bfloat16
F32 = jnp.float32

_NT = (((1,), (1,)), ((), ()))


def _dot(a, b):
    return jnp.dot(a.astype(BF16), b.astype(BF16), preferred_element_type=F32)


def _dot_nt(a, b):
    return lax.dot_general(a.astype(BF16), b.astype(BF16), _NT, preferred_element_type=F32)


def _dot_hi(a, b):
    return jnp.dot(a, b, precision=lax.Precision.HIGHEST, preferred_element_type=F32)


def _dot_nt_hi(a, b):
    return lax.dot_general(a, b, _NT, precision=lax.Precision.HIGHEST,
                           preferred_element_type=F32)


def _rms_scale(x):
    return lax.rsqrt(jnp.mean(x * x, axis=-1, keepdims=True) + NORM_EPS)


def _sigmoid(x):
    return 1.0 / (1.0 + jnp.exp(-x))


def _softplus(x):
    return jnp.maximum(x, 0.0) + jnp.log(1.0 + jnp.exp(-jnp.abs(x)))


def _head_masks():
    lane = lax.broadcasted_iota(jnp.int32, (1, PAIR), 1)
    return [((lane >> 6) == h).astype(F32) for h in range(2)]


def _memkv_kernel(mem_ref, g0_ref, w0_ref, g1_ref, w1_ref, o0_ref, o1_ref):
    m = mem_ref[0]
    mn = m * _rms_scale(m)
    o0_ref[0] = _dot(mn * g0_ref[...], w0_ref[...]).astype(o0_ref.dtype)
    o1_ref[0] = _dot(mn * g1_ref[...], w1_ref[...]).astype(o1_ref.dtype)


def _memkv(mem, g0, w0, g1, w1):
    b = mem.shape[0]
    full = lambda shp: pl.BlockSpec(shp, lambda i: (0,) * len(shp))
    return pl.pallas_call(
        _memkv_kernel,
        grid=(b,),
        in_specs=[pl.BlockSpec((1, N_MEM, D_MODEL), lambda i: (i, 0, 0)),
                  full((1, D_MODEL)), full((D_MODEL, 2 * MEM_WIDTH)),
                  full((1, D_MODEL)), full((D_MODEL, 2 * MEM_WIDTH))],
        out_specs=[pl.BlockSpec((1, N_MEM, 2 * MEM_WIDTH), lambda i: (i, 0, 0))] * 2,
        out_shape=[jax.ShapeDtypeStruct((b, N_MEM, 2 * MEM_WIDTH), BF16)] * 2,
        compiler_params=pltpu.CompilerParams(vmem_limit_bytes=VMEM_LIMIT),
        name="memkv",
    )(mem, g0, w0, g1, w1)


def _inproj0_kernel(x_ref, g_ref, wqkv_ref, wz_ref, wmq_ref, wba_ref,
                    qkv_ref, z_ref, mq_ref, ba_ref):
    x = x_ref[0]
    xn = (x * _rms_scale(x) * g_ref[...]).astype(BF16)
    for w_ref, o_ref in ((wqkv_ref, qkv_ref), (wz_ref, z_ref),
                         (wmq_ref, mq_ref), (wba_ref, ba_ref)):
        o_ref[0] = jnp.dot(xn, w_ref[...], preferred_element_type=F32).astype(o_ref.dtype)


def _inproj0(x, g, wqkv, wz, wmq, wba, tm):
    b, s, _ = x.shape
    widths = (3 * MIX_WIDTH, GATE_WIDTH, MEM_WIDTH, LANES)
    wspec = lambda n: pl.BlockSpec((D_MODEL, n), lambda i, j: (0, 0))
    return pl.pallas_call(
        _inproj0_kernel,
        grid=(b, s // tm),
        in_specs=[pl.BlockSpec((1, tm, D_MODEL), lambda i, j: (i, j, 0)),
                  pl.BlockSpec((1, D_MODEL), lambda i, j: (0, 0))]
                 + [wspec(n) for n in widths],
        out_specs=[pl.BlockSpec((1, tm, n), lambda i, j: (i, j, 0)) for n in widths],
        out_shape=[jax.ShapeDtypeStruct((b, s, n), dt)
                   for n, dt in zip(widths, (BF16, BF16, BF16, F32))],
        compiler_params=pltpu.CompilerParams(vmem_limit_bytes=VMEM_LIMIT),
        name="inproj0",
    )(x, g, wqkv, wz, wmq, wba)


def _inproj1_kernel(x_ref, g_ref, pos_ref, invf_ref, wq_ref, wk_ref, wv_ref, wz_ref, wmq_ref,
                    q_ref, k_ref, v_ref, z_ref, mq_ref):
    x = x_ref[0]
    xn = (x * _rms_scale(x) * g_ref[...]).astype(BF16)
    ang = pos_ref[0].astype(F32) * invf_ref[...]
    cos = jnp.cos(ang)
    sin = jnp.sin(ang)
    lane = lax.broadcasted_iota(jnp.int32, ang.shape, 1)
    first_half = (lane & (HEAD_DIM - 1)) < (HEAD_DIM // 2)
    sin_signed = jnp.where(first_half, -sin, sin)
    for w_ref, o_ref in ((wq_ref, q_ref), (wk_ref, k_ref)):
        y = jnp.dot(xn, w_ref[...], preferred_element_type=F32)
        for sl in range(MIX_WIDTH // LANES):
            ys = y[:, sl * LANES:(sl + 1) * LANES]
            rot = jnp.where(first_half,
                            pltpu.roll(ys, LANES - HEAD_DIM // 2, 1),
                            pltpu.roll(ys, HEAD_DIM // 2, 1))
            o_ref[0, :, sl * LANES:(sl + 1) * LANES] = (ys * cos + rot * sin_signed).astype(o_ref.dtype)
    for w_ref, o_ref in ((wv_ref, v_ref), (wz_ref, z_ref), (wmq_ref, mq_ref)):
        o_ref[0] = jnp.dot(xn, w_ref[...], preferred_element_type=F32).astype(o_ref.dtype)


def _inproj1(x, g, pos3, invf, wq, wk, wv, wz, wmq, tm):
    b, s, _ = x.shape
    widths = (MIX_WIDTH, MIX_WIDTH, MIX_WIDTH, GATE_WIDTH, MEM_WIDTH)
    wspec = lambda n: pl.BlockSpec((D_MODEL, n), lambda i, j: (0, 0))
    return pl.pallas_call(
        _inproj1_kernel,
        grid=(b, s // tm),
        in_specs=[pl.BlockSpec((1, tm, D_MODEL), lambda i, j: (i, j, 0)),
                  pl.BlockSpec((1, D_MODEL), lambda i, j: (0, 0)),
                  pl.BlockSpec((1, tm, 1), lambda i, j: (i, j, 0)),
                  pl.BlockSpec((1, LANES), lambda i, j: (0, 0))]
                 + [wspec(n) for n in widths],
        out_specs=[pl.BlockSpec((1, tm, n), lambda i, j: (i, j, 0)) for n in widths],
        out_shape=[jax.ShapeDtypeStruct((b, s, n), BF16) for n in widths],
        compiler_params=pltpu.CompilerParams(vmem_limit_bytes=VMEM_LIMIT),
        name="inproj1",
    )(x, g, pos3, invf, wq, wk, wv, wz, wmq)


def _unit_lower_inverses(lows, row, col, eye):
    c = lows[0].shape[0]
    shift = INV_BASE.bit_length() - 1
    base = (row >> shift) == (col >> shift)
    ns = [jnp.where(base, -low, 0.0) for low in lows]
    xs = [eye + n for n in ns]
    power = 2
    while power < INV_BASE:
        ns = [_dot(n, n) for n in ns]
        xs = [x + _dot(x, n) for x, n in zip(xs, ns)]
        power *= 2
    size = INV_BASE
    while size < c:
        sh = size.bit_length() - 1
        sel = (((row >> (sh + 1)) == (col >> (sh + 1)))
               & (((row >> sh) & 1) == 1) & (((col >> sh) & 1) == 0))
        ys = [_dot(x, jnp.where(sel, low, 0.0)) for x, low in zip(xs, lows)]
        xs = [x - _dot(y, x) for x, y in zip(xs, ys)]
        size *= 2
    return xs


def _delta_kernel(q_ref, k_ref, v_ref, ba_ref, cwq_ref, cwk_ref, cwv_ref,
                  alog_ref, dt_ref, o_ref,
                  ext_ref, p_ref, qp_ref, n_ref, op_ref, egl_ref):
    c = DELTA_CHUNK
    grp = DELTA_GROUP
    rows = grp * c
    seq = q_ref.shape[1]
    pair = pl.program_id(1)
    masks = _head_masks()
    row = lax.broadcasted_iota(jnp.int32, (c, c), 0)
    col = lax.broadcasted_iota(jnp.int32, (c, c), 1)
    tri_incl = row >= col
    tri_strict = row > col
    eye = (row == col).astype(F32)
    tril_b = tri_incl.astype(BF16)
    head_block = ((row >> 6) == (col >> 6)).astype(F32)
    head_block_b = head_block.astype(BF16)
    row2 = lax.broadcasted_iota(jnp.int32, (LANES, 2 * PAIR), 0)
    col2 = lax.broadcasted_iota(jnp.int32, (LANES, 2 * PAIR), 1)
    expand = (row2 == jnp.where(col2 < PAIR, 0, MIX_HEADS) + 2 * pair + ((col2 >> 6) & 1)).astype(BF16)
    lane_n = lax.broadcasted_iota(jnp.int32, (1, LANES), 1)
    is_beta = lane_n < MIX_HEADS
    neg_a = -jnp.exp(alog_ref[...])
    dt_row = dt_ref[...]

    ext_ref[:, 0:SUBLANES, :] = jnp.zeros((3, SUBLANES, PAIR), F32)

    def conv_silu(idx, x_ref, cw_ref, t0):
        raw = x_ref[0, pl.ds(t0, rows), :].astype(F32)
        ext_ref[idx, SUBLANES:SUBLANES + rows, :] = raw
        acc = raw * cw_ref[CONV_WIDTH - 1:CONV_WIDTH, :]
        for tap in range(CONV_WIDTH - 1):
            back = CONV_WIDTH - 1 - tap
            acc = acc + ext_ref[idx, SUBLANES - back:SUBLANES - back + rows, :] * cw_ref[tap:tap + 1, :]
        ext_ref[idx, 0:SUBLANES, :] = raw[rows - SUBLANES:rows, :]
        return acc * _sigmoid(acc)

    def l2norm(x):
        return x * lax.rsqrt(_dot(x * x, head_block_b) + NORM_EPS)

    def build_group(gi, carry):
        t0 = pl.multiple_of(gi * rows, rows)
        qn = l2norm(conv_silu(0, q_ref, cwq_ref, t0))
        kn = l2norm(conv_silu(1, k_ref, cwk_ref, t0))
        vc = conv_silu(2, v_ref, cwv_ref, t0)
        ba = ba_ref[0, pl.ds(t0, rows), :]
        act = jnp.where(is_beta, _sigmoid(ba), neg_a * _softplus(ba + dt_row))
        bg = jnp.dot(act.astype(BF16), expand, preferred_element_type=F32)
        beta = bg[:, :PAIR]
        g = bg[:, PAIR:]
        kb = kn * beta
        vb = vc * beta
        sls = [slice(ci * c, (ci + 1) * c) for ci in range(grp)]
        gcs = [jnp.dot(tril_b, g[sl].astype(BF16), preferred_element_type=F32) for sl in sls]
        gcts = [gc.T for gc in gcs]
        egs = [jnp.exp(gc) for gc in gcs]
        probs = [(ci, h) for ci in range(grp) for h in range(2)]
        decays = []
        for ci, h in probs:
            gcol = jnp.broadcast_to(gcs[ci][:, h * HEAD_DIM:h * HEAD_DIM + 1], (c, c))
            grow = gcts[ci][h * HEAD_DIM:h * HEAD_DIM + 1, :]
            decays.append(jnp.exp(jnp.where(tri_incl, gcol - grow, -jnp.inf)))
        kbhs = [kb[sls[ci]] * masks[h] for ci, h in probs]
        kks = [_dot_nt(kbh, kn[sls[ci]]) for kbh, (ci, h) in zip(kbhs, probs)]
        qks = [_dot_nt(qn[sls[ci]] * (masks[h] * QK_SCALE), kn[sls[ci]]) for ci, h in probs]
        lows = [jnp.where(tri_strict, kk * d, 0.0) for kk, d in zip(kks, decays)]
        aqks = [jnp.where(tri_incl, qk * d, 0.0) for qk, d in zip(qks, decays)]
        tinvs = _unit_lower_inverses(lows, row, col, eye)
        wus = [_dot(t, jnp.concatenate([kbh * egs[ci], vb[sls[ci]] * masks[h]], axis=1))
               for t, kbh, (ci, h) in zip(tinvs, kbhs, probs)]
        aus = [_dot(a, wu) for a, wu in zip(aqks, wus)]
        kds = [kn[sls[ci]] * jnp.exp(gcs[ci][c - 1:c, :] - gcs[ci]) for ci in range(grp)]
        pns = [_dot(kds[ci].T, wus[2 * ci] + wus[2 * ci + 1]) for ci in range(grp)]
        for ci in range(grp):
            cidx = gi * grp + ci
            au = aus[2 * ci] + aus[2 * ci + 1]
            p_ref[cidx] = (head_block * pns[ci][:, :PAIR]).astype(BF16)
            n_ref[cidx] = head_block * pns[ci][:, PAIR:]
            qp_ref[cidx] = (qn[sls[ci]] * (QK_SCALE * egs[ci]) - au[:, :PAIR]).astype(BF16)
            op_ref[cidx] = au[:, PAIR:]
            egl_ref[pl.ds(cidx, 1), :] = jnp.exp(gcs[ci][c - 1:c, :])
        return carry

    lax.fori_loop(0, seq // rows, build_group, 0)

    def scan_step(ci, state):
        sb = state.astype(BF16)
        o = jnp.dot(qp_ref[ci], sb, preferred_element_type=F32) + op_ref[ci]
        new_state = (state * egl_ref[pl.ds(ci, 1), :]
                     - jnp.dot(p_ref[ci], sb, preferred_element_type=F32) + n_ref[ci])
        o_ref[0, pl.ds(pl.multiple_of(ci * c, c), c), :] = o.astype(o_ref.dtype)
        return new_state

    lax.fori_loop(0, seq // c, scan_step, jnp.zeros((PAIR, PAIR), F32))


def _delta_mixer(qkv, ba, conv_w, alog_row, dt_row):
    b, s, _ = qkv.shape
    n_chunks = s // DELTA_CHUNK
    seq_spec = lambda off: pl.BlockSpec((1, s, PAIR), lambda i, p, off=off: (i, 0, off + p))
    cw_spec = lambda off: pl.BlockSpec((CONV_WIDTH, PAIR), lambda i, p, off=off: (0, off + p))
    row_spec = pl.BlockSpec((1, LANES), lambda i, p: (0, 0))
    return pl.pallas_call(
        _delta_kernel,
        grid=(b, N_PAIRS),
        in_specs=[seq_spec(0), seq_spec(N_PAIRS), seq_spec(2 * N_PAIRS),
                  pl.BlockSpec((1, s, LANES), lambda i, p: (i, 0, 0)),
                  cw_spec(0), cw_spec(N_PAIRS), cw_spec(2 * N_PAIRS),
                  row_spec, row_spec],
        out_specs=pl.BlockSpec((1, s, PAIR), lambda i, p: (i, 0, p)),
        out_shape=jax.ShapeDtypeStruct((b, s, MIX_WIDTH), BF16),
        scratch_shapes=[pltpu.VMEM((3, SUBLANES + DELTA_GROUP * DELTA_CHUNK, PAIR), F32),
                        pltpu.VMEM((n_chunks, PAIR, PAIR), BF16),
                        pltpu.VMEM((n_chunks, DELTA_CHUNK, PAIR), BF16),
                        pltpu.VMEM((n_chunks, PAIR, PAIR), F32),
                        pltpu.VMEM((n_chunks, DELTA_CHUNK, PAIR), F32),
                        pltpu.VMEM((n_chunks, PAIR), F32)],
        compiler_params=pltpu.CompilerParams(vmem_limit_bytes=VMEM_LIMIT),
        name="delta_mixer",
    )(qkv, qkv, qkv, ba, conv_w, conv_w, conv_w, alog_row, dt_row)


def _moba_kernel(q_ref, k_ref, v_ref, o_ref, kaug_ref, vt_ref, kmean_ref, qaug_ref, sa_ref, sb_ref):
    blk = MOBA_BLOCK
    qw = 2 * blk
    seq = k_ref.shape[1]
    nb = seq // blk
    qi = pl.program_id(2)
    masks = _head_masks()
    n_pairs = MOBA_PAIRS
    heads = range(2 * n_pairs)

    @pl.when(qi == 0)
    def _prepare():
        lane = lax.broadcasted_iota(jnp.int32, (blk, LANES), 1)
        ones_row = (lax.broadcasted_iota(jnp.int32, (VT_ROWS - HEAD_DIM, LANES), 0) == 0).astype(BF16)

        def key_block(j, carry):
            r0 = pl.multiple_of(j * blk, blk)
            onehot = (lane == j).astype(BF16)
            for pr in range(n_pairs):
                kj = k_ref[0, pl.ds(r0, blk), pr * PAIR:(pr + 1) * PAIR]
                vj = v_ref[0, pl.ds(r0, blk), pr * PAIR:(pr + 1) * PAIR].astype(F32)
                kaug_ref[pr, pl.ds(r0, blk), 0:LANES] = kj
                kaug_ref[pr, pl.ds(r0, blk), LANES:2 * LANES] = onehot
                for t in range(blk // LANES):
                    c0 = pl.multiple_of(r0 + t * LANES, LANES)
                    vt = vj[t * LANES:(t + 1) * LANES, :].T.astype(BF16)
                    for h in range(2):
                        vt_ref[2 * pr + h, 0:HEAD_DIM, pl.ds(c0, LANES)] = vt[h * HEAD_DIM:(h + 1) * HEAD_DIM, :]
                        vt_ref[2 * pr + h, HEAD_DIM:VT_ROWS, pl.ds(c0, LANES)] = ones_row
                kmean_ref[pr, pl.ds(j, 1), :] = jnp.sum(kj.astype(F32), axis=0, keepdims=True) * (1.0 / blk)
            return carry
        kmean_ref[...] = jnp.zeros((n_pairs, NB_PAD, PAIR), F32)
        lax.fori_loop(0, nb, key_block, 0)

        kstacks = []
        for pr in range(n_pairs):
            pieces = []
            for h in range(2):
                rest = kmean_ref[pr] * masks[h]
                for _ in range(3):
                    part = rest.astype(BF16)
                    pieces.append(part)
                    rest = rest - part.astype(F32)
            kstacks.append(jnp.concatenate(pieces, axis=0))
        blk_id = lax.broadcasted_iota(jnp.int32, (NB_PAD, qw), 0)
        blk_f = blk_id.astype(F32)
        half = lax.broadcasted_iota(jnp.int32, (NB_PAD, qw), 1) >> (blk.bit_length() - 1)
        head_rows = lax.broadcasted_iota(jnp.int32, (PAIR, qw), 0) >> 6
        pad_rows = jnp.zeros((PAIR - NB_PAD, qw), BF16)

        def query_blocks(m, carry):
            c0 = pl.multiple_of(m * qw, qw)
            own_blk = 2 * m + half
            valid = blk_id < own_blk
            q_ts, gs, opens = [], [], []
            for pr in range(n_pairs):
                q = q_ref[0, pl.ds(c0, qw), pr * PAIR:(pr + 1) * PAIR]
                qf = q.astype(F32) * (QK_SCALE * LOG2E)
                q_ts.append(jnp.concatenate(
                    [qf[t * LANES:(t + 1) * LANES, :].T for t in range(qw // LANES)], axis=1))
                gate_parts = lax.dot_general(kstacks[pr], q, _NT, preferred_element_type=F32)
                for h in range(2):
                    g3 = [gate_parts[(3 * h + i) * NB_PAD:(3 * h + i + 1) * NB_PAD] for i in range(3)]
                    gs.append(jnp.where(valid, g3[0] + g3[1] + g3[2], -jnp.inf))
                    opens.append((blk_id == own_blk).astype(F32))
            for _ in range(MOBA_TOPK):
                for hh in heads:
                    top = jnp.max(gs[hh], axis=0, keepdims=True)
                    first = jnp.min(jnp.where(gs[hh] == top, blk_f, float(NB_PAD)), axis=0, keepdims=True)
                    pick = (blk_f == first) & (top > -jnp.inf)
                    opens[hh] = jnp.where(pick, 1.0, opens[hh])
                    gs[hh] = jnp.where(pick, -jnp.inf, gs[hh])
            for hh in heads:
                bias = jnp.where(opens[hh] > 0.5, 0.0, MASK_VALUE).astype(BF16)
                qaug_ref[hh, 0:PAIR, pl.ds(c0, qw)] = jnp.where(
                    head_rows == hh % 2, q_ts[hh // 2], 0.0).astype(BF16)
                qaug_ref[hh, PAIR:PAIR + NB_PAD, pl.ds(c0, qw)] = bias
                qaug_ref[hh, PAIR + NB_PAD:2 * PAIR, pl.ds(c0, qw)] = pad_rows
            return carry
        lax.fori_loop(0, nb // 2, query_blocks, 0)

    qs = [qaug_ref[hh, :, pl.ds(pl.multiple_of(qi * qw, qw), qw)] for hh in heads]

    key_pos = lax.broadcasted_iota(jnp.int32, (blk, blk), 0)
    qry_pos = lax.broadcasted_iota(jnp.int32, (blk, blk), 1)
    always = jnp.int32(1 << 20)

    def scores(t, s_ref):
        r0 = pl.multiple_of(t * (2 * blk), 2 * blk)
        k2 = [kaug_ref[pr, pl.ds(r0, 2 * blk), :] for pr in range(n_pairs)]
        sts = [jnp.dot(k2[hh // 2], qs[hh], preferred_element_type=F32) for hh in heads]
        causal = key_pos <= qry_pos + jnp.where(t == qi, 0, always)
        maxima = []
        for hh, st in enumerate(sts):
            st = jnp.concatenate(
                [jnp.concatenate([jnp.where(causal, st[:blk, :blk], MASK_VALUE), st[:blk, blk:]], axis=1),
                 jnp.concatenate([st[blk:, :blk], jnp.where(causal, st[blk:, blk:], MASK_VALUE)], axis=1)],
                axis=0)
            s_ref[hh] = st
            maxima.append(jnp.max(st, axis=0, keepdims=True))
        return maxima

    def update(t, state, maxima, s_ref):
        r0 = pl.multiple_of(t * (2 * blk), 2 * blk)
        m_new = [jnp.maximum(state[2 * hh], maxima[hh]) for hh in heads]
        ps = [jnp.exp2(s_ref[hh] - m_new[hh]).astype(BF16) for hh in heads]
        pvs = [jnp.dot(vt_ref[hh, :, pl.ds(r0, 2 * blk)], ps[hh], preferred_element_type=F32)
               for hh in heads]
        out = []
        for hh in heads:
            alpha = jnp.exp2(state[2 * hh] - m_new[hh])
            out += [m_new[hh], state[2 * hh + 1] * alpha + pvs[hh]]
        return out

    init = []
    for hh in heads:
        init += [jnp.full((1, qw), MASK_VALUE, F32), jnp.zeros((VT_ROWS, qw), F32)]
    n_state = len(init)

    def two_pairs(u, carry):
        t0 = 2 * u
        max_b = scores(t0 + 1, sb_ref)
        state = update(t0, carry[:n_state], carry[n_state:], sa_ref)
        max_a = scores(t0 + 2, sa_ref)
        state = update(t0 + 1, state, max_b, sb_ref)
        return tuple(state + max_a)

    carry = lax.fori_loop(0, qi // 2, two_pairs, tuple(init + scores(0, sa_ref)))

    def odd_tail(carry):
        max_b = scores(qi, sb_ref)
        state = update(qi - 1, carry[:n_state], carry[n_state:], sa_ref)
        return tuple(update(qi, state, max_b, sb_ref))

    def even_tail(carry):
        return tuple(update(qi, carry[:n_state], carry[n_state:], sa_ref))

    fin = lax.cond((qi & 1) == 1, odd_tail, even_tail, carry)

    for pr in range(n_pairs):
        out_t = jnp.concatenate(
            [fin[2 * hh + 1][:HEAD_DIM] / fin[2 * hh + 1][HEAD_DIM:HEAD_DIM + 1]
             for hh in (2 * pr, 2 * pr + 1)], axis=0)
        o_ref[0, :, pr * PAIR:(pr + 1) * PAIR] = out_t.T.astype(o_ref.dtype)


def _moba(q, k, v):
    b, s, _ = q.shape
    nb = s // MOBA_BLOCK
    width = MOBA_PAIRS * PAIR
    n_heads = 2 * MOBA_PAIRS
    seq_spec = pl.BlockSpec((1, s, width), lambda i, p, j: (i, 0, p))
    return pl.pallas_call(
        _moba_kernel,
        grid=(b, N_PAIRS // MOBA_PAIRS, nb // 2),
        in_specs=[seq_spec, seq_spec, seq_spec],
        out_specs=pl.BlockSpec((1, 2 * MOBA_BLOCK, width), lambda i, p, j: (i, j, p)),
        out_shape=jax.ShapeDtypeStruct((b, s, MIX_WIDTH), BF16),
        scratch_shapes=[pltpu.VMEM((MOBA_PAIRS, s, 2 * LANES), BF16),
                        pltpu.VMEM((n_heads, VT_ROWS, s), BF16),
                        pltpu.VMEM((MOBA_PAIRS, NB_PAD, PAIR), F32),
                        pltpu.VMEM((n_heads, 2 * LANES, s), BF16),
                        pltpu.VMEM((n_heads, 2 * MOBA_BLOCK, 2 * MOBA_BLOCK), F32),
                        pltpu.VMEM((n_heads, 2 * MOBA_BLOCK, 2 * MOBA_BLOCK), F32)],
        compiler_params=pltpu.CompilerParams(
            dimension_semantics=("arbitrary", "arbitrary", "arbitrary"),
            vmem_limit_bytes=VMEM_LIMIT),
        name="moba",
    )(q, k, v)


def _outproj_kernel(mix_ref, z_ref, mq_ref, mkv_ref, h_ref, wmix_ref, wmem_ref, hg_ref, fg_ref,
                    o_ref, *, head_norm, final_norm):
    mq = mq_ref[0].astype(F32)
    mk = mkv_ref[0, :, :MEM_WIDTH]
    mv = mkv_ref[0, :, MEM_WIDTH:]
    row = lax.broadcasted_iota(jnp.int32, (MEM_WIDTH, MEM_WIDTH), 0)
    col = lax.broadcasted_iota(jnp.int32, (MEM_WIDTH, MEM_WIDTH), 1)
    lane = lax.broadcasted_iota(jnp.int32, (1, MEM_WIDTH), 1)
    mhs = [((lane >> 6) == h).astype(F32) for h in range(MEM_HEADS)]
    logits = [_dot_nt(mq * (mh * QK_SCALE), mk) for mh in mhs]
    es = [jnp.exp(lg - jnp.max(lg, axis=-1, keepdims=True)) for lg in logits]
    inv = [1.0 / jnp.sum(e, axis=-1, keepdims=True) for e in es]
    pvs = [jnp.dot(e.astype(BF16), mv, preferred_element_type=F32) for e in es]
    memo = pvs[0] * (mhs[0] * inv[0])
    for h in range(1, MEM_HEADS):
        memo = memo + pvs[h] * (mhs[h] * inv[h])
    z = z_ref[0].astype(F32)
    gate = z * _sigmoid(z)
    mix = mix_ref[0].astype(F32)
    if head_norm:
        same_head = ((row >> 6) == (col >> 6)).astype(BF16)
        slabs = []
        for sl in range(MIX_WIDTH // MEM_WIDTH):
            ms = mix[:, sl * MEM_WIDTH:(sl + 1) * MEM_WIDTH]
            ss = _dot(ms * ms, same_head) * (1.0 / HEAD_DIM)
            slabs.append(ms * lax.rsqrt(ss + NORM_EPS))
        mix = jnp.concatenate(slabs, axis=1) * hg_ref[...]
    y_mix = mix * gate[:, :MIX_WIDTH]
    y_mem = memo * gate[:, MIX_WIDTH:]
    out = h_ref[0] + _dot(y_mix, wmix_ref[...]) + _dot(y_mem, wmem_ref[...])
    if final_norm:
        out = out * _rms_scale(out) * fg_ref[...]
    o_ref[0] = out


def _outproj(mix, z, mq, mkv, h, wmix, wmem, hg, fg, tm, head_norm, final_norm):
    b, s, _ = h.shape
    tile = lambda n: pl.BlockSpec((1, tm, n), lambda i, j: (i, j, 0))
    return pl.pallas_call(
        functools.partial(_outproj_kernel, head_norm=head_norm, final_norm=final_norm),
        grid=(b, s // tm),
        in_specs=[tile(MIX_WIDTH), tile(GATE_WIDTH), tile(MEM_WIDTH),
                  pl.BlockSpec((1, N_MEM, 2 * MEM_WIDTH), lambda i, j: (i, 0, 0)),
                  tile(D_MODEL),
                  pl.BlockSpec((MIX_WIDTH, D_MODEL), lambda i, j: (0, 0)),
                  pl.BlockSpec((MEM_WIDTH, D_MODEL), lambda i, j: (0, 0)),
                  pl.BlockSpec((1, MIX_WIDTH), lambda i, j: (0, 0)),
                  pl.BlockSpec((1, D_MODEL), lambda i, j: (0, 0))],
        out_specs=tile(D_MODEL),
        out_shape=jax.ShapeDtypeStruct((b, s, D_MODEL), F32),
        compiler_params=pltpu.CompilerParams(vmem_limit_bytes=VMEM_LIMIT),
        name="outproj_final" if final_norm else "outproj",
    )(mix, z, mq, mkv, h, wmix, wmem, hg, fg)


def _row(v):
    return v.reshape(1, -1).astype(F32)


def kernel(x, mem, positions, norm_0, w_in_0, conv_w_0, a_log_0, dt_bias_0, o_norm_0,
           mem_norm_0, w_mem_kv_0, w_out_0, norm_1, w_in_1, mem_norm_1, w_mem_kv_1,
           w_out_1, final_norm):
    b, s, _ = x.shape
    assert s % (2 * MOBA_BLOCK) == 0 and MOBA_TOPK <= s // MOBA_BLOCK <= NB_PAD
    assert s % (DELTA_GROUP * DELTA_CHUNK) == 0
    tm = 512

    mkv0, mkv1 = _memkv(mem, _row(mem_norm_0), w_mem_kv_0.astype(BF16),
                        _row(mem_norm_1), w_mem_kv_1.astype(BF16))

    i1 = 3 * MIX_WIDTH
    i2 = i1 + GATE_WIDTH
    i3 = i2 + MEM_WIDTH
    w0 = w_in_0.astype(BF16)
    wba = jnp.pad(w0[:, i3:], ((0, 0), (0, LANES - 2 * MIX_HEADS)))
    qkv, z0, mq0, ba = _inproj0(x, _row(norm_0), w0[:, :i1], w0[:, i1:i2], w0[:, i2:i3], wba, tm)
    ba_cols = lambda v: _row(jnp.pad(v, (MIX_HEADS, LANES - 2 * MIX_HEADS)))
    o0 = _delta_mixer(qkv, ba, conv_w_0.astype(F32), ba_cols(a_log_0), ba_cols(dt_bias_0))
    wo0 = w_out_0.astype(BF16)
    h1 = _outproj(o0, z0, mq0, mkv0, x, wo0[:MIX_WIDTH], wo0[MIX_WIDTH:],
                  _row(jnp.tile(o_norm_0, MIX_HEADS)), _row(final_norm),
                  tm, head_norm=True, final_norm=False)

    w1 = w_in_1.astype(BF16)
    half = HEAD_DIM // 2
    inv_freq = ROPE_THETA ** (-jnp.arange(half, dtype=F32) * (2.0 / HEAD_DIM))
    invf = _row(jnp.tile(inv_freq, LANES // half))
    q1, k1, v1, z1, mq1 = _inproj1(
        h1, _row(norm_1), positions.reshape(b, s, 1), invf,
        w1[:, :MIX_WIDTH], w1[:, MIX_WIDTH:2 * MIX_WIDTH], w1[:, 2 * MIX_WIDTH:i1],
        w1[:, i1:i2], w1[:, i2:], tm)
    o1 = _moba(q1, k1, v1)
    wo1 = w_out_1.astype(BF16)
    return _outproj(o1, z1, mq1, mkv1, h1, wo1[:MIX_WIDTH], wo1[MIX_WIDTH:],
                    jnp.ones((1, MIX_WIDTH), F32), _row(final_norm),
                    tm, head_norm=False, final_norm=True)
```

```python
import functools

import jax
import jax.numpy as jnp
from jax import lax
from jax.experimental import pallas as pl
from jax.experimental.pallas import tpu as pltpu

D_MODEL = 1024
HEAD_DIM = 64
MIX_HEADS = 12
MEM_HEADS = 4
MIX_WIDTH = MIX_HEADS * HEAD_DIM
MEM_WIDTH = MEM_HEADS * HEAD_DIM
GATE_WIDTH = MIX_WIDTH + MEM_WIDTH
N_MEM = 256
CONV_WIDTH = 4
MOBA_BLOCK = 256
MOBA_TOPK = 3
ROPE_THETA = 10000.0
NORM_EPS = 1e-6
MASK_VALUE = -1e30

LANES = 128
SUBLANES = 8
PAIR = 2 * HEAD_DIM
N_PAIRS = MIX_HEADS // 2
DELTA_CHUNK = 128
DELTA_GROUP = 8
INV_BASE = 16
QK_SCALE = HEAD_DIM ** -0.5
LOG2E = 1.4426950408889634
MOBA_PAIRS = 2
NB_PAD = 16
VT_ROWS = HEAD_DIM + 16
VMEM_LIMIT = 56 * 1024 * 1024

BF16 = jnp.bfloat16
F32 = jnp.float32

_NT = (((1,), (1,)), ((), ()))


def _dot(a, b):
    return jnp.dot(a.astype(BF16), b.astype(BF16), preferred_element_type=F32)


def _dot_nt(a, b):
    return lax.dot_general(a.astype(BF16), b.astype(BF16), _NT, preferred_element_type=F32)


def _dot_hi(a, b):
    return jnp.dot(a, b, precision=lax.Precision.HIGHEST, preferred_element_type=F32)


def _dot_nt_hi(a, b):
    return lax.dot_general(a, b, _NT, precision=lax.Precision.HIGHEST,
                           preferred_element_type=F32)


def _rms_scale(x):
    return lax.rsqrt(jnp.mean(x * x, axis=-1, keepdims=True) + NORM_EPS)


def _sigmoid(x):
    return 1.0 / (1.0 + jnp.exp(-x))


def _softplus(x):
    return jnp.maximum(x, 0.0) + jnp.log(1.0 + jnp.exp(-jnp.abs(x)))


def _head_masks():
    lane = lax.broadcasted_iota(jnp.int32, (1, PAIR), 1)
    return [((lane >> 6) == h).astype(F32) for h in range(2)]


def _memkv_kernel(mem_ref, g0_ref, w0_ref, g1_ref, w1_ref, o0_ref, o1_ref):
    m = mem_ref[0]
    mn = m * _rms_scale(m)
    o0_ref[0] = _dot(mn * g0_ref[...], w0_ref[...]).astype(o0_ref.dtype)
    o1_ref[0] = _dot(mn * g1_ref[...], w1_ref[...]).astype(o1_ref.dtype)


def _memkv(mem, g0, w0, g1, w1):
    b = mem.shape[0]
    full = lambda shp: pl.BlockSpec(shp, lambda i: (0,) * len(shp))
    return pl.pallas_call(
        _memkv_kernel,
        grid=(b,),
        in_specs=[pl.BlockSpec((1, N_MEM, D_MODEL), lambda i: (i, 0, 0)),
                  full((1, D_MODEL)), full((D_MODEL, 2 * MEM_WIDTH)),
                  full((1, D_MODEL)), full((D_MODEL, 2 * MEM_WIDTH))],
        out_specs=[pl.BlockSpec((1, N_MEM, 2 * MEM_WIDTH), lambda i: (i, 0, 0))] * 2,
        out_shape=[jax.ShapeDtypeStruct((b, N_MEM, 2 * MEM_WIDTH), BF16)] * 2,
        compiler_params=pltpu.CompilerParams(vmem_limit_bytes=VMEM_LIMIT),
        name="memkv",
    )(mem, g0, w0, g1, w1)


def _inproj0_kernel(x_ref, g_ref, wqkv_ref, wz_ref, wmq_ref, wba_ref,
                    qkv_ref, z_ref, mq_ref, ba_ref):
    x = x_ref[0]
    xn = (x * _rms_scale(x) * g_ref[...]).astype(BF16)
    for w_ref, o_ref in ((wqkv_ref, qkv_ref), (wz_ref, z_ref),
                         (wmq_ref, mq_ref), (wba_ref, ba_ref)):
        o_ref[0] = jnp.dot(xn, w_ref[...], preferred_element_type=F32).astype(o_ref.dtype)


def _inproj0(x, g, wqkv, wz, wmq, wba, tm):
    b, s, _ = x.shape
    widths = (3 * MIX_WIDTH, GATE_WIDTH, MEM_WIDTH, LANES)
    wspec = lambda n: pl.BlockSpec((D_MODEL, n), lambda i, j: (0, 0))
    return pl.pallas_call(
        _inproj0_kernel,
        grid=(b, s // tm),
        in_specs=[pl.BlockSpec((1, tm, D_MODEL), lambda i, j: (i, j, 0)),
                  pl.BlockSpec((1, D_MODEL), lambda i, j: (0, 0))]
                 + [wspec(n) for n in widths],
        out_specs=[pl.BlockSpec((1, tm, n), lambda i, j: (i, j, 0)) for n in widths],
        out_shape=[jax.ShapeDtypeStruct((b, s, n), dt)
                   for n, dt in zip(widths, (BF16, BF16, BF16, F32))],
        compiler_params=pltpu.CompilerParams(vmem_limit_bytes=VMEM_LIMIT),
        name="inproj0",
    )(x, g, wqkv, wz, wmq, wba)


def _inproj1_kernel(x_ref, g_ref, pos_ref, invf_ref, wq_ref, wk_ref, wv_ref, wz_ref, wmq_ref,
                    q_ref, k_ref, v_ref, z_ref, mq_ref):
    x = x_ref[0]
    xn = (x * _rms_scale(x) * g_ref[...]).astype(BF16)
    ang = pos_ref[0].astype(F32) * invf_ref[...]
    cos = jnp.cos(ang)
    sin = jnp.sin(ang)
    lane = lax.broadcasted_iota(jnp.int32, ang.shape, 1)
    first_half = (lane & (HEAD_DIM - 1)) < (HEAD_DIM // 2)
    sin_signed = jnp.where(first_half, -sin, sin)
    for w_ref, o_ref in ((wq_ref, q_ref), (wk_ref, k_ref)):
        y = jnp.dot(xn, w_ref[...], preferred_element_type=F32)
        for sl in range(MIX_WIDTH // LANES):
            ys = y[:, sl * LANES:(sl + 1) * LANES]
            rot = jnp.where(first_half,
                            pltpu.roll(ys, LANES - HEAD_DIM // 2, 1),
                            pltpu.roll(ys, HEAD_DIM // 2, 1))
            o_ref[0, :, sl * LANES:(sl + 1) * LANES] = (ys * cos + rot * sin_signed).astype(o_ref.dtype)
    for w_ref, o_ref in ((wv_ref, v_ref), (wz_ref, z_ref), (wmq_ref, mq_ref)):
        o_ref[0] = jnp.dot(xn, w_ref[...], preferred_element_type=F32).astype(o_ref.dtype)


def _inproj1(x, g, pos3, invf, wq, wk, wv, wz, wmq, tm):
    b, s, _ = x.shape
    widths = (MIX_WIDTH, MIX_WIDTH, MIX_WIDTH, GATE_WIDTH, MEM_WIDTH)
    wspec = lambda n: pl.BlockSpec((D_MODEL, n), lambda i, j: (0, 0))
    return pl.pallas_call(
        _inproj1_kernel,
        grid=(b, s // tm),
        in_specs=[pl.BlockSpec((1, tm, D_MODEL), lambda i, j: (i, j, 0)),
                  pl.BlockSpec((1, D_MODEL), lambda i, j: (0, 0)),
                  pl.BlockSpec((1, tm, 1), lambda i, j: (i, j, 0)),
                  pl.BlockSpec((1, LANES), lambda i, j: (0, 0))]
                 + [wspec(n) for n in widths],
        out_specs=[pl.BlockSpec((1, tm, n), lambda i, j: (i, j, 0)) for n in widths],
        out_shape=[jax.ShapeDtypeStruct((b, s, n), BF16) for n in widths],
        compiler_params=pltpu.CompilerParams(vmem_limit_bytes=VMEM_LIMIT),
        name="inproj1",
    )(x, g, pos3, invf, wq, wk, wv, wz, wmq)


def _unit_lower_inverses(lows, row, col, eye):
    c = lows[0].shape[0]
    shift = INV_BASE.bit_length() - 1
    base = (row >> shift) == (col >> shift)
    ns = [jnp.where(base, -low, 0.0) for low in lows]
    xs = [eye + n for n in ns]
    ns = [_dot(n, n) for n in ns]
    power = 2
    while 2 * power < INV_BASE:
        prods = [_dot(jnp.concatenate([n, x], axis=0), n) for x, n in zip(xs, ns)]
        xs = [x + p[c:] for x, p in zip(xs, prods)]
        ns = [p[:c] for p in prods]
        power *= 2
    xs = [x + _dot(x, n) for x, n in zip(xs, ns)]
    size = INV_BASE
    while size < c:
        sh = size.bit_length() - 1
        sel = (((row >> (sh + 1)) == (col >> (sh + 1)))
               & (((row >> sh) & 1) == 1) & (((col >> sh) & 1) == 0))
        ys = [_dot(x, jnp.where(sel, low, 0.0)) for x, low in zip(xs, lows)]
        xs = [x - _dot(y, x) for x, y in zip(xs, ys)]
        size *= 2
    return xs


def _delta_kernel(q_ref, k_ref, v_ref, ba_ref, cwq_ref, cwk_ref, cwv_ref,
                  alog_ref, dt_ref, o_ref,
                  ext_ref, p_ref, qp_ref, n_ref, op_ref, egl_ref):
    c = DELTA_CHUNK
    grp = DELTA_GROUP
    rows = grp * c
    seq = q_ref.shape[1]
    pair = pl.program_id(1)
    masks = _head_masks()
    row = lax.broadcasted_iota(jnp.int32, (c, c), 0)
    col = lax.broadcasted_iota(jnp.int32, (c, c), 1)
    tri_incl = row >= col
    tri_strict = row > col
    eye = (row == col).astype(F32)
    tril_b = tri_incl.astype(BF16)
    head_block = ((row >> 6) == (col >> 6)).astype(F32)
    head_block_b = head_block.astype(BF16)
    row2 = lax.broadcasted_iota(jnp.int32, (LANES, 2 * PAIR), 0)
    col2 = lax.broadcasted_iota(jnp.int32, (LANES, 2 * PAIR), 1)
    expand = (row2 == jnp.where(col2 < PAIR, 0, MIX_HEADS) + 2 * pair + ((col2 >> 6) & 1)).astype(BF16)
    lane_n = lax.broadcasted_iota(jnp.int32, (1, LANES), 1)
    is_beta = lane_n < MIX_HEADS
    neg_a = -jnp.exp(alog_ref[...])
    dt_row = dt_ref[...]

    ext_ref[:, 0:SUBLANES, :] = jnp.zeros((3, SUBLANES, PAIR), F32)

    def conv_silu(idx, x_ref, cw_ref, t0):
        raw = x_ref[0, pl.ds(t0, rows), :].astype(F32)
        ext_ref[idx, SUBLANES:SUBLANES + rows, :] = raw
        acc = raw * cw_ref[CONV_WIDTH - 1:CONV_WIDTH, :]
        for tap in range(CONV_WIDTH - 1):
            back = CONV_WIDTH - 1 - tap
            acc = acc + ext_ref[idx, SUBLANES - back:SUBLANES - back + rows, :] * cw_ref[tap:tap + 1, :]
        ext_ref[idx, 0:SUBLANES, :] = raw[rows - SUBLANES:rows, :]
        return acc * _sigmoid(acc)

    def l2norm(x):
        return x * lax.rsqrt(_dot(x * x, head_block_b) + NORM_EPS)

    def build_group(gi, carry):
        t0 = pl.multiple_of(gi * rows, rows)
        qn = l2norm(conv_silu(0, q_ref, cwq_ref, t0))
        kn = l2norm(conv_silu(1, k_ref, cwk_ref, t0))
        vc = conv_silu(2, v_ref, cwv_ref, t0)
        ba = ba_ref[0, pl.ds(t0, rows), :]
        act = jnp.where(is_beta, _sigmoid(ba), neg_a * _softplus(ba + dt_row))
        bg = jnp.dot(act.astype(BF16), expand, preferred_element_type=F32)
        beta = bg[:, :PAIR]
        g = bg[:, PAIR:]
        kb = kn * beta
        vb = vc * beta
        sls = [slice(ci * c, (ci + 1) * c) for ci in range(grp)]
        gcs = [jnp.dot(tril_b, g[sl].astype(BF16), preferred_element_type=F32) for sl in sls]
        gcts = [gc.T for gc in gcs]
        egs = [jnp.exp(gc) for gc in gcs]
        probs = [(ci, h) for ci in range(grp) for h in range(2)]
        decays = []
        for ci, h in probs:
            gcol = jnp.broadcast_to(gcs[ci][:, h * HEAD_DIM:h * HEAD_DIM + 1], (c, c))
            grow = gcts[ci][h * HEAD_DIM:h * HEAD_DIM + 1, :]
            decays.append(jnp.exp(jnp.where(tri_incl, gcol - grow, -jnp.inf)))
        kbhs = [kb[sls[ci]] * masks[h] for ci, h in probs]
        kq = [_dot_nt(jnp.concatenate(
            [kbhs[2 * ci], kbhs[2 * ci + 1]]
            + [qn[sls[ci]] * (masks[h] * QK_SCALE) for h in range(2)], axis=0), kn[sls[ci]])
            for ci in range(grp)]
        lows = [jnp.where(tri_strict, kq[ci][h * c:(h + 1) * c] * d, 0.0)
                for (ci, h), d in zip(probs, decays)]
        aqks = [jnp.where(tri_incl, kq[ci][(2 + h) * c:(3 + h) * c] * d, 0.0)
                for (ci, h), d in zip(probs, decays)]
        tinvs = _unit_lower_inverses(lows, row, col, eye)
        wus = [_dot(t, jnp.concatenate([kbh * egs[ci], vb[sls[ci]] * masks[h]], axis=1))
               for t, kbh, (ci, h) in zip(tinvs, kbhs, probs)]
        aus = [_dot(a, wu) for a, wu in zip(aqks, wus)]
        kds = [kn[sls[ci]] * jnp.exp(gcs[ci][c - 1:c, :] - gcs[ci]) for ci in range(grp)]
        pns = [_dot(kds[ci].T, wus[2 * ci] + wus[2 * ci + 1]) for ci in range(grp)]
        for ci in range(grp):
            cidx = gi * grp + ci
            au = aus[2 * ci] + aus[2 * ci + 1]
            p_ref[cidx] = (head_block * pns[ci][:, :PAIR]).astype(BF16)
            n_ref[cidx] = head_block * pns[ci][:, PAIR:]
            qp_ref[cidx] = (qn[sls[ci]] * (QK_SCALE * egs[ci]) - au[:, :PAIR]).astype(BF16)
            op_ref[cidx] = au[:, PAIR:]
            egl_ref[pl.ds(cidx, 1), :] = jnp.exp(gcs[ci][c - 1:c, :])
        return carry

    lax.fori_loop(0, seq // rows, build_group, 0)

    def scan_step(ci, state):
        sb = state.astype(BF16)
        o = jnp.dot(qp_ref[ci], sb, preferred_element_type=F32) + op_ref[ci]
        new_state = (state * egl_ref[pl.ds(ci, 1), :]
                     - jnp.dot(p_ref[ci], sb, preferred_element_type=F32) + n_ref[ci])
        o_ref[0, pl.ds(pl.multiple_of(ci * c, c), c), :] = o.astype(o_ref.dtype)
        return new_state

    lax.fori_loop(0, seq // c, scan_step, jnp.zeros((PAIR, PAIR), F32))


def _delta_mixer(qkv, ba, conv_w, alog_row, dt_row):
    b, s, _ = qkv.shape
    n_chunks = s // DELTA_CHUNK
    seq_spec = lambda off: pl.BlockSpec((1, s, PAIR), lambda i, p, off=off: (i, 0, off + p))
    cw_spec = lambda off: pl.BlockSpec((CONV_WIDTH, PAIR), lambda i, p, off=off: (0, off + p))
    row_spec = pl.BlockSpec((1, LANES), lambda i, p: (0, 0))
    return pl.pallas_call(
        _delta_kernel,
        grid=(b, N_PAIRS),
        in_specs=[seq_spec(0), seq_spec(N_PAIRS), seq_spec(2 * N_PAIRS),
                  pl.BlockSpec((1, s, LANES), lambda i, p: (i, 0, 0)),
                  cw_spec(0), cw_spec(N_PAIRS), cw_spec(2 * N_PAIRS),
                  row_spec, row_spec],
        out_specs=pl.BlockSpec((1, s, PAIR), lambda i, p: (i, 0, p)),
        out_shape=jax.ShapeDtypeStruct((b, s, MIX_WIDTH), BF16),
        scratch_shapes=[pltpu.VMEM((3, SUBLANES + DELTA_GROUP * DELTA_CHUNK, PAIR), F32),
                        pltpu.VMEM((n_chunks, PAIR, PAIR), BF16),
                        pltpu.VMEM((n_chunks, DELTA_CHUNK, PAIR), BF16),
                        pltpu.VMEM((n_chunks, PAIR, PAIR), F32),
                        pltpu.VMEM((n_chunks, DELTA_CHUNK, PAIR), F32),
                        pltpu.VMEM((n_chunks, PAIR), F32)],
        compiler_params=pltpu.CompilerParams(vmem_limit_bytes=VMEM_LIMIT),
        name="delta_mixer",
    )(qkv, qkv, qkv, ba, conv_w, conv_w, conv_w, alog_row, dt_row)


def _moba_kernel(q_ref, k_ref, v_ref, o_ref, kaug_ref, vt_ref, kmean_ref, qaug_ref, sa_ref, sb_ref):
    blk = MOBA_BLOCK
    qw = 2 * blk
    seq = k_ref.shape[1]
    nb = seq // blk
    qi = pl.program_id(2)
    masks = _head_masks()
    n_pairs = MOBA_PAIRS
    heads = range(2 * n_pairs)

    @pl.when(qi == 0)
    def _prepare():
        lane = lax.broadcasted_iota(jnp.int32, (blk, LANES), 1)
        ones_row = (lax.broadcasted_iota(jnp.int32, (VT_ROWS - HEAD_DIM, LANES), 0) == 0).astype(BF16)

        def key_block(j, carry):
            r0 = pl.multiple_of(j * blk, blk)
            onehot = (lane == j).astype(BF16)
            for pr in range(n_pairs):
                kj = k_ref[0, pl.ds(r0, blk), pr * PAIR:(pr + 1) * PAIR]
                vj = v_ref[0, pl.ds(r0, blk), pr * PAIR:(pr + 1) * PAIR].astype(F32)
                kaug_ref[pr, pl.ds(r0, blk), 0:LANES] = kj
                kaug_ref[pr, pl.ds(r0, blk), LANES:2 * LANES] = onehot
                for t in range(blk // LANES):
                    c0 = pl.multiple_of(r0 + t * LANES, LANES)
                    vt = vj[t * LANES:(t + 1) * LANES, :].T.astype(BF16)
                    for h in range(2):
                        vt_ref[2 * pr + h, 0:HEAD_DIM, pl.ds(c0, LANES)] = vt[h * HEAD_DIM:(h + 1) * HEAD_DIM, :]
                        vt_ref[2 * pr + h, HEAD_DIM:VT_ROWS, pl.ds(c0, LANES)] = ones_row
                kmean_ref[pr, pl.ds(j, 1), :] = jnp.sum(kj.astype(F32), axis=0, keepdims=True) * (1.0 / blk)
            return carry
        kmean_ref[...] = jnp.zeros((n_pairs, NB_PAD, PAIR), F32)
        lax.fori_loop(0, nb, key_block, 0)

        kstacks = []
        for pr in range(n_pairs):
            pieces = []
            for h in range(2):
                rest = kmean_ref[pr] * masks[h]
                for _ in range(3):
                    part = rest.astype(BF16)
                    pieces.append(part)
                    rest = rest - part.astype(F32)
            kstacks.append(jnp.concatenate(pieces, axis=0))
        blk_id = lax.broadcasted_iota(jnp.int32, (NB_PAD, qw), 0)
        blk_f = blk_id.astype(F32)
        half = lax.broadcasted_iota(jnp.int32, (NB_PAD, qw), 1) >> (blk.bit_length() - 1)
        head_rows = lax.broadcasted_iota(jnp.int32, (PAIR, qw), 0) >> 6
        pad_rows = jnp.zeros((PAIR - NB_PAD, qw), BF16)

        def query_blocks(m, carry):
            c0 = pl.multiple_of(m * qw, qw)
            own_blk = 2 * m + half
            valid = blk_id < own_blk
            q_ts, gs, opens = [], [], []
            for pr in range(n_pairs):
                q = q_ref[0, pl.ds(c0, qw), pr * PAIR:(pr + 1) * PAIR]
                qf = q.astype(F32) * (QK_SCALE * LOG2E)
                q_ts.append(jnp.concatenate(
                    [qf[t * LANES:(t + 1) * LANES, :].T for t in range(qw // LANES)], axis=1))
                gate_parts = lax.dot_general(kstacks[pr], q, _NT, preferred_element_type=F32)
                for h in range(2):
                    g3 = [gate_parts[(3 * h + i) * NB_PAD:(3 * h + i + 1) * NB_PAD] for i in range(3)]
                    gs.append(jnp.where(valid, g3[0] + g3[1] + g3[2], -jnp.inf))
                    opens.append((blk_id == own_blk).astype(F32))
            for _ in range(MOBA_TOPK):
                for hh in heads:
                    top = jnp.max(gs[hh], axis=0, keepdims=True)
                    first = jnp.min(jnp.where(gs[hh] == top, blk_f, float(NB_PAD)), axis=0, keepdims=True)
                    pick = (blk_f == first) & (top > -jnp.inf)
                    opens[hh] = jnp.where(pick, 1.0, opens[hh])
                    gs[hh] = jnp.where(pick, -jnp.inf, gs[hh])
            for hh in heads:
                bias = jnp.where(opens[hh] > 0.5, 0.0, MASK_VALUE).astype(BF16)
                qaug_ref[hh, 0:PAIR, pl.ds(c0, qw)] = jnp.where(
                    head_rows == hh % 2, q_ts[hh // 2], 0.0).astype(BF16)
                qaug_ref[hh, PAIR:PAIR + NB_PAD, pl.ds(c0, qw)] = bias
                qaug_ref[hh, PAIR + NB_PAD:2 * PAIR, pl.ds(c0, qw)] = pad_rows
            return carry
        lax.fori_loop(0, nb // 2, query_blocks, 0)

    qs = [qaug_ref[hh, :, pl.ds(pl.multiple_of(qi * qw, qw), qw)] for hh in heads]

    key_pos = lax.broadcasted_iota(jnp.int32, (blk, blk), 0)
    qry_pos = lax.broadcasted_iota(jnp.int32, (blk, blk), 1)
    always = jnp.int32(1 << 20)

    def scores(t, s_ref):
        r0 = pl.multiple_of(t * (2 * blk), 2 * blk)
        k2 = [kaug_ref[pr, pl.ds(r0, 2 * blk), :] for pr in range(n_pairs)]
        sts = [jnp.dot(k2[hh // 2], qs[hh], preferred_element_type=F32) for hh in heads]
        causal = key_pos <= qry_pos + jnp.where(t == qi, 0, always)
        maxima = []
        for hh, st in enumerate(sts):
            st = jnp.concatenate(
                [jnp.concatenate([jnp.where(causal, st[:blk, :blk], MASK_VALUE), st[:blk, blk:]], axis=1),
                 jnp.concatenate([st[blk:, :blk], jnp.where(causal, st[blk:, blk:], MASK_VALUE)], axis=1)],
                axis=0)
            s_ref[hh] = st
            maxima.append(jnp.max(st, axis=0, keepdims=True))
        return maxima

    def update(t, state, maxima, s_ref):
        r0 = pl.multiple_of(t * (2 * blk), 2 * blk)
        m_new = [jnp.maximum(state[2 * hh], maxima[hh]) for hh in heads]
        ps = [jnp.exp2(s_ref[hh] - m_new[hh]).astype(BF16) for hh in heads]
        pvs = [jnp.dot(vt_ref[hh, :, pl.ds(r0, 2 * blk)], ps[hh], preferred_element_type=F32)
               for hh in heads]
        out = []
        for hh in heads:
            alpha = jnp.exp2(state[2 * hh] - m_new[hh])
            out += [m_new[hh], state[2 * hh + 1] * alpha + pvs[hh]]
        return out

    init = []
    for hh in heads:
        init += [jnp.full((1, qw), MASK_VALUE, F32), jnp.zeros((VT_ROWS, qw), F32)]
    n_state = len(init)

    def two_pairs(u, carry):
        t0 = 2 * u
        max_b = scores(t0 + 1, sb_ref)
        state = update(t0, carry[:n_state], carry[n_state:], sa_ref)
        max_a = scores(t0 + 2, sa_ref)
        state = update(t0 + 1, state, max_b, sb_ref)
        return tuple(state + max_a)

    carry = lax.fori_loop(0, qi // 2, two_pairs, tuple(init + scores(0, sa_ref)))

    def odd_tail(carry):
        max_b = scores(qi, sb_ref)
        state = update(qi - 1, carry[:n_state], carry[n_state:], sa_ref)
        return tuple(update(qi, state, max_b, sb_ref))

    def even_tail(carry):
        return tuple(update(qi, carry[:n_state], carry[n_state:], sa_ref))

    fin = lax.cond((qi & 1) == 1, odd_tail, even_tail, carry)

    for pr in range(n_pairs):
        out_t = jnp.concatenate(
            [fin[2 * hh + 1][:HEAD_DIM] / fin[2 * hh + 1][HEAD_DIM:HEAD_DIM + 1]
             for hh in (2 * pr, 2 * pr + 1)], axis=0)
        o_ref[0, :, pr * PAIR:(pr + 1) * PAIR] = out_t.T.astype(o_ref.dtype)


def _moba(q, k, v):
    b, s, _ = q.shape
    nb = s // MOBA_BLOCK
    width = MOBA_PAIRS * PAIR
    n_heads = 2 * MOBA_PAIRS
    seq_spec = pl.BlockSpec((1, s, width), lambda i, p, j: (i, 0, p))
    return pl.pallas_call(
        _moba_kernel,
        grid=(b, N_PAIRS // MOBA_PAIRS, nb // 2),
        in_specs=[seq_spec, seq_spec, seq_spec],
        out_specs=pl.BlockSpec((1, 2 * MOBA_BLOCK, width), lambda i, p, j: (i, j, p)),
        out_shape=jax.ShapeDtypeStruct((b, s, MIX_WIDTH), BF16),
        scratch_shapes=[pltpu.VMEM((MOBA_PAIRS, s, 2 * LANES), BF16),
                        pltpu.VMEM((n_heads, VT_ROWS, s), BF16),
                        pltpu.VMEM((MOBA_PAIRS, NB_PAD, PAIR), F32),
                        pltpu.VMEM((n_heads, 2 * LANES, s), BF16),
                        pltpu.VMEM((n_heads, 2 * MOBA_BLOCK, 2 * MOBA_BLOCK), F32),
                        pltpu.VMEM((n_heads, 2 * MOBA_BLOCK, 2 * MOBA_BLOCK), F32)],
        compiler_params=pltpu.CompilerParams(
            dimension_semantics=("arbitrary", "arbitrary", "arbitrary"),
            vmem_limit_bytes=VMEM_LIMIT),
        name="moba",
    )(q, k, v)


def _outproj_kernel(mix_ref, z_ref, mq_ref, mkv_ref, h_ref, wmix_ref, wmem_ref, hg_ref, fg_ref,
                    o_ref, *, head_norm, final_norm):
    mq = mq_ref[0].astype(F32)
    mk = mkv_ref[0, :, :MEM_WIDTH]
    mv = mkv_ref[0, :, MEM_WIDTH:]
    row = lax.broadcasted_iota(jnp.int32, (MEM_WIDTH, MEM_WIDTH), 0)
    col = lax.broadcasted_iota(jnp.int32, (MEM_WIDTH, MEM_WIDTH), 1)
    lane = lax.broadcasted_iota(jnp.int32, (1, MEM_WIDTH), 1)
    mhs = [((lane >> 6) == h).astype(F32) for h in range(MEM_HEADS)]
    logits = [_dot_nt(mq * (mh * QK_SCALE), mk) for mh in mhs]
    es = [jnp.exp(lg - jnp.max(lg, axis=-1, keepdims=True)) for lg in logits]
    inv = [1.0 / jnp.sum(e, axis=-1, keepdims=True) for e in es]
    pvs = [jnp.dot(e.astype(BF16), mv, preferred_element_type=F32) for e in es]
    memo = pvs[0] * (mhs[0] * inv[0])
    for h in range(1, MEM_HEADS):
        memo = memo + pvs[h] * (mhs[h] * inv[h])
    z = z_ref[0].astype(F32)
    gate = z * _sigmoid(z)
    mix = mix_ref[0].astype(F32)
    if head_norm:
        same_head = ((row >> 6) == (col >> 6)).astype(BF16)
        slabs = []
        for sl in range(MIX_WIDTH // MEM_WIDTH):
            ms = mix[:, sl * MEM_WIDTH:(sl + 1) * MEM_WIDTH]
            ss = _dot(ms * ms, same_head) * (1.0 / HEAD_DIM)
            slabs.append(ms * lax.rsqrt(ss + NORM_EPS))
        mix = jnp.concatenate(slabs, axis=1) * hg_ref[...]
    y_mix = mix * gate[:, :MIX_WIDTH]
    y_mem = memo * gate[:, MIX_WIDTH:]
    out = h_ref[0] + _dot(y_mix, wmix_ref[...]) + _dot(y_mem, wmem_ref[...])
    if final_norm:
        out = out * _rms_scale(out) * fg_ref[...]
    o_ref[0] = out


def _outproj(mix, z, mq, mkv, h, wmix, wmem, hg, fg, tm, head_norm, final_norm):
    b, s, _ = h.shape
    tile = lambda n: pl.BlockSpec((1, tm, n), lambda i, j: (i, j, 0))
    return pl.pallas_call(
        functools.partial(_outproj_kernel, head_norm=head_norm, final_norm=final_norm),
        grid=(b, s // tm),
        in_specs=[tile(MIX_WIDTH), tile(GATE_WIDTH), tile(MEM_WIDTH),
                  pl.BlockSpec((1, N_MEM, 2 * MEM_WIDTH), lambda i, j: (i, 0, 0)),
                  tile(D_MODEL),
                  pl.BlockSpec((MIX_WIDTH, D_MODEL), lambda i, j: (0, 0)),
                  pl.BlockSpec((MEM_WIDTH, D_MODEL), lambda i, j: (0, 0)),
                  pl.BlockSpec((1, MIX_WIDTH), lambda i, j: (0, 0)),
                  pl.BlockSpec((1, D_MODEL), lambda i, j: (0, 0))],
        out_specs=tile(D_MODEL),
        out_shape=jax.ShapeDtypeStruct((b, s, D_MODEL), F32),
        compiler_params=pltpu.CompilerParams(vmem_limit_bytes=VMEM_LIMIT),
        name="outproj_final" if final_norm else "outproj",
    )(mix, z, mq, mkv, h, wmix, wmem, hg, fg)


def _row(v):
    return v.reshape(1, -1).astype(F32)


def kernel(x, mem, positions, norm_0, w_in_0, conv_w_0, a_log_0, dt_bias_0, o_norm_0,
           mem_norm_0, w_mem_kv_0, w_out_0, norm_1, w_in_1, mem_norm_1, w_mem_kv_1,
           w_out_1, final_norm):
    b, s, _ = x.shape
    assert s % (2 * MOBA_BLOCK) == 0 and MOBA_TOPK <= s // MOBA_BLOCK <= NB_PAD
    assert s % (DELTA_GROUP * DELTA_CHUNK) == 0
    tm = 512

    mkv0, mkv1 = _memkv(mem, _row(mem_norm_0), w_mem_kv_0.astype(BF16),
                        _row(mem_norm_1), w_mem_kv_1.astype(BF16))

    i1 = 3 * MIX_WIDTH
    i2 = i1 + GATE_WIDTH
    i3 = i2 + MEM_WIDTH
    w0 = w_in_0.astype(BF16)
    wba = jnp.pad(w0[:, i3:], ((0, 0), (0, LANES - 2 * MIX_HEADS)))
    qkv, z0, mq0, ba = _inproj0(x, _row(norm_0), w0[:, :i1], w0[:, i1:i2], w0[:, i2:i3], wba, tm)
    ba_cols = lambda v: _row(jnp.pad(v, (MIX_HEADS, LANES - 2 * MIX_HEADS)))
    o0 = _delta_mixer(qkv, ba, conv_w_0.astype(F32), ba_cols(a_log_0), ba_cols(dt_bias_0))
    wo0 = w_out_0.astype(BF16)
    h1 = _outproj(o0, z0, mq0, mkv0, x, wo0[:MIX_WIDTH], wo0[MIX_WIDTH:],
                  _row(jnp.tile(o_norm_0, MIX_HEADS)), _row(final_norm),
                  tm, head_norm=True, final_norm=False)

    w1 = w_in_1.astype(BF16)
    half = HEAD_DIM // 2
    inv_freq = ROPE_THETA ** (-jnp.arange(half, dtype=F32) * (2.0 / HEAD_DIM))
    invf = _row(jnp.tile(inv_freq, LANES // half))
    q1, k1, v1, z1, mq1 = _inproj1(
        h1, _row(norm_1), positions.reshape(b, s, 1), invf,
        w1[:, :MIX_WIDTH], w1[:, MIX_WIDTH:2 * MIX_WIDTH], w1[:, 2 * MIX_WIDTH:i1],
        w1[:, i1:i2], w1[:, i2:], tm)
    o1 = _moba(q1, k1, v1)
    wo1 = w_out_1.astype(BF16)
    return _outproj(o1, z1, mq1, mkv1, h1, wo1[:MIX_WIDTH], wo1[MIX_WIDTH:],
                    jnp.ones((1, MIX_WIDTH), F32), _row(final_norm),
                    tm, head_norm=False, final_norm=True)
```

```python
import functools

import jax
import jax.numpy as jnp
from jax import lax
from jax.experimental import pallas as pl
from jax.experimental.pallas import tpu as pltpu

D_MODEL = 1024
HEAD_DIM = 64
MIX_HEADS = 12
MEM_HEADS = 4
MIX_WIDTH = MIX_HEADS * HEAD_DIM
MEM_WIDTH = MEM_HEADS * HEAD_DIM
GATE_WIDTH = MIX_WIDTH + MEM_WIDTH
N_MEM = 256
CONV_WIDTH = 4
MOBA_BLOCK = 256
MOBA_TOPK = 3
ROPE_THETA = 10000.0
NORM_EPS = 1e-6
MASK_VALUE = -1e30

LANES = 128
SUBLANES = 8
PAIR = 2 * HEAD_DIM
N_PAIRS = MIX_HEADS // 2
DELTA_CHUNK = 128
DELTA_GROUP = 8
INV_BASE = 16
QK_SCALE = HEAD_DIM ** -0.5
LOG2E = 1.4426950408889634
MOBA_PAIRS = 2
NB_PAD = 16
VT_ROWS = HEAD_DIM + 16
VMEM_LIMIT = 56 * 1024 * 1024

BF16 = jnp.bfloat16
F32 = jnp.float32

_NT = (((1,), (1,)), ((), ()))


def _dot(a, b):
    return jnp.dot(a.astype(BF16), b.astype(BF16), preferred_element_type=F32)


def _dot_nt(a, b):
    return lax.dot_general(a.astype(BF16), b.astype(BF16), _NT, preferred_element_type=F32)


def _dot_hi(a, b):
    return jnp.dot(a, b, precision=lax.Precision.HIGHEST, preferred_element_type=F32)


def _dot_nt_hi(a, b):
    return lax.dot_general(a, b, _NT, precision=lax.Precision.HIGHEST,
                           preferred_element_type=F32)


def _rms_scale(x):
    return lax.rsqrt(jnp.mean(x * x, axis=-1, keepdims=True) + NORM_EPS)


def _sigmoid(x):
    return 1.0 / (1.0 + jnp.exp(-x))


def _softplus(x):
    return jnp.maximum(x, 0.0) + jnp.log(1.0 + jnp.exp(-jnp.abs(x)))


def _head_masks():
    lane = lax.broadcasted_iota(jnp.int32, (1, PAIR), 1)
    return [((lane >> 6) == h).astype(F32) for h in range(2)]


def _memkv_kernel(mem_ref, g0_ref, w0_ref, g1_ref, w1_ref, o0_ref, o1_ref):
    m = mem_ref[0]
    mn = m * _rms_scale(m)
    o0_ref[0] = _dot(mn * g0_ref[...], w0_ref[...]).astype(o0_ref.dtype)
    o1_ref[0] = _dot(mn * g1_ref[...], w1_ref[...]).astype(o1_ref.dtype)


def _memkv(mem, g0, w0, g1, w1):
    b = mem.shape[0]
    full = lambda shp: pl.BlockSpec(shp, lambda i: (0,) * len(shp))
    return pl.pallas_call(
        _memkv_kernel,
        grid=(b,),
        in_specs=[pl.BlockSpec((1, N_MEM, D_MODEL), lambda i: (i, 0, 0)),
                  full((1, D_MODEL)), full((D_MODEL, 2 * MEM_WIDTH)),
                  full((1, D_MODEL)), full((D_MODEL, 2 * MEM_WIDTH))],
        out_specs=[pl.BlockSpec((1, N_MEM, 2 * MEM_WIDTH), lambda i: (i, 0, 0))] * 2,
        out_shape=[jax.ShapeDtypeStruct((b, N_MEM, 2 * MEM_WIDTH), BF16)] * 2,
        compiler_params=pltpu.CompilerParams(vmem_limit_bytes=VMEM_LIMIT),
        name="memkv",
    )(mem, g0, w0, g1, w1)


def _inproj0_kernel(x_ref, g_ref, wqkv_ref, wz_ref, wmq_ref, wba_ref,
                    qkv_ref, z_ref, mq_ref, ba_ref):
    x = x_ref[0]
    xn = (x * _rms_scale(x) * g_ref[...]).astype(BF16)
    for w_ref, o_ref in ((wqkv_ref, qkv_ref), (wz_ref, z_ref),
                         (wmq_ref, mq_ref), (wba_ref, ba_ref)):
        o_ref[0] = jnp.dot(xn, w_ref[...], preferred_element_type=F32).astype(o_ref.dtype)


def _inproj0(x, g, wqkv, wz, wmq, wba, tm):
    b, s, _ = x.shape
    widths = (3 * MIX_WIDTH, GATE_WIDTH, MEM_WIDTH, LANES)
    wspec = lambda n: pl.BlockSpec((D_MODEL, n), lambda i, j: (0, 0))
    return pl.pallas_call(
        _inproj0_kernel,
        grid=(b, s // tm),
        in_specs=[pl.BlockSpec((1, tm, D_MODEL), lambda i, j: (i, j, 0)),
                  pl.BlockSpec((1, D_MODEL), lambda i, j: (0, 0))]
                 + [wspec(n) for n in widths],
        out_specs=[pl.BlockSpec((1, tm, n), lambda i, j: (i, j, 0)) for n in widths],
        out_shape=[jax.ShapeDtypeStruct((b, s, n), dt)
                   for n, dt in zip(widths, (BF16, BF16, BF16, F32))],
        compiler_params=pltpu.CompilerParams(vmem_limit_bytes=VMEM_LIMIT),
        name="inproj0",
    )(x, g, wqkv, wz, wmq, wba)


def _inproj1_kernel(x_ref, g_ref, pos_ref, invf_ref, wq_ref, wk_ref, wv_ref, wz_ref, wmq_ref,
                    q_ref, k_ref, v_ref, z_ref, mq_ref):
    x = x_ref[0]
    xn = (x * _rms_scale(x) * g_ref[...]).astype(BF16)
    ang = pos_ref[0].astype(F32) * invf_ref[...]
    cos = jnp.cos(ang)
    sin = jnp.sin(ang)
    lane = lax.broadcasted_iota(jnp.int32, ang.shape, 1)
    first_half = (lane & (HEAD_DIM - 1)) < (HEAD_DIM // 2)
    sin_signed = jnp.where(first_half, -sin, sin)
    for w_ref, o_ref in ((wq_ref, q_ref), (wk_ref, k_ref)):
        y = jnp.dot(xn, w_ref[...], preferred_element_type=F32)
        for sl in range(MIX_WIDTH // LANES):
            ys = y[:, sl * LANES:(sl + 1) * LANES]
            rot = jnp.where(first_half,
                            pltpu.roll(ys, LANES - HEAD_DIM // 2, 1),
                            pltpu.roll(ys, HEAD_DIM // 2, 1))
            o_ref[0, :, sl * LANES:(sl + 1) * LANES] = (ys * cos + rot * sin_signed).astype(o_ref.dtype)
    for w_ref, o_ref in ((wv_ref, v_ref), (wz_ref, z_ref), (wmq_ref, mq_ref)):
        o_ref[0] = jnp.dot(xn, w_ref[...], preferred_element_type=F32).astype(o_ref.dtype)


def _inproj1(x, g, pos3, invf, wq, wk, wv, wz, wmq, tm):
    b, s, _ = x.shape
    widths = (MIX_WIDTH, MIX_WIDTH, MIX_WIDTH, GATE_WIDTH, MEM_WIDTH)
    wspec = lambda n: pl.BlockSpec((D_MODEL, n), lambda i, j: (0, 0))
    return pl.pallas_call(
        _inproj1_kernel,
        grid=(b, s // tm),
        in_specs=[pl.BlockSpec((1, tm, D_MODEL), lambda i, j: (i, j, 0)),
                  pl.BlockSpec((1, D_MODEL), lambda i, j: (0, 0)),
                  pl.BlockSpec((1, tm, 1), lambda i, j: (i, j, 0)),
                  pl.BlockSpec((1, LANES), lambda i, j: (0, 0))]
                 + [wspec(n) for n in widths],
        out_specs=[pl.BlockSpec((1, tm, n), lambda i, j: (i, j, 0)) for n in widths],
        out_shape=[jax.ShapeDtypeStruct((b, s, n), BF16) for n in widths],
        compiler_params=pltpu.CompilerParams(vmem_limit_bytes=VMEM_LIMIT),
        name="inproj1",
    )(x, g, pos3, invf, wq, wk, wv, wz, wmq)


def _unit_lower_inverses(lows, row, col, eye, between):
    c = lows[0].shape[0]
    shift = INV_BASE.bit_length() - 1
    base = (row >> shift) == (col >> shift)
    ns = [jnp.where(base, -low, 0.0) for low in lows]
    xs = [eye + n for n in ns]
    ns = [_dot(n, n) for n in ns]
    between()
    power = 2
    while 2 * power < INV_BASE:
        prods = [_dot(jnp.concatenate([n, x], axis=0), n) for x, n in zip(xs, ns)]
        between()
        xs = [x + p[c:] for x, p in zip(xs, prods)]
        ns = [p[:c] for p in prods]
        power *= 2
    xs = [x + _dot(x, n) for x, n in zip(xs, ns)]
    between()
    size = INV_BASE
    while size < c:
        sh = size.bit_length() - 1
        sel = (((row >> (sh + 1)) == (col >> (sh + 1)))
               & (((row >> sh) & 1) == 1) & (((col >> sh) & 1) == 0))
        ys = [_dot(x, jnp.where(sel, low, 0.0)) for x, low in zip(xs, lows)]
        between()
        xs = [x - _dot(y, x) for x, y in zip(xs, ys)]
        between()
        size *= 2
    return xs


def _delta_kernel(q_ref, k_ref, v_ref, ba_ref, cwq_ref, cwk_ref, cwv_ref,
                  alog_ref, dt_ref, o_ref,
                  ext_ref, p_ref, qp_ref, n_ref, op_ref, egl_ref):
    c = DELTA_CHUNK
    grp = DELTA_GROUP
    rows = grp * c
    seq = q_ref.shape[1]
    pair = pl.program_id(1)
    masks = _head_masks()
    row = lax.broadcasted_iota(jnp.int32, (c, c), 0)
    col = lax.broadcasted_iota(jnp.int32, (c, c), 1)
    tri_incl = row >= col
    tri_strict = row > col
    eye = (row == col).astype(F32)
    tril_b = tri_incl.astype(BF16)
    head_block = ((row >> 6) == (col >> 6)).astype(F32)
    head_block_b = head_block.astype(BF16)
    row2 = lax.broadcasted_iota(jnp.int32, (LANES, 2 * PAIR), 0)
    col2 = lax.broadcasted_iota(jnp.int32, (LANES, 2 * PAIR), 1)
    expand = (row2 == jnp.where(col2 < PAIR, 0, MIX_HEADS) + 2 * pair + ((col2 >> 6) & 1)).astype(BF16)
    lane_n = lax.broadcasted_iota(jnp.int32, (1, LANES), 1)
    is_beta = lane_n < MIX_HEADS
    neg_a = -jnp.exp(alog_ref[...])
    dt_row = dt_ref[...]

    ext_ref[:, 0:SUBLANES, :] = jnp.zeros((3, SUBLANES, PAIR), F32)

    def conv_silu(idx, x_ref, cw_ref, t0):
        raw = x_ref[0, pl.ds(t0, rows), :].astype(F32)
        ext_ref[idx, SUBLANES:SUBLANES + rows, :] = raw
        acc = raw * cw_ref[CONV_WIDTH - 1:CONV_WIDTH, :]
        for tap in range(CONV_WIDTH - 1):
            back = CONV_WIDTH - 1 - tap
            acc = acc + ext_ref[idx, SUBLANES - back:SUBLANES - back + rows, :] * cw_ref[tap:tap + 1, :]
        ext_ref[idx, 0:SUBLANES, :] = raw[rows - SUBLANES:rows, :]
        return acc * _sigmoid(acc)

    def l2norm(x):
        return x * lax.rsqrt(_dot(x * x, head_block_b) + NORM_EPS)

    def build_group(gi, between):
        t0 = pl.multiple_of(gi * rows, rows)
        qn = l2norm(conv_silu(0, q_ref, cwq_ref, t0))
        kn = l2norm(conv_silu(1, k_ref, cwk_ref, t0))
        vc = conv_silu(2, v_ref, cwv_ref, t0)
        ba = ba_ref[0, pl.ds(t0, rows), :]
        act = jnp.where(is_beta, _sigmoid(ba), neg_a * _softplus(ba + dt_row))
        bg = jnp.dot(act.astype(BF16), expand, preferred_element_type=F32)
        beta = bg[:, :PAIR]
        g = bg[:, PAIR:]
        kb = kn * beta
        vb = vc * beta
        sls = [slice(ci * c, (ci + 1) * c) for ci in range(grp)]
        gcs = [jnp.dot(tril_b, g[sl].astype(BF16), preferred_element_type=F32) for sl in sls]
        between()
        gcts = [gc.T for gc in gcs]
        egs = [jnp.exp(gc) for gc in gcs]
        probs = [(ci, h) for ci in range(grp) for h in range(2)]
        decays = []
        for ci, h in probs:
            gcol = jnp.broadcast_to(gcs[ci][:, h * HEAD_DIM:h * HEAD_DIM + 1], (c, c))
            grow = gcts[ci][h * HEAD_DIM:h * HEAD_DIM + 1, :]
            decays.append(jnp.exp(jnp.where(tri_incl, gcol - grow, -jnp.inf)))
        kbhs = [kb[sls[ci]] * masks[h] for ci, h in probs]
        kq = [_dot_nt(jnp.concatenate(
            [kbhs[2 * ci], kbhs[2 * ci + 1]]
            + [qn[sls[ci]] * (masks[h] * QK_SCALE) for h in range(2)], axis=0), kn[sls[ci]])
            for ci in range(grp)]
        between()
        lows = [jnp.where(tri_strict, kq[ci][h * c:(h + 1) * c] * d, 0.0)
                for (ci, h), d in zip(probs, decays)]
        aqks = [jnp.where(tri_incl, kq[ci][(2 + h) * c:(3 + h) * c] * d, 0.0)
                for (ci, h), d in zip(probs, decays)]
        tinvs = _unit_lower_inverses(lows, row, col, eye, between)
        wus = [_dot(t, jnp.concatenate([kbh * egs[ci], vb[sls[ci]] * masks[h]], axis=1))
               for t, kbh, (ci, h) in zip(tinvs, kbhs, probs)]
        aus = [_dot(a, wu) for a, wu in zip(aqks, wus)]
        kds = [kn[sls[ci]] * jnp.exp(gcs[ci][c - 1:c, :] - gcs[ci]) for ci in range(grp)]
        pns = [_dot(kds[ci].T, wus[2 * ci] + wus[2 * ci + 1]) for ci in range(grp)]
        for ci in range(grp):
            cidx = gi * grp + ci
            au = aus[2 * ci] + aus[2 * ci + 1]
            p_ref[cidx] = (head_block * pns[ci][:, :PAIR]).astype(BF16)
            n_ref[cidx] = head_block * pns[ci][:, PAIR:]
            qp_ref[cidx] = (qn[sls[ci]] * (QK_SCALE * egs[ci]) - au[:, :PAIR]).astype(BF16)
            op_ref[cidx] = au[:, PAIR:]
            egl_ref[pl.ds(cidx, 1), :] = jnp.exp(gcs[ci][c - 1:c, :])

    def scan_step(ci, state):
        sb = state.astype(BF16)
        o = jnp.dot(qp_ref[ci], sb, preferred_element_type=F32) + op_ref[ci]
        new_state = (state * egl_ref[pl.ds(ci, 1), :]
                     - jnp.dot(p_ref[ci], sb, preferred_element_type=F32) + n_ref[ci])
        o_ref[0, pl.ds(pl.multiple_of(ci * c, c), c), :] = o.astype(o_ref.dtype)
        return new_state

    def build_and_scan(gi, state):
        box = [state]
        pending = iter(range(grp))

        def one_scan_step():
            ci = next(pending, None)
            if ci is not None:
                box[0] = scan_step((gi - 1) * grp + ci, box[0])

        build_group(gi, one_scan_step)
        for ci in pending:
            box[0] = scan_step((gi - 1) * grp + ci, box[0])
        return box[0]

    n_groups = seq // rows
    build_group(0, lambda: None)
    state = lax.fori_loop(1, n_groups, build_and_scan, jnp.zeros((PAIR, PAIR), F32))
    lax.fori_loop((n_groups - 1) * grp, n_groups * grp, scan_step, state)


def _delta_mixer(qkv, ba, conv_w, alog_row, dt_row):
    b, s, _ = qkv.shape
    n_chunks = s // DELTA_CHUNK
    seq_spec = lambda off: pl.BlockSpec((1, s, PAIR), lambda i, p, off=off: (i, 0, off + p))
    cw_spec = lambda off: pl.BlockSpec((CONV_WIDTH, PAIR), lambda i, p, off=off: (0, off + p))
    row_spec = pl.BlockSpec((1, LANES), lambda i, p: (0, 0))
    return pl.pallas_call(
        _delta_kernel,
        grid=(b, N_PAIRS),
        in_specs=[seq_spec(0), seq_spec(N_PAIRS), seq_spec(2 * N_PAIRS),
                  pl.BlockSpec((1, s, LANES), lambda i, p: (i, 0, 0)),
                  cw_spec(0), cw_spec(N_PAIRS), cw_spec(2 * N_PAIRS),
                  row_spec, row_spec],
        out_specs=pl.BlockSpec((1, s, PAIR), lambda i, p: (i, 0, p)),
        out_shape=jax.ShapeDtypeStruct((b, s, MIX_WIDTH), BF16),
        scratch_shapes=[pltpu.VMEM((3, SUBLANES + DELTA_GROUP * DELTA_CHUNK, PAIR), F32),
                        pltpu.VMEM((n_chunks, PAIR, PAIR), BF16),
                        pltpu.VMEM((n_chunks, DELTA_CHUNK, PAIR), BF16),
                        pltpu.VMEM((n_chunks, PAIR, PAIR), F32),
                        pltpu.VMEM((n_chunks, DELTA_CHUNK, PAIR), F32),
                        pltpu.VMEM((n_chunks, PAIR), F32)],
        compiler_params=pltpu.CompilerParams(vmem_limit_bytes=VMEM_LIMIT),
        name="delta_mixer",
    )(qkv, qkv, qkv, ba, conv_w, conv_w, conv_w, alog_row, dt_row)


def _moba_kernel(q_ref, k_ref, v_ref, o_ref, kaug_ref, vt_ref, kmean_ref, qaug_ref, sa_ref, sb_ref):
    blk = MOBA_BLOCK
    qw = 2 * blk
    seq = k_ref.shape[1]
    nb = seq // blk
    qi = pl.program_id(2)
    masks = _head_masks()
    n_pairs = MOBA_PAIRS
    heads = range(2 * n_pairs)

    @pl.when(qi == 0)
    def _prepare():
        lane = lax.broadcasted_iota(jnp.int32, (blk, LANES), 1)
        ones_row = (lax.broadcasted_iota(jnp.int32, (VT_ROWS - HEAD_DIM, LANES), 0) == 0).astype(BF16)

        def key_block(j, carry):
            r0 = pl.multiple_of(j * blk, blk)
            onehot = (lane == j).astype(BF16)
            for pr in range(n_pairs):
                kj = k_ref[0, pl.ds(r0, blk), pr * PAIR:(pr + 1) * PAIR]
                vj = v_ref[0, pl.ds(r0, blk), pr * PAIR:(pr + 1) * PAIR].astype(F32)
                kaug_ref[pr, pl.ds(r0, blk), 0:LANES] = kj
                kaug_ref[pr, pl.ds(r0, blk), LANES:2 * LANES] = onehot
                for t in range(blk // LANES):
                    c0 = pl.multiple_of(r0 + t * LANES, LANES)
                    vt = vj[t * LANES:(t + 1) * LANES, :].T.astype(BF16)
                    for h in range(2):
                        vt_ref[2 * pr + h, 0:HEAD_DIM, pl.ds(c0, LANES)] = vt[h * HEAD_DIM:(h + 1) * HEAD_DIM, :]
                        vt_ref[2 * pr + h, HEAD_DIM:VT_ROWS, pl.ds(c0, LANES)] = ones_row
                kmean_ref[pr, pl.ds(j, 1), :] = jnp.sum(kj.astype(F32), axis=0, keepdims=True) * (1.0 / blk)
            return carry
        kmean_ref[...] = jnp.zeros((n_pairs, NB_PAD, PAIR), F32)
        lax.fori_loop(0, nb, key_block, 0)

        kstacks = []
        for pr in range(n_pairs):
            pieces = []
            for h in range(2):
                rest = kmean_ref[pr] * masks[h]
                for _ in range(3):
                    part = rest.astype(BF16)
                    pieces.append(part)
                    rest = rest - part.astype(F32)
            kstacks.append(jnp.concatenate(pieces, axis=0))
        blk_id = lax.broadcasted_iota(jnp.int32, (NB_PAD, qw), 0)
        blk_f = blk_id.astype(F32)
        half = lax.broadcasted_iota(jnp.int32, (NB_PAD, qw), 1) >> (blk.bit_length() - 1)
        head_rows = lax.broadcasted_iota(jnp.int32, (PAIR, qw), 0) >> 6
        pad_rows = jnp.zeros((PAIR - NB_PAD, qw), BF16)

        def query_blocks(m, carry):
            c0 = pl.multiple_of(m * qw, qw)
            own_blk = 2 * m + half
            valid = blk_id < own_blk
            q_ts, gs, opens = [], [], []
            for pr in range(n_pairs):
                q = q_ref[0, pl.ds(c0, qw), pr * PAIR:(pr + 1) * PAIR]
                qf = q.astype(F32) * (QK_SCALE * LOG2E)
                q_ts.append(jnp.concatenate(
                    [qf[t * LANES:(t + 1) * LANES, :].T for t in range(qw // LANES)], axis=1))
                gate_parts = lax.dot_general(kstacks[pr], q, _NT, preferred_element_type=F32)
                for h in range(2):
                    g3 = [gate_parts[(3 * h + i) * NB_PAD:(3 * h + i + 1) * NB_PAD] for i in range(3)]
                    gs.append(jnp.where(valid, g3[0] + g3[1] + g3[2], -jnp.inf))
                    opens.append((blk_id == own_blk).astype(F32))
            for _ in range(MOBA_TOPK):
                for hh in heads:
                    top = jnp.max(gs[hh], axis=0, keepdims=True)
                    first = jnp.min(jnp.where(gs[hh] == top, blk_f, float(NB_PAD)), axis=0, keepdims=True)
                    pick = (blk_f == first) & (top > -jnp.inf)
                    opens[hh] = jnp.where(pick, 1.0, opens[hh])
                    gs[hh] = jnp.where(pick, -jnp.inf, gs[hh])
            for hh in heads:
                bias = jnp.where(opens[hh] > 0.5, 0.0, MASK_VALUE).astype(BF16)
                qaug_ref[hh, 0:PAIR, pl.ds(c0, qw)] = jnp.where(
                    head_rows == hh % 2, q_ts[hh // 2], 0.0).astype(BF16)
                qaug_ref[hh, PAIR:PAIR + NB_PAD, pl.ds(c0, qw)] = bias
                qaug_ref[hh, PAIR + NB_PAD:2 * PAIR, pl.ds(c0, qw)] = pad_rows
            return carry
        lax.fori_loop(0, nb // 2, query_blocks, 0)

    qs = [qaug_ref[hh, :, pl.ds(pl.multiple_of(qi * qw, qw), qw)] for hh in heads]

    key_pos = lax.broadcasted_iota(jnp.int32, (blk, blk), 0)
    qry_pos = lax.broadcasted_iota(jnp.int32, (blk, blk), 1)
    always = jnp.int32(1 << 20)

    def scores(t, s_ref):
        r0 = pl.multiple_of(t * (2 * blk), 2 * blk)
        k2 = [kaug_ref[pr, pl.ds(r0, 2 * blk), :] for pr in range(n_pairs)]
        sts = [jnp.dot(k2[hh // 2], qs[hh], preferred_element_type=F32) for hh in heads]
        causal = key_pos <= qry_pos + jnp.where(t == qi, 0, always)
        maxima = []
        for hh, st in enumerate(sts):
            st = jnp.concatenate(
                [jnp.concatenate([jnp.where(causal, st[:blk, :blk], MASK_VALUE), st[:blk, blk:]], axis=1),
                 jnp.concatenate([st[blk:, :blk], jnp.where(causal, st[blk:, blk:], MASK_VALUE)], axis=1)],
                axis=0)
            s_ref[hh] = st
            maxima.append(jnp.max(st, axis=0, keepdims=True))
        return maxima

    def update(t, state, maxima, s_ref):
        r0 = pl.multiple_of(t * (2 * blk), 2 * blk)
        m_new = [jnp.maximum(state[2 * hh], maxima[hh]) for hh in heads]
        ps = [jnp.exp2(s_ref[hh] - m_new[hh]).astype(BF16) for hh in heads]
        pvs = [jnp.dot(vt_ref[hh, :, pl.ds(r0, 2 * blk)], ps[hh], preferred_element_type=F32)
               for hh in heads]
        out = []
        for hh in heads:
            alpha = jnp.exp2(state[2 * hh] - m_new[hh])
            out += [m_new[hh], state[2 * hh + 1] * alpha + pvs[hh]]
        return out

    init = []
    for hh in heads:
        init += [jnp.full((1, qw), MASK_VALUE, F32), jnp.zeros((VT_ROWS, qw), F32)]
    n_state = len(init)

    def two_pairs(u, carry):
        t0 = 2 * u
        max_b = scores(t0 + 1, sb_ref)
        state = update(t0, carry[:n_state], carry[n_state:], sa_ref)
        max_a = scores(t0 + 2, sa_ref)
        state = update(t0 + 1, state, max_b, sb_ref)
        return tuple(state + max_a)

    carry = lax.fori_loop(0, qi // 2, two_pairs, tuple(init + scores(0, sa_ref)))

    def odd_tail(carry):
        max_b = scores(qi, sb_ref)
        state = update(qi - 1, carry[:n_state], carry[n_state:], sa_ref)
        return tuple(update(qi, state, max_b, sb_ref))

    def even_tail(carry):
        return tuple(update(qi, carry[:n_state], carry[n_state:], sa_ref))

    fin = lax.cond((qi & 1) == 1, odd_tail, even_tail, carry)

    for pr in range(n_pairs):
        out_t = jnp.concatenate(
            [fin[2 * hh + 1][:HEAD_DIM] / fin[2 * hh + 1][HEAD_DIM:HEAD_DIM + 1]
             for hh in (2 * pr, 2 * pr + 1)], axis=0)
        o_ref[0, :, pr * PAIR:(pr + 1) * PAIR] = out_t.T.astype(o_ref.dtype)


def _moba(q, k, v):
    b, s, _ = q.shape
    nb = s // MOBA_BLOCK
    width = MOBA_PAIRS * PAIR
    n_heads = 2 * MOBA_PAIRS
    seq_spec = pl.BlockSpec((1, s, width), lambda i, p, j: (i, 0, p))
    return pl.pallas_call(
        _moba_kernel,
        grid=(b, N_PAIRS // MOBA_PAIRS, nb // 2),
        in_specs=[seq_spec, seq_spec, seq_spec],
        out_specs=pl.BlockSpec((1, 2 * MOBA_BLOCK, width), lambda i, p, j: (i, j, p)),
        out_shape=jax.ShapeDtypeStruct((b, s, MIX_WIDTH), BF16),
        scratch_shapes=[pltpu.VMEM((MOBA_PAIRS, s, 2 * LANES), BF16),
                        pltpu.VMEM((n_heads, VT_ROWS, s), BF16),
                        pltpu.VMEM((MOBA_PAIRS, NB_PAD, PAIR), F32),
                        pltpu.VMEM((n_heads, 2 * LANES, s), BF16),
                        pltpu.VMEM((n_heads, 2 * MOBA_BLOCK, 2 * MOBA_BLOCK), F32),
                        pltpu.VMEM((n_heads, 2 * MOBA_BLOCK, 2 * MOBA_BLOCK), F32)],
        compiler_params=pltpu.CompilerParams(
            dimension_semantics=("arbitrary", "arbitrary", "arbitrary"),
            vmem_limit_bytes=VMEM_LIMIT),
        name="moba",
    )(q, k, v)


def _outproj_kernel(mix_ref, z_ref, mq_ref, mkv_ref, h_ref, wmix_ref, wmem_ref, hg_ref, fg_ref,
                    o_ref, *, head_norm, final_norm):
    mq = mq_ref[0].astype(F32)
    mk = mkv_ref[0, :, :MEM_WIDTH]
    mv = mkv_ref[0, :, MEM_WIDTH:]
    row = lax.broadcasted_iota(jnp.int32, (MEM_WIDTH, MEM_WIDTH), 0)
    col = lax.broadcasted_iota(jnp.int32, (MEM_WIDTH, MEM_WIDTH), 1)
    lane = lax.broadcasted_iota(jnp.int32, (1, MEM_WIDTH), 1)
    mhs = [((lane >> 6) == h).astype(F32) for h in range(MEM_HEADS)]
    logits = [_dot_nt(mq * (mh * QK_SCALE), mk) for mh in mhs]
    es = [jnp.exp(lg - jnp.max(lg, axis=-1, keepdims=True)) for lg in logits]
    inv = [1.0 / jnp.sum(e, axis=-1, keepdims=True) for e in es]
    pvs = [jnp.dot(e.astype(BF16), mv, preferred_element_type=F32) for e in es]
    memo = pvs[0] * (mhs[0] * inv[0])
    for h in range(1, MEM_HEADS):
        memo = memo + pvs[h] * (mhs[h] * inv[h])
    z = z_ref[0].astype(F32)
    gate = z * _sigmoid(z)
    mix = mix_ref[0].astype(F32)
    if head_norm:
        same_head = ((row >> 6) == (col >> 6)).astype(BF16)
        slabs = []
        for sl in range(MIX_WIDTH // MEM_WIDTH):
            ms = mix[:, sl * MEM_WIDTH:(sl + 1) * MEM_WIDTH]
            ss = _dot(ms * ms, same_head) * (1.0 / HEAD_DIM)
            slabs.append(ms * lax.rsqrt(ss + NORM_EPS))
        mix = jnp.concatenate(slabs, axis=1) * hg_ref[...]
    y_mix = mix * gate[:, :MIX_WIDTH]
    y_mem = memo * gate[:, MIX_WIDTH:]
    out = h_ref[0] + _dot(y_mix, wmix_ref[...]) + _dot(y_mem, wmem_ref[...])
    if final_norm:
        out = out * _rms_scale(out) * fg_ref[...]
    o_ref[0] = out


def _outproj(mix, z, mq, mkv, h, wmix, wmem, hg, fg, tm, head_norm, final_norm):
    b, s, _ = h.shape
    tile = lambda n: pl.BlockSpec((1, tm, n), lambda i, j: (i, j, 0))
    return pl.pallas_call(
        functools.partial(_outproj_kernel, head_norm=head_norm, final_norm=final_norm),
        grid=(b, s // tm),
        in_specs=[tile(MIX_WIDTH), tile(GATE_WIDTH), tile(MEM_WIDTH),
                  pl.BlockSpec((1, N_MEM, 2 * MEM_WIDTH), lambda i, j: (i, 0, 0)),
                  tile(D_MODEL),
                  pl.BlockSpec((MIX_WIDTH, D_MODEL), lambda i, j: (0, 0)),
                  pl.BlockSpec((MEM_WIDTH, D_MODEL), lambda i, j: (0, 0)),
                  pl.BlockSpec((1, MIX_WIDTH), lambda i, j: (0, 0)),
                  pl.BlockSpec((1, D_MODEL), lambda i, j: (0, 0))],
        out_specs=tile(D_MODEL),
        out_shape=jax.ShapeDtypeStruct((b, s, D_MODEL), F32),
        compiler_params=pltpu.CompilerParams(vmem_limit_bytes=VMEM_LIMIT),
        name="outproj_final" if final_norm else "outproj",
    )(mix, z, mq, mkv, h, wmix, wmem, hg, fg)


def _row(v):
    return v.reshape(1, -1).astype(F32)


def kernel(x, mem, positions, norm_0, w_in_0, conv_w_0, a_log_0, dt_bias_0, o_norm_0,
           mem_norm_0, w_mem_kv_0, w_out_0, norm_1, w_in_1, mem_norm_1, w_mem_kv_1,
           w_out_1, final_norm):
    b, s, _ = x.shape
    assert s % (2 * MOBA_BLOCK) == 0 and MOBA_TOPK <= s // MOBA_BLOCK <= NB_PAD
    assert s % (DELTA_GROUP * DELTA_CHUNK) == 0
    tm = 512

    mkv0, mkv1 = _memkv(mem, _row(mem_norm_0), w_mem_kv_0.astype(BF16),
                        _row(mem_norm_1), w_mem_kv_1.astype(BF16))

    i1 = 3 * MIX_WIDTH
    i2 = i1 + GATE_WIDTH
    i3 = i2 + MEM_WIDTH
    w0 = w_in_0.astype(BF16)
    wba = jnp.pad(w0[:, i3:], ((0, 0), (0, LANES - 2 * MIX_HEADS)))
    qkv, z0, mq0, ba = _inproj0(x, _row(norm_0), w0[:, :i1], w0[:, i1:i2], w0[:, i2:i3], wba, tm)
    ba_cols = lambda v: _row(jnp.pad(v, (MIX_HEADS, LANES - 2 * MIX_HEADS)))
    o0 = _delta_mixer(qkv, ba, conv_w_0.astype(F32), ba_cols(a_log_0), ba_cols(dt_bias_0))
    wo0 = w_out_0.astype(BF16)
    h1 = _outproj(o0, z0, mq0, mkv0, x, wo0[:MIX_WIDTH], wo0[MIX_WIDTH:],
                  _row(jnp.tile(o_norm_0, MIX_HEADS)), _row(final_norm),
                  tm, head_norm=True, final_norm=False)

    w1 = w_in_1.astype(BF16)
    half = HEAD_DIM // 2
    inv_freq = ROPE_THETA ** (-jnp.arange(half, dtype=F32) * (2.0 / HEAD_DIM))
    invf = _row(jnp.tile(inv_freq, LANES // half))
    q1, k1, v1, z1, mq1 = _inproj1(
        h1, _row(norm_1), positions.reshape(b, s, 1), invf,
        w1[:, :MIX_WIDTH], w1[:, MIX_WIDTH:2 * MIX_WIDTH], w1[:, 2 * MIX_WIDTH:i1],
        w1[:, i1:i2], w1[:, i2:], tm)
    o1 = _moba(q1, k1, v1)
    wo1 = w_out_1.astype(BF16)
    return _outproj(o1, z1, mq1, mkv1, h1, wo1[:MIX_WIDTH], wo1[MIX_WIDTH:],
                    jnp.ones((1, MIX_WIDTH), F32), _row(final_norm),
                    tm, head_norm=False, final_norm=True)
```

```python
import functools

import jax
import jax.numpy as jnp
from jax import lax
from jax.experimental import pallas as pl
from jax.experimental.pallas import tpu as pltpu

D_MODEL = 1024
HEAD_DIM = 64
MIX_HEADS = 12
MEM_HEADS = 4
MIX_WIDTH = MIX_HEADS * HEAD_DIM
MEM_WIDTH = MEM_HEADS * HEAD_DIM
GATE_WIDTH = MIX_WIDTH + MEM_WIDTH
N_MEM = 256
CONV_WIDTH = 4
MOBA_BLOCK = 256
MOBA_TOPK = 3
ROPE_THETA = 10000.0
NORM_EPS = 1e-6
MASK_VALUE = -1e30

LANES = 128
SUBLANES = 8
PAIR = 2 * HEAD_DIM
N_PAIRS = MIX_HEADS // 2
DELTA_CHUNK = 128
DELTA_GROUP = 8
INV_BASE = 16
QK_SCALE = HEAD_DIM ** -0.5
LOG2E = 1.4426950408889634
MOBA_PAIRS = 2
NB_PAD = 16
VT_ROWS = HEAD_DIM + 16
VMEM_LIMIT = 56 * 1024 * 1024

BF16 = jnp.bfloat16
F32 = jnp.float32

_NT = (((1,), (1,)), ((), ()))


def _dot(a, b):
    return jnp.dot(a.astype(BF16), b.astype(BF16), preferred_element_type=F32)


def _dot_nt(a, b):
    return lax.dot_general(a.astype(BF16), b.astype(BF16), _NT, preferred_element_type=F32)


def _dot_hi(a, b):
    return jnp.dot(a, b, precision=lax.Precision.HIGHEST, preferred_element_type=F32)


def _dot_nt_hi(a, b):
    return lax.dot_general(a, b, _NT, precision=lax.Precision.HIGHEST,
                           preferred_element_type=F32)


def _rms_scale(x):
    return lax.rsqrt(jnp.mean(x * x, axis=-1, keepdims=True) + NORM_EPS)


def _sigmoid(x):
    return 1.0 / (1.0 + jnp.exp(-x))


def _softplus(x):
    return jnp.maximum(x, 0.0) + jnp.log(1.0 + jnp.exp(-jnp.abs(x)))


def _head_masks():
    lane = lax.broadcasted_iota(jnp.int32, (1, PAIR), 1)
    return [((lane >> 6) == h).astype(F32) for h in range(2)]


def _memkv_kernel(mem_ref, g0_ref, w0_ref, g1_ref, w1_ref, o0_ref, o1_ref):
    m = mem_ref[0]
    mn = m * _rms_scale(m)
    o0_ref[0] = _dot(mn * g0_ref[...], w0_ref[...]).astype(o0_ref.dtype)
    o1_ref[0] = _dot(mn * g1_ref[...], w1_ref[...]).astype(o1_ref.dtype)


def _memkv(mem, g0, w0, g1, w1):
    b = mem.shape[0]
    full = lambda shp: pl.BlockSpec(shp, lambda i: (0,) * len(shp))
    return pl.pallas_call(
        _memkv_kernel,
        grid=(b,),
        in_specs=[pl.BlockSpec((1, N_MEM, D_MODEL), lambda i: (i, 0, 0)),
                  full((1, D_MODEL)), full((D_MODEL, 2 * MEM_WIDTH)),
                  full((1, D_MODEL)), full((D_MODEL, 2 * MEM_WIDTH))],
        out_specs=[pl.BlockSpec((1, N_MEM, 2 * MEM_WIDTH), lambda i: (i, 0, 0))] * 2,
        out_shape=[jax.ShapeDtypeStruct((b, N_MEM, 2 * MEM_WIDTH), BF16)] * 2,
        compiler_params=pltpu.CompilerParams(vmem_limit_bytes=VMEM_LIMIT),
        name="memkv",
    )(mem, g0, w0, g1, w1)


def _inproj0_kernel(x_ref, g_ref, wqkv_ref, wz_ref, wmq_ref, wba_ref,
                    qkv_ref, z_ref, mq_ref, ba_ref):
    x = x_ref[0]
    xn = (x * _rms_scale(x) * g_ref[...]).astype(BF16)
    for w_ref, o_ref in ((wqkv_ref, qkv_ref), (wz_ref, z_ref),
                         (wmq_ref, mq_ref), (wba_ref, ba_ref)):
        o_ref[0] = jnp.dot(xn, w_ref[...], preferred_element_type=F32).astype(o_ref.dtype)


def _inproj0(x, g, wqkv, wz, wmq, wba, tm):
    b, s, _ = x.shape
    widths = (3 * MIX_WIDTH, GATE_WIDTH, MEM_WIDTH, LANES)
    wspec = lambda n: pl.BlockSpec((D_MODEL, n), lambda i, j: (0, 0))
    return pl.pallas_call(
        _inproj0_kernel,
        grid=(b, s // tm),
        in_specs=[pl.BlockSpec((1, tm, D_MODEL), lambda i, j: (i, j, 0)),
                  pl.BlockSpec((1, D_MODEL), lambda i, j: (0, 0))]
                 + [wspec(n) for n in widths],
        out_specs=[pl.BlockSpec((1, tm, n), lambda i, j: (i, j, 0)) for n in widths],
        out_shape=[jax.ShapeDtypeStruct((b, s, n), dt)
                   for n, dt in zip(widths, (BF16, BF16, BF16, F32))],
        compiler_params=pltpu.CompilerParams(vmem_limit_bytes=VMEM_LIMIT),
        name="inproj0",
    )(x, g, wqkv, wz, wmq, wba)


def _inproj1_kernel(x_ref, g_ref, pos_ref, invf_ref, wq_ref, wk_ref, wv_ref, wz_ref, wmq_ref,
                    q_ref, k_ref, v_ref, z_ref, mq_ref):
    x = x_ref[0]
    xn = (x * _rms_scale(x) * g_ref[...]).astype(BF16)
    ang = pos_ref[0].astype(F32) * invf_ref[...]
    cos = jnp.cos(ang)
    sin = jnp.sin(ang)
    lane = lax.broadcasted_iota(jnp.int32, ang.shape, 1)
    first_half = (lane & (HEAD_DIM - 1)) < (HEAD_DIM // 2)
    sin_signed = jnp.where(first_half, -sin, sin)
    for w_ref, o_ref in ((wq_ref, q_ref), (wk_ref, k_ref)):
        y = jnp.dot(xn, w_ref[...], preferred_element_type=F32)
        for sl in range(MIX_WIDTH // LANES):
            ys = y[:, sl * LANES:(sl + 1) * LANES]
            rot = jnp.where(first_half,
                            pltpu.roll(ys, LANES - HEAD_DIM // 2, 1),
                            pltpu.roll(ys, HEAD_DIM // 2, 1))
            o_ref[0, :, sl * LANES:(sl + 1) * LANES] = (ys * cos + rot * sin_signed).astype(o_ref.dtype)
    for w_ref, o_ref in ((wv_ref, v_ref), (wz_ref, z_ref), (wmq_ref, mq_ref)):
        o_ref[0] = jnp.dot(xn, w_ref[...], preferred_element_type=F32).astype(o_ref.dtype)


def _inproj1(x, g, pos3, invf, wq, wk, wv, wz, wmq, tm):
    b, s, _ = x.shape
    widths = (MIX_WIDTH, MIX_WIDTH, MIX_WIDTH, GATE_WIDTH, MEM_WIDTH)
    wspec = lambda n: pl.BlockSpec((D_MODEL, n), lambda i, j: (0, 0))
    return pl.pallas_call(
        _inproj1_kernel,
        grid=(b, s // tm),
        in_specs=[pl.BlockSpec((1, tm, D_MODEL), lambda i, j: (i, j, 0)),
                  pl.BlockSpec((1, D_MODEL), lambda i, j: (0, 0)),
                  pl.BlockSpec((1, tm, 1), lambda i, j: (i, j, 0)),
                  pl.BlockSpec((1, LANES), lambda i, j: (0, 0))]
                 + [wspec(n) for n in widths],
        out_specs=[pl.BlockSpec((1, tm, n), lambda i, j: (i, j, 0)) for n in widths],
        out_shape=[jax.ShapeDtypeStruct((b, s, n), BF16) for n in widths],
        compiler_params=pltpu.CompilerParams(vmem_limit_bytes=VMEM_LIMIT),
        name="inproj1",
    )(x, g, pos3, invf, wq, wk, wv, wz, wmq)


def _unit_lower_inverses(lows, row, col, eye, between):
    c = lows[0].shape[0]
    shift = INV_BASE.bit_length() - 1
    base = (row >> shift) == (col >> shift)
    ns = [jnp.where(base, -low, 0.0) for low in lows]
    xs = [eye + n for n in ns]
    ns = [_dot(n, n) for n in ns]
    between()
    power = 2
    while 2 * power < INV_BASE:
        prods = [_dot(jnp.concatenate([n, x], axis=0), n) for x, n in zip(xs, ns)]
        between()
        xs = [x + p[c:] for x, p in zip(xs, prods)]
        ns = [p[:c] for p in prods]
        power *= 2
    xs = [x + _dot(x, n) for x, n in zip(xs, ns)]
    between()
    size = INV_BASE
    while size < c:
        sh = size.bit_length() - 1
        sel = (((row >> (sh + 1)) == (col >> (sh + 1)))
               & (((row >> sh) & 1) == 1) & (((col >> sh) & 1) == 0))
        odd = [slice(r, r + size) for r in range(size, c, 2 * size)]
        x_odd = [jnp.concatenate([x[sl] for sl in odd], axis=0) for x in xs]
        ys = [_dot(xo, jnp.where(sel, low, 0.0)) for xo, low in zip(x_odd, lows)]
        between()
        new_odd = [xo - _dot(y, x) for xo, y, x in zip(x_odd, ys, xs)]
        between()
        xs = [jnp.concatenate(
            [blk for k, sl in enumerate(odd)
             for blk in (x[sl.start - size:sl.start], no[k * size:(k + 1) * size])], axis=0)
            for x, no in zip(xs, new_odd)]
        size *= 2
    return xs


def _delta_kernel(q_ref, k_ref, v_ref, ba_ref, cwq_ref, cwk_ref, cwv_ref,
                  alog_ref, dt_ref, o_ref,
                  ext_ref, p_ref, qp_ref, n_ref, op_ref, egl_ref):
    c = DELTA_CHUNK
    grp = DELTA_GROUP
    rows = grp * c
    seq = q_ref.shape[1]
    pair = pl.program_id(1)
    masks = _head_masks()
    row = lax.broadcasted_iota(jnp.int32, (c, c), 0)
    col = lax.broadcasted_iota(jnp.int32, (c, c), 1)
    tri_incl = row >= col
    tri_strict = row > col
    eye = (row == col).astype(F32)
    tril_b = tri_incl.astype(BF16)
    head_block = ((row >> 6) == (col >> 6)).astype(F32)
    head_block_b = head_block.astype(BF16)
    row2 = lax.broadcasted_iota(jnp.int32, (LANES, 2 * PAIR), 0)
    col2 = lax.broadcasted_iota(jnp.int32, (LANES, 2 * PAIR), 1)
    expand = (row2 == jnp.where(col2 < PAIR, 0, MIX_HEADS) + 2 * pair + ((col2 >> 6) & 1)).astype(BF16)
    lane_n = lax.broadcasted_iota(jnp.int32, (1, LANES), 1)
    is_beta = lane_n < MIX_HEADS
    neg_a = -jnp.exp(alog_ref[...])
    dt_row = dt_ref[...]

    ext_ref[:, 0:SUBLANES, :] = jnp.zeros((3, SUBLANES, PAIR), F32)

    def conv_silu(idx, x_ref, cw_ref, t0):
        raw = x_ref[0, pl.ds(t0, rows), :].astype(F32)
        ext_ref[idx, SUBLANES:SUBLANES + rows, :] = raw
        acc = raw * cw_ref[CONV_WIDTH - 1:CONV_WIDTH, :]
        for tap in range(CONV_WIDTH - 1):
            back = CONV_WIDTH - 1 - tap
            acc = acc + ext_ref[idx, SUBLANES - back:SUBLANES - back + rows, :] * cw_ref[tap:tap + 1, :]
        ext_ref[idx, 0:SUBLANES, :] = raw[rows - SUBLANES:rows, :]
        return acc * _sigmoid(acc)

    def l2norm(x):
        return x * lax.rsqrt(_dot(x * x, head_block_b) + NORM_EPS)

    def build_group(gi, between):
        t0 = pl.multiple_of(gi * rows, rows)
        qn = l2norm(conv_silu(0, q_ref, cwq_ref, t0))
        kn = l2norm(conv_silu(1, k_ref, cwk_ref, t0))
        vc = conv_silu(2, v_ref, cwv_ref, t0)
        ba = ba_ref[0, pl.ds(t0, rows), :]
        act = jnp.where(is_beta, _sigmoid(ba), neg_a * _softplus(ba + dt_row))
        bg = jnp.dot(act.astype(BF16), expand, preferred_element_type=F32)
        beta = bg[:, :PAIR]
        g = bg[:, PAIR:]
        kb = kn * beta
        vb = vc * beta
        sls = [slice(ci * c, (ci + 1) * c) for ci in range(grp)]
        gcs = [jnp.dot(tril_b, g[sl].astype(BF16), preferred_element_type=F32) for sl in sls]
        between()
        gcts = [gc.T for gc in gcs]
        egs = [jnp.exp(gc) for gc in gcs]
        probs = [(ci, h) for ci in range(grp) for h in range(2)]
        decays = []
        for ci, h in probs:
            gcol = jnp.broadcast_to(gcs[ci][:, h * HEAD_DIM:h * HEAD_DIM + 1], (c, c))
            grow = gcts[ci][h * HEAD_DIM:h * HEAD_DIM + 1, :]
            decays.append(jnp.exp(jnp.where(tri_incl, gcol - grow, -jnp.inf)))
        kbhs = [kb[sls[ci]] * masks[h] for ci, h in probs]
        kq = [_dot_nt(jnp.concatenate(
            [kbhs[2 * ci], kbhs[2 * ci + 1]]
            + [qn[sls[ci]] * (masks[h] * QK_SCALE) for h in range(2)], axis=0), kn[sls[ci]])
            for ci in range(grp)]
        between()
        lows = [jnp.where(tri_strict, kq[ci][h * c:(h + 1) * c] * d, 0.0)
                for (ci, h), d in zip(probs, decays)]
        aqks = [jnp.where(tri_incl, kq[ci][(2 + h) * c:(3 + h) * c] * d, 0.0)
                for (ci, h), d in zip(probs, decays)]
        tinvs = _unit_lower_inverses(lows, row, col, eye, between)
        wus = [_dot(t, jnp.concatenate([kbh * egs[ci], vb[sls[ci]] * masks[h]], axis=1))
               for t, kbh, (ci, h) in zip(tinvs, kbhs, probs)]
        aus = [_dot(a, wu) for a, wu in zip(aqks, wus)]
        kds = [kn[sls[ci]] * jnp.exp(gcs[ci][c - 1:c, :] - gcs[ci]) for ci in range(grp)]
        pns = [_dot(kds[ci].T, wus[2 * ci] + wus[2 * ci + 1]) for ci in range(grp)]
        for ci in range(grp):
            cidx = gi * grp + ci
            au = aus[2 * ci] + aus[2 * ci + 1]
            p_ref[cidx] = (head_block * pns[ci][:, :PAIR]).astype(BF16)
            n_ref[cidx] = head_block * pns[ci][:, PAIR:]
            qp_ref[cidx] = (qn[sls[ci]] * (QK_SCALE * egs[ci]) - au[:, :PAIR]).astype(BF16)
            op_ref[cidx] = au[:, PAIR:]
            egl_ref[pl.ds(cidx, 1), :] = jnp.exp(gcs[ci][c - 1:c, :])

    def scan_step(ci, state):
        sb = state.astype(BF16)
        o = jnp.dot(qp_ref[ci], sb, preferred_element_type=F32) + op_ref[ci]
        new_state = (state * egl_ref[pl.ds(ci, 1), :]
                     - jnp.dot(p_ref[ci], sb, preferred_element_type=F32) + n_ref[ci])
        o_ref[0, pl.ds(pl.multiple_of(ci * c, c), c), :] = o.astype(o_ref.dtype)
        return new_state

    def build_and_scan(gi, state):
        box = [state]
        pending = iter(range(grp))

        def one_scan_step():
            ci = next(pending, None)
            if ci is not None:
                box[0] = scan_step((gi - 1) * grp + ci, box[0])

        build_group(gi, one_scan_step)
        for ci in pending:
            box[0] = scan_step((gi - 1) * grp + ci, box[0])
        return box[0]

    n_groups = seq // rows
    build_group(0, lambda: None)
    state = lax.fori_loop(1, n_groups, build_and_scan, jnp.zeros((PAIR, PAIR), F32))
    lax.fori_loop((n_groups - 1) * grp, n_groups * grp, scan_step, state)


def _delta_mixer(qkv, ba, conv_w, alog_row, dt_row):
    b, s, _ = qkv.shape
    n_chunks = s // DELTA_CHUNK
    seq_spec = lambda off: pl.BlockSpec((1, s, PAIR), lambda i, p, off=off: (i, 0, off + p))
    cw_spec = lambda off: pl.BlockSpec((CONV_WIDTH, PAIR), lambda i, p, off=off: (0, off + p))
    row_spec = pl.BlockSpec((1, LANES), lambda i, p: (0, 0))
    return pl.pallas_call(
        _delta_kernel,
        grid=(b, N_PAIRS),
        in_specs=[seq_spec(0), seq_spec(N_PAIRS), seq_spec(2 * N_PAIRS),
                  pl.BlockSpec((1, s, LANES), lambda i, p: (i, 0, 0)),
                  cw_spec(0), cw_spec(N_PAIRS), cw_spec(2 * N_PAIRS),
                  row_spec, row_spec],
        out_specs=pl.BlockSpec((1, s, PAIR), lambda i, p: (i, 0, p)),
        out_shape=jax.ShapeDtypeStruct((b, s, MIX_WIDTH), BF16),
        scratch_shapes=[pltpu.VMEM((3, SUBLANES + DELTA_GROUP * DELTA_CHUNK, PAIR), F32),
                        pltpu.VMEM((n_chunks, PAIR, PAIR), BF16),
                        pltpu.VMEM((n_chunks, DELTA_CHUNK, PAIR), BF16),
                        pltpu.VMEM((n_chunks, PAIR, PAIR), F32),
                        pltpu.VMEM((n_chunks, DELTA_CHUNK, PAIR), F32),
                        pltpu.VMEM((n_chunks, PAIR), F32)],
        compiler_params=pltpu.CompilerParams(vmem_limit_bytes=VMEM_LIMIT),
        name="delta_mixer",
    )(qkv, qkv, qkv, ba, conv_w, conv_w, conv_w, alog_row, dt_row)


def _moba_kernel(q_ref, k_ref, v_ref, o_ref, kaug_ref, vt_ref, kmean_ref, qaug_ref, sa_ref, sb_ref):
    blk = MOBA_BLOCK
    qw = 2 * blk
    seq = k_ref.shape[1]
    nb = seq // blk
    qi = pl.program_id(2)
    masks = _head_masks()
    n_pairs = MOBA_PAIRS
    heads = range(2 * n_pairs)

    @pl.when(qi == 0)
    def _prepare():
        lane = lax.broadcasted_iota(jnp.int32, (blk, LANES), 1)
        ones_row = (lax.broadcasted_iota(jnp.int32, (VT_ROWS - HEAD_DIM, LANES), 0) == 0).astype(BF16)

        def key_block(j, carry):
            r0 = pl.multiple_of(j * blk, blk)
            onehot = (lane == j).astype(BF16)
            for pr in range(n_pairs):
                kj = k_ref[0, pl.ds(r0, blk), pr * PAIR:(pr + 1) * PAIR]
                vj = v_ref[0, pl.ds(r0, blk), pr * PAIR:(pr + 1) * PAIR].astype(F32)
                kaug_ref[pr, pl.ds(r0, blk), 0:LANES] = kj
                kaug_ref[pr, pl.ds(r0, blk), LANES:2 * LANES] = onehot
                for t in range(blk // LANES):
                    c0 = pl.multiple_of(r0 + t * LANES, LANES)
                    vt = vj[t * LANES:(t + 1) * LANES, :].T.astype(BF16)
                    for h in range(2):
                        vt_ref[2 * pr + h, 0:HEAD_DIM, pl.ds(c0, LANES)] = vt[h * HEAD_DIM:(h + 1) * HEAD_DIM, :]
                        vt_ref[2 * pr + h, HEAD_DIM:VT_ROWS, pl.ds(c0, LANES)] = ones_row
                kmean_ref[pr, pl.ds(j, 1), :] = jnp.sum(kj.astype(F32), axis=0, keepdims=True) * (1.0 / blk)
            return carry
        kmean_ref[...] = jnp.zeros((n_pairs, NB_PAD, PAIR), F32)
        lax.fori_loop(0, nb, key_block, 0)

        kstacks = []
        for pr in range(n_pairs):
            pieces = []
            for h in range(2):
                rest = kmean_ref[pr] * masks[h]
                for _ in range(3):
                    part = rest.astype(BF16)
                    pieces.append(part)
                    rest = rest - part.astype(F32)
            kstacks.append(jnp.concatenate(pieces, axis=0))
        blk_id = lax.broadcasted_iota(jnp.int32, (NB_PAD, qw), 0)
        blk_f = blk_id.astype(F32)
        half = lax.broadcasted_iota(jnp.int32, (NB_PAD, qw), 1) >> (blk.bit_length() - 1)
        head_rows = lax.broadcasted_iota(jnp.int32, (PAIR, qw), 0) >> 6
        pad_rows = jnp.zeros((PAIR - NB_PAD, qw), BF16)

        def query_blocks(m, carry):
            c0 = pl.multiple_of(m * qw, qw)
            own_blk = 2 * m + half
            valid = blk_id < own_blk
            q_ts, gs, opens = [], [], []
            for pr in range(n_pairs):
                q = q_ref[0, pl.ds(c0, qw), pr * PAIR:(pr + 1) * PAIR]
                qf = q.astype(F32) * (QK_SCALE * LOG2E)
                q_ts.append(jnp.concatenate(
                    [qf[t * LANES:(t + 1) * LANES, :].T for t in range(qw // LANES)], axis=1))
                gate_parts = lax.dot_general(kstacks[pr], q, _NT, preferred_element_type=F32)
                for h in range(2):
                    g3 = [gate_parts[(3 * h + i) * NB_PAD:(3 * h + i + 1) * NB_PAD] for i in range(3)]
                    gs.append(jnp.where(valid, g3[0] + g3[1] + g3[2], -jnp.inf))
                    opens.append((blk_id == own_blk).astype(F32))
            for _ in range(MOBA_TOPK):
                for hh in heads:
                    top = jnp.max(gs[hh], axis=0, keepdims=True)
                    first = jnp.min(jnp.where(gs[hh] == top, blk_f, float(NB_PAD)), axis=0, keepdims=True)
                    pick = (blk_f == first) & (top > -jnp.inf)
                    opens[hh] = jnp.where(pick, 1.0, opens[hh])
                    gs[hh] = jnp.where(pick, -jnp.inf, gs[hh])
            for hh in heads:
                bias = jnp.where(opens[hh] > 0.5, 0.0, MASK_VALUE).astype(BF16)
                qaug_ref[hh, 0:PAIR, pl.ds(c0, qw)] = jnp.where(
                    head_rows == hh % 2, q_ts[hh // 2], 0.0).astype(BF16)
                qaug_ref[hh, PAIR:PAIR + NB_PAD, pl.ds(c0, qw)] = bias
                qaug_ref[hh, PAIR + NB_PAD:2 * PAIR, pl.ds(c0, qw)] = pad_rows
            return carry
        lax.fori_loop(0, nb // 2, query_blocks, 0)

    qs = [qaug_ref[hh, :, pl.ds(pl.multiple_of(qi * qw, qw), qw)] for hh in heads]

    key_pos = lax.broadcasted_iota(jnp.int32, (blk, blk), 0)
    qry_pos = lax.broadcasted_iota(jnp.int32, (blk, blk), 1)
    always = jnp.int32(1 << 20)

    def scores(t, s_ref, past_only=False):
        r0 = pl.multiple_of(t * (2 * blk), 2 * blk)
        k2 = [kaug_ref[pr, pl.ds(r0, 2 * blk), :] for pr in range(n_pairs)]
        sts = [jnp.dot(k2[hh // 2], qs[hh], preferred_element_type=F32) for hh in heads]
        maxima = []
        for hh, st in enumerate(sts):
            if not past_only:
                causal = key_pos <= qry_pos + jnp.where(t == qi, 0, always)
                st = jnp.concatenate(
                    [jnp.concatenate([jnp.where(causal, st[:blk, :blk], MASK_VALUE), st[:blk, blk:]], axis=1),
                     jnp.concatenate([st[blk:, :blk], jnp.where(causal, st[blk:, blk:], MASK_VALUE)], axis=1)],
                    axis=0)
            s_ref[hh] = st
            maxima.append(jnp.max(st, axis=0, keepdims=True))
        return maxima

    def update(t, state, maxima, s_ref):
        r0 = pl.multiple_of(t * (2 * blk), 2 * blk)
        m_new = [jnp.maximum(state[2 * hh], maxima[hh]) for hh in heads]
        ps = [jnp.exp2(s_ref[hh] - m_new[hh]).astype(BF16) for hh in heads]
        pvs = [jnp.dot(vt_ref[hh, :, pl.ds(r0, 2 * blk)], ps[hh], preferred_element_type=F32)
               for hh in heads]
        out = []
        for hh in heads:
            alpha = jnp.exp2(state[2 * hh] - m_new[hh])
            out += [m_new[hh], state[2 * hh + 1] * alpha + pvs[hh]]
        return out

    init = []
    for hh in heads:
        init += [jnp.full((1, qw), MASK_VALUE, F32), jnp.zeros((VT_ROWS, qw), F32)]
    n_state = len(init)

    def two_pairs(u, carry):
        t0 = 2 * u
        max_b = scores(t0 + 1, sb_ref, past_only=True)
        state = update(t0, carry[:n_state], carry[n_state:], sa_ref)
        max_a = scores(t0 + 2, sa_ref)
        state = update(t0 + 1, state, max_b, sb_ref)
        return tuple(state + max_a)

    carry = lax.fori_loop(0, qi // 2, two_pairs, tuple(init + scores(0, sa_ref)))

    def odd_tail(carry):
        max_b = scores(qi, sb_ref)
        state = update(qi - 1, carry[:n_state], carry[n_state:], sa_ref)
        return tuple(update(qi, state, max_b, sb_ref))

    def even_tail(carry):
        return tuple(update(qi, carry[:n_state], carry[n_state:], sa_ref))

    fin = lax.cond((qi & 1) == 1, odd_tail, even_tail, carry)

    for pr in range(n_pairs):
        out_t = jnp.concatenate(
            [fin[2 * hh + 1][:HEAD_DIM] / fin[2 * hh + 1][HEAD_DIM:HEAD_DIM + 1]
             for hh in (2 * pr, 2 * pr + 1)], axis=0)
        o_ref[0, :, pr * PAIR:(pr + 1) * PAIR] = out_t.T.astype(o_ref.dtype)


def _moba(q, k, v):
    b, s, _ = q.shape
    nb = s // MOBA_BLOCK
    width = MOBA_PAIRS * PAIR
    n_heads = 2 * MOBA_PAIRS
    seq_spec = pl.BlockSpec((1, s, width), lambda i, p, j: (i, 0, p))
    return pl.pallas_call(
        _moba_kernel,
        grid=(b, N_PAIRS // MOBA_PAIRS, nb // 2),
        in_specs=[seq_spec, seq_spec, seq_spec],
        out_specs=pl.BlockSpec((1, 2 * MOBA_BLOCK, width), lambda i, p, j: (i, j, p)),
        out_shape=jax.ShapeDtypeStruct((b, s, MIX_WIDTH), BF16),
        scratch_shapes=[pltpu.VMEM((MOBA_PAIRS, s, 2 * LANES), BF16),
                        pltpu.VMEM((n_heads, VT_ROWS, s), BF16),
                        pltpu.VMEM((MOBA_PAIRS, NB_PAD, PAIR), F32),
                        pltpu.VMEM((n_heads, 2 * LANES, s), BF16),
                        pltpu.VMEM((n_heads, 2 * MOBA_BLOCK, 2 * MOBA_BLOCK), F32),
                        pltpu.VMEM((n_heads, 2 * MOBA_BLOCK, 2 * MOBA_BLOCK), F32)],
        compiler_params=pltpu.CompilerParams(
            dimension_semantics=("arbitrary", "arbitrary", "arbitrary"),
            vmem_limit_bytes=VMEM_LIMIT),
        name="moba",
    )(q, k, v)


def _outproj_kernel(mix_ref, z_ref, mq_ref, mkv_ref, h_ref, wmix_ref, wmem_ref, hg_ref, fg_ref,
                    o_ref, *, head_norm, final_norm):
    mq = mq_ref[0].astype(F32)
    mk = mkv_ref[0, :, :MEM_WIDTH]
    mv = mkv_ref[0, :, MEM_WIDTH:]
    row = lax.broadcasted_iota(jnp.int32, (MEM_WIDTH, MEM_WIDTH), 0)
    col = lax.broadcasted_iota(jnp.int32, (MEM_WIDTH, MEM_WIDTH), 1)
    lane = lax.broadcasted_iota(jnp.int32, (1, MEM_WIDTH), 1)
    mhs = [((lane >> 6) == h).astype(F32) for h in range(MEM_HEADS)]
    logits = [_dot_nt(mq * (mh * QK_SCALE), mk) for mh in mhs]
    es = [jnp.exp(lg - jnp.max(lg, axis=-1, keepdims=True)) for lg in logits]
    inv = [1.0 / jnp.sum(e, axis=-1, keepdims=True) for e in es]
    pvs = [jnp.dot(e.astype(BF16), mv, preferred_element_type=F32) for e in es]
    memo = pvs[0] * (mhs[0] * inv[0])
    for h in range(1, MEM_HEADS):
        memo = memo + pvs[h] * (mhs[h] * inv[h])
    z = z_ref[0].astype(F32)
    gate = z * _sigmoid(z)
    mix = mix_ref[0].astype(F32)
    if head_norm:
        same_head = ((row >> 6) == (col >> 6)).astype(BF16)
        slabs = []
        for sl in range(MIX_WIDTH // MEM_WIDTH):
            ms = mix[:, sl * MEM_WIDTH:(sl + 1) * MEM_WIDTH]
            ss = _dot(ms * ms, same_head) * (1.0 / HEAD_DIM)
            slabs.append(ms * lax.rsqrt(ss + NORM_EPS))
        mix = jnp.concatenate(slabs, axis=1) * hg_ref[...]
    y_mix = mix * gate[:, :MIX_WIDTH]
    y_mem = memo * gate[:, MIX_WIDTH:]
    out = h_ref[0] + _dot(y_mix, wmix_ref[...]) + _dot(y_mem, wmem_ref[...])
    if final_norm:
        out = out * _rms_scale(out) * fg_ref[...]
    o_ref[0] = out


def _outproj(mix, z, mq, mkv, h, wmix, wmem, hg, fg, tm, head_norm, final_norm):
    b, s, _ = h.shape
    tile = lambda n: pl.BlockSpec((1, tm, n), lambda i, j: (i, j, 0))
    return pl.pallas_call(
        functools.partial(_outproj_kernel, head_norm=head_norm, final_norm=final_norm),
        grid=(b, s // tm),
        in_specs=[tile(MIX_WIDTH), tile(GATE_WIDTH), tile(MEM_WIDTH),
                  pl.BlockSpec((1, N_MEM, 2 * MEM_WIDTH), lambda i, j: (i, 0, 0)),
                  tile(D_MODEL),
                  pl.BlockSpec((MIX_WIDTH, D_MODEL), lambda i, j: (0, 0)),
                  pl.BlockSpec((MEM_WIDTH, D_MODEL), lambda i, j: (0, 0)),
                  pl.BlockSpec((1, MIX_WIDTH), lambda i, j: (0, 0)),
                  pl.BlockSpec((1, D_MODEL), lambda i, j: (0, 0))],
        out_specs=tile(D_MODEL),
        out_shape=jax.ShapeDtypeStruct((b, s, D_MODEL), F32),
        compiler_params=pltpu.CompilerParams(vmem_limit_bytes=VMEM_LIMIT),
        name="outproj_final" if final_norm else "outproj",
    )(mix, z, mq, mkv, h, wmix, wmem, hg, fg)


def _row(v):
    return v.reshape(1, -1).astype(F32)


def kernel(x, mem, positions, norm_0, w_in_0, conv_w_0, a_log_0, dt_bias_0, o_norm_0,
           mem_norm_0, w_mem_kv_0, w_out_0, norm_1, w_in_1, mem_norm_1, w_mem_kv_1,
           w_out_1, final_norm):
    b, s, _ = x.shape
    assert s % (2 * MOBA_BLOCK) == 0 and MOBA_TOPK <= s // MOBA_BLOCK <= NB_PAD
    assert s % (DELTA_GROUP * DELTA_CHUNK) == 0
    tm = 512

    mkv0, mkv1 = _memkv(mem, _row(mem_norm_0), w_mem_kv_0.astype(BF16),
                        _row(mem_norm_1), w_mem_kv_1.astype(BF16))

    i1 = 3 * MIX_WIDTH
    i2 = i1 + GATE_WIDTH
    i3 = i2 + MEM_WIDTH
    w0 = w_in_0.astype(BF16)
    wba = jnp.pad(w0[:, i3:], ((0, 0), (0, LANES - 2 * MIX_HEADS)))
    qkv, z0, mq0, ba = _inproj0(x, _row(norm_0), w0[:, :i1], w0[:, i1:i2], w0[:, i2:i3], wba, tm)
    ba_cols = lambda v: _row(jnp.pad(v, (MIX_HEADS, LANES - 2 * MIX_HEADS)))
    o0 = _delta_mixer(qkv, ba, conv_w_0.astype(F32), ba_cols(a_log_0), ba_cols(dt_bias_0))
    wo0 = w_out_0.astype(BF16)
    h1 = _outproj(o0, z0, mq0, mkv0, x, wo0[:MIX_WIDTH], wo0[MIX_WIDTH:],
                  _row(jnp.tile(o_norm_0, MIX_HEADS)), _row(final_norm),
                  tm, head_norm=True, final_norm=False)

    w1 = w_in_1.astype(BF16)
    half = HEAD_DIM // 2
    inv_freq = ROPE_THETA ** (-jnp.arange(half, dtype=F32) * (2.0 / HEAD_DIM))
    invf = _row(jnp.tile(inv_freq, LANES // half))
    q1, k1, v1, z1, mq1 = _inproj1(
        h1, _row(norm_1), positions.reshape(b, s, 1), invf,
        w1[:, :MIX_WIDTH], w1[:, MIX_WIDTH:2 * MIX_WIDTH], w1[:, 2 * MIX_WIDTH:i1],
        w1[:, i1:i2], w1[:, i2:], tm)
    o1 = _moba(q1, k1, v1)
    wo1 = w_out_1.astype(BF16)
    return _outproj(o1, z1, mq1, mkv1, h1, wo1[:MIX_WIDTH], wo1[MIX_WIDTH:],
                    jnp.ones((1, MIX_WIDTH), F32), _row(final_norm),
                    tm, head_norm=False, final_norm=True)
```

```python
import functools

import jax
import jax.numpy as jnp
from jax import lax
from jax.experimental import pallas as pl
from jax.experimental.pallas import tpu as pltpu

D_MODEL = 1024
HEAD_DIM = 64
MIX_HEADS = 12
MEM_HEADS = 4
MIX_WIDTH = MIX_HEADS * HEAD_DIM
MEM_WIDTH = MEM_HEADS * HEAD_DIM
GATE_WIDTH = MIX_WIDTH + MEM_WIDTH
N_MEM = 256
CONV_WIDTH = 4
MOBA_BLOCK = 256
MOBA_TOPK = 3
ROPE_THETA = 10000.0
NORM_EPS = 1e-6
MASK_VALUE = -1e30

LANES = 128
SUBLANES = 8
PAIR = 2 * HEAD_DIM
N_PAIRS = MIX_HEADS // 2
DELTA_CHUNK = 128
DELTA_GROUP = 8
INV_BASE = 16
QK_SCALE = HEAD_DIM ** -0.5
LOG2E = 1.4426950408889634
MOBA_PAIRS = 2
NB_PAD = 16
VT_ROWS = HEAD_DIM + 16
VMEM_LIMIT = 56 * 1024 * 1024

BF16 = jnp.bfloat16
F32 = jnp.float32

_NT = (((1,), (1,)), ((), ()))


def _dot(a, b):
    return jnp.dot(a.astype(BF16), b.astype(BF16), preferred_element_type=F32)


def _dot_nt(a, b):
    return lax.dot_general(a.astype(BF16), b.astype(BF16), _NT, preferred_element_type=F32)


def _dot_hi(a, b):
    return jnp.dot(a, b, precision=lax.Precision.HIGHEST, preferred_element_type=F32)


def _dot_nt_hi(a, b):
    return lax.dot_general(a, b, _NT, precision=lax.Precision.HIGHEST,
                           preferred_element_type=F32)


def _rms_scale(x):
    return lax.rsqrt(jnp.mean(x * x, axis=-1, keepdims=True) + NORM_EPS)


def _sigmoid(x):
    return 1.0 / (1.0 + jnp.exp(-x))


def _silu(x):
    half = 0.5 * x
    return half + half * jnp.tanh(half)


def _softplus(x):
    return jnp.maximum(x, 0.0) + jnp.log(1.0 + jnp.exp(-jnp.abs(x)))


def _head_masks():
    lane = lax.broadcasted_iota(jnp.int32, (1, PAIR), 1)
    return [((lane >> 6) == h).astype(F32) for h in range(2)]


def _memkv_kernel(mem_ref, g0_ref, w0_ref, g1_ref, w1_ref, o0_ref, o1_ref):
    m = mem_ref[0]
    mn = m * _rms_scale(m)
    o0_ref[0] = _dot(mn * g0_ref[...], w0_ref[...]).astype(o0_ref.dtype)
    o1_ref[0] = _dot(mn * g1_ref[...], w1_ref[...]).astype(o1_ref.dtype)


def _memkv(mem, g0, w0, g1, w1):
    b = mem.shape[0]
    full = lambda shp: pl.BlockSpec(shp, lambda i: (0,) * len(shp))
    return pl.pallas_call(
        _memkv_kernel,
        grid=(b,),
        in_specs=[pl.BlockSpec((1, N_MEM, D_MODEL), lambda i: (i, 0, 0)),
                  full((1, D_MODEL)), full((D_MODEL, 2 * MEM_WIDTH)),
                  full((1, D_MODEL)), full((D_MODEL, 2 * MEM_WIDTH))],
        out_specs=[pl.BlockSpec((1, N_MEM, 2 * MEM_WIDTH), lambda i: (i, 0, 0))] * 2,
        out_shape=[jax.ShapeDtypeStruct((b, N_MEM, 2 * MEM_WIDTH), BF16)] * 2,
        compiler_params=pltpu.CompilerParams(vmem_limit_bytes=VMEM_LIMIT),
        name="memkv",
    )(mem, g0, w0, g1, w1)


def _inproj0_kernel(x_ref, g_ref, wqkv_ref, wz_ref, wmq_ref, wba_ref,
                    qkv_ref, z_ref, mq_ref, ba_ref):
    x = x_ref[0]
    xn = (x * _rms_scale(x) * g_ref[...]).astype(BF16)
    for w_ref, o_ref in ((wqkv_ref, qkv_ref), (wz_ref, z_ref),
                         (wmq_ref, mq_ref), (wba_ref, ba_ref)):
        o_ref[0] = jnp.dot(xn, w_ref[...], preferred_element_type=F32).astype(o_ref.dtype)


def _inproj0(x, g, wqkv, wz, wmq, wba, tm):
    b, s, _ = x.shape
    widths = (3 * MIX_WIDTH, GATE_WIDTH, MEM_WIDTH, LANES)
    wspec = lambda n: pl.BlockSpec((D_MODEL, n), lambda i, j: (0, 0))
    return pl.pallas_call(
        _inproj0_kernel,
        grid=(b, s // tm),
        in_specs=[pl.BlockSpec((1, tm, D_MODEL), lambda i, j: (i, j, 0)),
                  pl.BlockSpec((1, D_MODEL), lambda i, j: (0, 0))]
                 + [wspec(n) for n in widths],
        out_specs=[pl.BlockSpec((1, tm, n), lambda i, j: (i, j, 0)) for n in widths],
        out_shape=[jax.ShapeDtypeStruct((b, s, n), dt)
                   for n, dt in zip(widths, (BF16, BF16, BF16, F32))],
        compiler_params=pltpu.CompilerParams(vmem_limit_bytes=VMEM_LIMIT),
        name="inproj0",
    )(x, g, wqkv, wz, wmq, wba)


def _inproj1_kernel(x_ref, g_ref, pos_ref, invf_ref, wq_ref, wk_ref, wv_ref, wz_ref, wmq_ref,
                    q_ref, k_ref, v_ref, z_ref, mq_ref):
    x = x_ref[0]
    xn = (x * _rms_scale(x) * g_ref[...]).astype(BF16)
    n_freq = HEAD_DIM // 2
    per_pack = LANES // n_freq
    tm = x.shape[0]
    pos = pos_ref[0].astype(F32)
    lane_p = lax.broadcasted_iota(jnp.int32, (LANES, LANES), 1)
    block = lane_p // n_freq
    cos_parts, sin_parts = [], []
    for base in range(0, tm, per_pack * LANES):
        pos_p = jnp.zeros((LANES, LANES), F32)
        for gidx in range(per_pack):
            rows = slice(base + gidx * LANES, base + (gidx + 1) * LANES)
            pos_p = jnp.where(block == gidx, pos[rows], pos_p)
        ang_p = pos_p * invf_ref[...]
        for trig, parts in ((jnp.cos(ang_p), cos_parts), (jnp.sin(ang_p), sin_parts)):
            for gidx in range(per_pack):
                one = jnp.where(block == gidx, trig, 0.0)
                spread = one
                for k in range(1, per_pack):
                    spread = spread + pltpu.roll(one, k * n_freq, 1)
                parts.append(spread)
    cos = jnp.concatenate(cos_parts, axis=0)
    sin = jnp.concatenate(sin_parts, axis=0)
    lane = lax.broadcasted_iota(jnp.int32, (tm, LANES), 1)
    first_half = (lane & (HEAD_DIM - 1)) < (HEAD_DIM // 2)
    sin_signed = jnp.where(first_half, -sin, sin)
    for w_ref, o_ref in ((wq_ref, q_ref), (wk_ref, k_ref)):
        y = jnp.dot(xn, w_ref[...], preferred_element_type=F32)
        for sl in range(MIX_WIDTH // LANES):
            ys = y[:, sl * LANES:(sl + 1) * LANES]
            rot = jnp.where(first_half,
                            pltpu.roll(ys, LANES - HEAD_DIM // 2, 1),
                            pltpu.roll(ys, HEAD_DIM // 2, 1))
            o_ref[0, :, sl * LANES:(sl + 1) * LANES] = (ys * cos + rot * sin_signed).astype(o_ref.dtype)
    for w_ref, o_ref in ((wv_ref, v_ref), (wz_ref, z_ref), (wmq_ref, mq_ref)):
        o_ref[0] = jnp.dot(xn, w_ref[...], preferred_element_type=F32).astype(o_ref.dtype)


def _inproj1(x, g, pos3, invf, wq, wk, wv, wz, wmq, tm):
    b, s, _ = x.shape
    widths = (MIX_WIDTH, MIX_WIDTH, MIX_WIDTH, GATE_WIDTH, MEM_WIDTH)
    wspec = lambda n: pl.BlockSpec((D_MODEL, n), lambda i, j: (0, 0))
    return pl.pallas_call(
        _inproj1_kernel,
        grid=(b, s // tm),
        in_specs=[pl.BlockSpec((1, tm, D_MODEL), lambda i, j: (i, j, 0)),
                  pl.BlockSpec((1, D_MODEL), lambda i, j: (0, 0)),
                  pl.BlockSpec((1, tm, 1), lambda i, j: (i, j, 0)),
                  pl.BlockSpec((1, LANES), lambda i, j: (0, 0))]
                 + [wspec(n) for n in widths],
        out_specs=[pl.BlockSpec((1, tm, n), lambda i, j: (i, j, 0)) for n in widths],
        out_shape=[jax.ShapeDtypeStruct((b, s, n), BF16) for n in widths],
        compiler_params=pltpu.CompilerParams(vmem_limit_bytes=VMEM_LIMIT),
        name="inproj1",
    )(x, g, pos3, invf, wq, wk, wv, wz, wmq)


def _unit_lower_inverses(lows, row, col, eye, between):
    c = lows[0].shape[0]
    shift = INV_BASE.bit_length() - 1
    base = (row >> shift) == (col >> shift)
    ns = [jnp.where(base, -low, 0.0) for low in lows]
    xs = [eye + n for n in ns]
    ns = [_dot(n, n) for n in ns]
    between()
    power = 2
    while 2 * power < INV_BASE:
        prods = [_dot(jnp.concatenate([n, x], axis=0), n) for x, n in zip(xs, ns)]
        between()
        xs = [x + p[c:] for x, p in zip(xs, prods)]
        ns = [p[:c] for p in prods]
        power *= 2
    xs = [x + _dot(x, n) for x, n in zip(xs, ns)]
    between()
    size = INV_BASE
    while size < c:
        sh = size.bit_length() - 1
        sel = (((row >> (sh + 1)) == (col >> (sh + 1)))
               & (((row >> sh) & 1) == 1) & (((col >> sh) & 1) == 0))
        odd = [slice(r, r + size) for r in range(size, c, 2 * size)]
        x_odd = [jnp.concatenate([x[sl] for sl in odd], axis=0) for x in xs]
        ys = [_dot(xo, jnp.where(sel, low, 0.0)) for xo, low in zip(x_odd, lows)]
        between()
        new_odd = [xo - _dot(y, x) for xo, y, x in zip(x_odd, ys, xs)]
        between()
        xs = [jnp.concatenate(
            [blk for k, sl in enumerate(odd)
             for blk in (x[sl.start - size:sl.start], no[k * size:(k + 1) * size])], axis=0)
            for x, no in zip(xs, new_odd)]
        size *= 2
    return xs


def _delta_kernel(q_ref, k_ref, v_ref, ba_ref, cwq_ref, cwk_ref, cwv_ref,
                  alog_ref, dt_ref, o_ref,
                  ext_ref, p_ref, qp_ref, n_ref, op_ref, egl_ref):
    c = DELTA_CHUNK
    grp = DELTA_GROUP
    rows = grp * c
    seq = q_ref.shape[1]
    pair = pl.program_id(1)
    masks = _head_masks()
    row = lax.broadcasted_iota(jnp.int32, (c, c), 0)
    col = lax.broadcasted_iota(jnp.int32, (c, c), 1)
    tri_incl = row >= col
    tri_strict = row > col
    eye = (row == col).astype(F32)
    tril_b = tri_incl.astype(BF16)
    head_block = ((row >> 6) == (col >> 6)).astype(F32)
    head_block_b = head_block.astype(BF16)
    row2 = lax.broadcasted_iota(jnp.int32, (LANES, 2 * PAIR), 0)
    col2 = lax.broadcasted_iota(jnp.int32, (LANES, 2 * PAIR), 1)
    expand = (row2 == jnp.where(col2 < PAIR, 0, MIX_HEADS) + 2 * pair + ((col2 >> 6) & 1)).astype(BF16)
    lane_n = lax.broadcasted_iota(jnp.int32, (1, LANES), 1)
    is_beta = lane_n < MIX_HEADS
    neg_a = -jnp.exp(alog_ref[...])
    dt_row = dt_ref[...]

    ext_ref[:, 0:SUBLANES, :] = jnp.zeros((3, SUBLANES, PAIR), F32)

    def conv_silu(idx, x_ref, cw_ref, t0):
        raw = x_ref[0, pl.ds(t0, rows), :].astype(F32)
        ext_ref[idx, SUBLANES:SUBLANES + rows, :] = raw
        acc = raw * cw_ref[CONV_WIDTH - 1:CONV_WIDTH, :]
        for tap in range(CONV_WIDTH - 1):
            back = CONV_WIDTH - 1 - tap
            acc = acc + ext_ref[idx, SUBLANES - back:SUBLANES - back + rows, :] * cw_ref[tap:tap + 1, :]
        ext_ref[idx, 0:SUBLANES, :] = raw[rows - SUBLANES:rows, :]
        return _silu(acc)

    def l2norm(x):
        return x * lax.rsqrt(_dot(x * x, head_block_b) + NORM_EPS)

    def build_group(gi, between):
        t0 = pl.multiple_of(gi * rows, rows)
        qn = l2norm(conv_silu(0, q_ref, cwq_ref, t0))
        kn = l2norm(conv_silu(1, k_ref, cwk_ref, t0))
        vc = conv_silu(2, v_ref, cwv_ref, t0)
        ba = ba_ref[0, pl.ds(t0, rows), :]
        act = jnp.where(is_beta, _sigmoid(ba), neg_a * _softplus(ba + dt_row))
        bg = jnp.dot(act.astype(BF16), expand, preferred_element_type=F32)
        beta = bg[:, :PAIR]
        g = bg[:, PAIR:]
        kb = kn * beta
        vb = vc * beta
        sls = [slice(ci * c, (ci + 1) * c) for ci in range(grp)]
        gcs = [jnp.dot(tril_b, g[sl].astype(BF16), preferred_element_type=F32) * LOG2E for sl in sls]
        between()
        gcts = [gc.T for gc in gcs]
        egs = [jnp.exp2(gc) for gc in gcs]
        probs = [(ci, h) for ci in range(grp) for h in range(2)]
        decays = []
        for ci, h in probs:
            gcol = jnp.broadcast_to(gcs[ci][:, h * HEAD_DIM:h * HEAD_DIM + 1], (c, c))
            grow = gcts[ci][h * HEAD_DIM:h * HEAD_DIM + 1, :]
            decays.append(jnp.exp2(jnp.where(tri_incl, gcol - grow, -jnp.inf)))
        kbhs = [kb[sls[ci]] * masks[h] for ci, h in probs]
        kq = [_dot_nt(jnp.concatenate(
            [kbhs[2 * ci], kbhs[2 * ci + 1]]
            + [qn[sls[ci]] * (masks[h] * QK_SCALE) for h in range(2)], axis=0), kn[sls[ci]])
            for ci in range(grp)]
        between()
        lows = [jnp.where(tri_strict, kq[ci][h * c:(h + 1) * c] * d, 0.0)
                for (ci, h), d in zip(probs, decays)]
        aqks = [jnp.where(tri_incl, kq[ci][(2 + h) * c:(3 + h) * c] * d, 0.0)
                for (ci, h), d in zip(probs, decays)]
        tinvs = _unit_lower_inverses(lows, row, col, eye, between)
        wus = [_dot(t, jnp.concatenate([kbh * egs[ci], vb[sls[ci]] * masks[h]], axis=1))
               for t, kbh, (ci, h) in zip(tinvs, kbhs, probs)]
        aus = [_dot(a, wu) for a, wu in zip(aqks, wus)]
        kds = [kn[sls[ci]] * jnp.exp2(gcs[ci][c - 1:c, :] - gcs[ci]) for ci in range(grp)]
        pns = [_dot(kds[ci].T, wus[2 * ci] + wus[2 * ci + 1]) for ci in range(grp)]
        for ci in range(grp):
            cidx = gi * grp + ci
            au = aus[2 * ci] + aus[2 * ci + 1]
            p_ref[cidx] = (head_block * pns[ci][:, :PAIR]).astype(BF16)
            n_ref[cidx] = head_block * pns[ci][:, PAIR:]
            qp_ref[cidx] = (qn[sls[ci]] * (QK_SCALE * egs[ci]) - au[:, :PAIR]).astype(BF16)
            op_ref[cidx] = au[:, PAIR:]
            egl_ref[pl.ds(cidx, 1), :] = jnp.exp2(gcs[ci][c - 1:c, :])

    def scan_step(ci, state):
        sb = state.astype(BF16)
        o = jnp.dot(qp_ref[ci], sb, preferred_element_type=F32) + op_ref[ci]
        new_state = (state * egl_ref[pl.ds(ci, 1), :]
                     - jnp.dot(p_ref[ci], sb, preferred_element_type=F32) + n_ref[ci])
        o_ref[0, pl.ds(pl.multiple_of(ci * c, c), c), :] = o.astype(o_ref.dtype)
        return new_state

    def build_and_scan(gi, state):
        box = [state]
        pending = iter(range(grp))

        def one_scan_step():
            ci = next(pending, None)
            if ci is not None:
                box[0] = scan_step((gi - 1) * grp + ci, box[0])

        build_group(gi, one_scan_step)
        for ci in pending:
            box[0] = scan_step((gi - 1) * grp + ci, box[0])
        return box[0]

    n_groups = seq // rows
    build_group(0, lambda: None)
    state = lax.fori_loop(1, n_groups, build_and_scan, jnp.zeros((PAIR, PAIR), F32))
    lax.fori_loop((n_groups - 1) * grp, n_groups * grp, scan_step, state)


def _delta_mixer(qkv, ba, conv_w, alog_row, dt_row):
    b, s, _ = qkv.shape
    n_chunks = s // DELTA_CHUNK
    seq_spec = lambda off: pl.BlockSpec((1, s, PAIR), lambda i, p, off=off: (i, 0, off + p))
    cw_spec = lambda off: pl.BlockSpec((CONV_WIDTH, PAIR), lambda i, p, off=off: (0, off + p))
    row_spec = pl.BlockSpec((1, LANES), lambda i, p: (0, 0))
    return pl.pallas_call(
        _delta_kernel,
        grid=(b, N_PAIRS),
        in_specs=[seq_spec(0), seq_spec(N_PAIRS), seq_spec(2 * N_PAIRS),
                  pl.BlockSpec((1, s, LANES), lambda i, p: (i, 0, 0)),
                  cw_spec(0), cw_spec(N_PAIRS), cw_spec(2 * N_PAIRS),
                  row_spec, row_spec],
        out_specs=pl.BlockSpec((1, s, PAIR), lambda i, p: (i, 0, p)),
        out_shape=jax.ShapeDtypeStruct((b, s, MIX_WIDTH), BF16),
        scratch_shapes=[pltpu.VMEM((3, SUBLANES + DELTA_GROUP * DELTA_CHUNK, PAIR), F32),
                        pltpu.VMEM((n_chunks, PAIR, PAIR), BF16),
                        pltpu.VMEM((n_chunks, DELTA_CHUNK, PAIR), BF16),
                        pltpu.VMEM((n_chunks, PAIR, PAIR), F32),
                        pltpu.VMEM((n_chunks, DELTA_CHUNK, PAIR), F32),
                        pltpu.VMEM((n_chunks, PAIR), F32)],
        compiler_params=pltpu.CompilerParams(vmem_limit_bytes=VMEM_LIMIT),
        name="delta_mixer",
    )(qkv, qkv, qkv, ba, conv_w, conv_w, conv_w, alog_row, dt_row)


def _moba_kernel(q_ref, k_ref, v_ref, o_ref, kaug_ref, vt_ref, kmean_ref, qaug_ref, sa_ref, sb_ref):
    blk = MOBA_BLOCK
    qw = 2 * blk
    seq = k_ref.shape[1]
    nb = seq // blk
    qi = pl.program_id(2)
    masks = _head_masks()
    n_pairs = MOBA_PAIRS
    heads = range(2 * n_pairs)

    @pl.when(qi == 0)
    def _prepare():
        lane = lax.broadcasted_iota(jnp.int32, (blk, LANES), 1)
        ones_row = (lax.broadcasted_iota(jnp.int32, (VT_ROWS - HEAD_DIM, LANES), 0) == 0).astype(BF16)

        def key_block(j, carry):
            r0 = pl.multiple_of(j * blk, blk)
            onehot = (lane == j).astype(BF16)
            for pr in range(n_pairs):
                kj = k_ref[0, pl.ds(r0, blk), pr * PAIR:(pr + 1) * PAIR]
                vj = v_ref[0, pl.ds(r0, blk), pr * PAIR:(pr + 1) * PAIR].astype(F32)
                kaug_ref[pr, pl.ds(r0, blk), 0:LANES] = kj
                kaug_ref[pr, pl.ds(r0, blk), LANES:2 * LANES] = onehot
                for t in range(blk // LANES):
                    c0 = pl.multiple_of(r0 + t * LANES, LANES)
                    vt = vj[t * LANES:(t + 1) * LANES, :].T.astype(BF16)
                    for h in range(2):
                        vt_ref[2 * pr + h, 0:HEAD_DIM, pl.ds(c0, LANES)] = vt[h * HEAD_DIM:(h + 1) * HEAD_DIM, :]
                        vt_ref[2 * pr + h, HEAD_DIM:VT_ROWS, pl.ds(c0, LANES)] = ones_row
                kmean_ref[pr, pl.ds(j, 1), :] = jnp.sum(kj.astype(F32), axis=0, keepdims=True) * (1.0 / blk)
            return carry
        kmean_ref[...] = jnp.zeros((n_pairs, NB_PAD, PAIR), F32)
        lax.fori_loop(0, nb, key_block, 0)

        kstacks = []
        for pr in range(n_pairs):
            pieces = []
            for h in range(2):
                rest = kmean_ref[pr] * masks[h]
                for _ in range(3):
                    part = rest.astype(BF16)
                    pieces.append(part)
                    rest = rest - part.astype(F32)
            kstacks.append(jnp.concatenate(pieces, axis=0))
        blk_id = lax.broadcasted_iota(jnp.int32, (NB_PAD, qw), 0)
        blk_f = blk_id.astype(F32)
        half = lax.broadcasted_iota(jnp.int32, (NB_PAD, qw), 1) >> (blk.bit_length() - 1)
        head_rows = lax.broadcasted_iota(jnp.int32, (PAIR, qw), 0) >> 6
        pad_rows = jnp.zeros((PAIR - NB_PAD, qw), BF16)

        def query_blocks(m, carry):
            c0 = pl.multiple_of(m * qw, qw)
            own_blk = 2 * m + half
            valid = blk_id < own_blk
            q_ts, gs, opens = [], [], []
            for pr in range(n_pairs):
                q = q_ref[0, pl.ds(c0, qw), pr * PAIR:(pr + 1) * PAIR]
                qf = q.astype(F32) * (QK_SCALE * LOG2E)
                q_ts.append(jnp.concatenate(
                    [qf[t * LANES:(t + 1) * LANES, :].T for t in range(qw // LANES)], axis=1))
                gate_parts = lax.dot_general(kstacks[pr], q, _NT, preferred_element_type=F32)
                for h in range(2):
                    g3 = [gate_parts[(3 * h + i) * NB_PAD:(3 * h + i + 1) * NB_PAD] for i in range(3)]
                    gs.append(jnp.where(valid, g3[0] + g3[1] + g3[2], -jnp.inf))
                    opens.append((blk_id == own_blk).astype(F32))
            for _ in range(MOBA_TOPK):
                for hh in heads:
                    top = jnp.max(gs[hh], axis=0, keepdims=True)
                    first = jnp.min(jnp.where(gs[hh] == top, blk_f, float(NB_PAD)), axis=0, keepdims=True)
                    pick = (blk_f == first) & (top > -jnp.inf)
                    opens[hh] = jnp.where(pick, 1.0, opens[hh])
                    gs[hh] = jnp.where(pick, -jnp.inf, gs[hh])
            for hh in heads:
                bias = jnp.where(opens[hh] > 0.5, 0.0, MASK_VALUE).astype(BF16)
                qaug_ref[hh, 0:PAIR, pl.ds(c0, qw)] = jnp.where(
                    head_rows == hh % 2, q_ts[hh // 2], 0.0).astype(BF16)
                qaug_ref[hh, PAIR:PAIR + NB_PAD, pl.ds(c0, qw)] = bias
                qaug_ref[hh, PAIR + NB_PAD:2 * PAIR, pl.ds(c0, qw)] = pad_rows
            return carry
        lax.fori_loop(0, nb // 2, query_blocks, 0)

    qs = [qaug_ref[hh, :, pl.ds(pl.multiple_of(qi * qw, qw), qw)] for hh in heads]

    key_pos = lax.broadcasted_iota(jnp.int32, (blk, blk), 0)
    qry_pos = lax.broadcasted_iota(jnp.int32, (blk, blk), 1)
    always = jnp.int32(1 << 20)

    def scores(t, s_ref, past_only=False):
        r0 = pl.multiple_of(t * (2 * blk), 2 * blk)
        k2 = [kaug_ref[pr, pl.ds(r0, 2 * blk), :] for pr in range(n_pairs)]
        sts = [jnp.dot(k2[hh // 2], qs[hh], preferred_element_type=F32) for hh in heads]
        maxima = []
        for hh, st in enumerate(sts):
            if not past_only:
                causal = key_pos <= qry_pos + jnp.where(t == qi, 0, always)
                st = jnp.concatenate(
                    [jnp.concatenate([jnp.where(causal, st[:blk, :blk], MASK_VALUE), st[:blk, blk:]], axis=1),
                     jnp.concatenate([st[blk:, :blk], jnp.where(causal, st[blk:, blk:], MASK_VALUE)], axis=1)],
                    axis=0)
            s_ref[hh] = st
            maxima.append(jnp.max(st, axis=0, keepdims=True))
        return maxima

    def update(t, state, maxima, s_ref):
        r0 = pl.multiple_of(t * (2 * blk), 2 * blk)
        m_new = [jnp.maximum(state[2 * hh], maxima[hh]) for hh in heads]
        ps = [jnp.exp2(s_ref[hh] - m_new[hh]).astype(BF16) for hh in heads]
        pvs = [jnp.dot(vt_ref[hh, :, pl.ds(r0, 2 * blk)], ps[hh], preferred_element_type=F32)
               for hh in heads]
        out = []
        for hh in heads:
            alpha = jnp.exp2(state[2 * hh] - m_new[hh])
            out += [m_new[hh], state[2 * hh + 1] * alpha + pvs[hh]]
        return out

    init = []
    for hh in heads:
        init += [jnp.full((1, qw), MASK_VALUE, F32), jnp.zeros((VT_ROWS, qw), F32)]
    n_state = len(init)

    def two_pairs(u, carry):
        t0 = 2 * u
        max_b = scores(t0 + 1, sb_ref, past_only=True)
        state = update(t0, carry[:n_state], carry[n_state:], sa_ref)
        max_a = scores(t0 + 2, sa_ref)
        state = update(t0 + 1, state, max_b, sb_ref)
        return tuple(state + max_a)

    carry = lax.fori_loop(0, qi // 2, two_pairs, tuple(init + scores(0, sa_ref)))

    def odd_tail(carry):
        max_b = scores(qi, sb_ref)
        state = update(qi - 1, carry[:n_state], carry[n_state:], sa_ref)
        return tuple(update(qi, state, max_b, sb_ref))

    def even_tail(carry):
        return tuple(update(qi, carry[:n_state], carry[n_state:], sa_ref))

    fin = lax.cond((qi & 1) == 1, odd_tail, even_tail, carry)

    for pr in range(n_pairs):
        out_t = jnp.concatenate(
            [fin[2 * hh + 1][:HEAD_DIM] / fin[2 * hh + 1][HEAD_DIM:HEAD_DIM + 1]
             for hh in (2 * pr, 2 * pr + 1)], axis=0)
        o_ref[0, :, pr * PAIR:(pr + 1) * PAIR] = out_t.T.astype(o_ref.dtype)


def _moba(q, k, v):
    b, s, _ = q.shape
    nb = s // MOBA_BLOCK
    width = MOBA_PAIRS * PAIR
    n_heads = 2 * MOBA_PAIRS
    seq_spec = pl.BlockSpec((1, s, width), lambda i, p, j: (i, 0, p))
    return pl.pallas_call(
        _moba_kernel,
        grid=(b, N_PAIRS // MOBA_PAIRS, nb // 2),
        in_specs=[seq_spec, seq_spec, seq_spec],
        out_specs=pl.BlockSpec((1, 2 * MOBA_BLOCK, width), lambda i, p, j: (i, j, p)),
        out_shape=jax.ShapeDtypeStruct((b, s, MIX_WIDTH), BF16),
        scratch_shapes=[pltpu.VMEM((MOBA_PAIRS, s, 2 * LANES), BF16),
                        pltpu.VMEM((n_heads, VT_ROWS, s), BF16),
                        pltpu.VMEM((MOBA_PAIRS, NB_PAD, PAIR), F32),
                        pltpu.VMEM((n_heads, 2 * LANES, s), BF16),
                        pltpu.VMEM((n_heads, 2 * MOBA_BLOCK, 2 * MOBA_BLOCK), F32),
                        pltpu.VMEM((n_heads, 2 * MOBA_BLOCK, 2 * MOBA_BLOCK), F32)],
        compiler_params=pltpu.CompilerParams(
            dimension_semantics=("arbitrary", "arbitrary", "arbitrary"),
            vmem_limit_bytes=VMEM_LIMIT),
        name="moba",
    )(q, k, v)


def _outproj_kernel(mix_ref, z_ref, mq_ref, mkv_ref, h_ref, wmix_ref, wmem_ref, hg_ref, fg_ref,
                    o_ref, *, head_norm, final_norm):
    mq = mq_ref[0].astype(F32)
    mk = mkv_ref[0, :, :MEM_WIDTH]
    mv = mkv_ref[0, :, MEM_WIDTH:]
    row = lax.broadcasted_iota(jnp.int32, (MEM_WIDTH, MEM_WIDTH), 0)
    col = lax.broadcasted_iota(jnp.int32, (MEM_WIDTH, MEM_WIDTH), 1)
    lane = lax.broadcasted_iota(jnp.int32, (1, MEM_WIDTH), 1)
    mhs = [((lane >> 6) == h).astype(F32) for h in range(MEM_HEADS)]
    logits = [_dot_nt(mq * (mh * (QK_SCALE * LOG2E)), mk) for mh in mhs]
    es = [jnp.exp2(lg - jnp.max(lg, axis=-1, keepdims=True)) for lg in logits]
    inv = [1.0 / jnp.sum(e, axis=-1, keepdims=True) for e in es]
    pvs = [jnp.dot(e.astype(BF16), mv[:, (h // 2) * PAIR:(h // 2 + 1) * PAIR], preferred_element_type=F32)
           for h, e in enumerate(es)]
    pair_masks = _head_masks()
    memo = jnp.concatenate(
        [pvs[2 * pp] * (pair_masks[0] * inv[2 * pp]) + pvs[2 * pp + 1] * (pair_masks[1] * inv[2 * pp + 1])
         for pp in range(MEM_HEADS // 2)], axis=1)
    z = z_ref[0].astype(F32)
    gate = _silu(z)
    mix = mix_ref[0].astype(F32)
    if head_norm:
        same_head = ((row >> 6) == (col >> 6)).astype(BF16)
        slabs = []
        for sl in range(MIX_WIDTH // MEM_WIDTH):
            ms = mix[:, sl * MEM_WIDTH:(sl + 1) * MEM_WIDTH]
            ss = _dot(ms * ms, same_head) * (1.0 / HEAD_DIM)
            slabs.append(ms * lax.rsqrt(ss + NORM_EPS))
        mix = jnp.concatenate(slabs, axis=1) * hg_ref[...]
    y_mix = mix * gate[:, :MIX_WIDTH]
    y_mem = memo * gate[:, MIX_WIDTH:]
    out = h_ref[0] + _dot(y_mix, wmix_ref[...]) + _dot(y_mem, wmem_ref[...])
    if final_norm:
        out = out * _rms_scale(out) * fg_ref[...]
    o_ref[0] = out


def _outproj(mix, z, mq, mkv, h, wmix, wmem, hg, fg, tm, head_norm, final_norm):
    b, s, _ = h.shape
    tile = lambda n: pl.BlockSpec((1, tm, n), lambda i, j: (i, j, 0))
    return pl.pallas_call(
        functools.partial(_outproj_kernel, head_norm=head_norm, final_norm=final_norm),
        grid=(b, s // tm),
        in_specs=[tile(MIX_WIDTH), tile(GATE_WIDTH), tile(MEM_WIDTH),
                  pl.BlockSpec((1, N_MEM, 2 * MEM_WIDTH), lambda i, j: (i, 0, 0)),
                  tile(D_MODEL),
                  pl.BlockSpec((MIX_WIDTH, D_MODEL), lambda i, j: (0, 0)),
                  pl.BlockSpec((MEM_WIDTH, D_MODEL), lambda i, j: (0, 0)),
                  pl.BlockSpec((1, MIX_WIDTH), lambda i, j: (0, 0)),
                  pl.BlockSpec((1, D_MODEL), lambda i, j: (0, 0))],
        out_specs=tile(D_MODEL),
        out_shape=jax.ShapeDtypeStruct((b, s, D_MODEL), F32),
        compiler_params=pltpu.CompilerParams(vmem_limit_bytes=VMEM_LIMIT),
        name="outproj_final" if final_norm else "outproj",
    )(mix, z, mq, mkv, h, wmix, wmem, hg, fg)


def _row(v):
    return v.reshape(1, -1).astype(F32)


def kernel(x, mem, positions, norm_0, w_in_0, conv_w_0, a_log_0, dt_bias_0, o_norm_0,
           mem_norm_0, w_mem_kv_0, w_out_0, norm_1, w_in_1, mem_norm_1, w_mem_kv_1,
           w_out_1, final_norm):
    b, s, _ = x.shape
    assert s % (2 * MOBA_BLOCK) == 0 and MOBA_TOPK <= s // MOBA_BLOCK <= NB_PAD
    assert s % (DELTA_GROUP * DELTA_CHUNK) == 0
    tm = 512

    mkv0, mkv1 = _memkv(mem, _row(mem_norm_0), w_mem_kv_0.astype(BF16),
                        _row(mem_norm_1), w_mem_kv_1.astype(BF16))

    i1 = 3 * MIX_WIDTH
    i2 = i1 + GATE_WIDTH
    i3 = i2 + MEM_WIDTH
    w0 = w_in_0.astype(BF16)
    wba = jnp.pad(w0[:, i3:], ((0, 0), (0, LANES - 2 * MIX_HEADS)))
    qkv, z0, mq0, ba = _inproj0(x, _row(norm_0), w0[:, :i1], w0[:, i1:i2], w0[:, i2:i3], wba, tm)
    ba_cols = lambda v: _row(jnp.pad(v, (MIX_HEADS, LANES - 2 * MIX_HEADS)))
    o0 = _delta_mixer(qkv, ba, conv_w_0.astype(F32), ba_cols(a_log_0), ba_cols(dt_bias_0))
    wo0 = w_out_0.astype(BF16)
    h1 = _outproj(o0, z0, mq0, mkv0, x, wo0[:MIX_WIDTH], wo0[MIX_WIDTH:],
                  _row(jnp.tile(o_norm_0, MIX_HEADS)), _row(final_norm),
                  tm, head_norm=True, final_norm=False)

    w1 = w_in_1.astype(BF16)
    half = HEAD_DIM // 2
    inv_freq = ROPE_THETA ** (-jnp.arange(half, dtype=F32) * (2.0 / HEAD_DIM))
    invf = _row(jnp.tile(inv_freq, LANES // half))
    q1, k1, v1, z1, mq1 = _inproj1(
        h1, _row(norm_1), positions.reshape(b, s, 1), invf,
        w1[:, :MIX_WIDTH], w1[:, MIX_WIDTH:2 * MIX_WIDTH], w1[:, 2 * MIX_WIDTH:i1],
        w1[:, i1:i2], w1[:, i2:], tm)
    o1 = _moba(q1, k1, v1)
    wo1 = w_out_1.astype(BF16)
    return _outproj(o1, z1, mq1, mkv1, h1, wo1[:MIX_WIDTH], wo1[MIX_WIDTH:],
                    jnp.ones((1, MIX_WIDTH), F32), _row(final_norm),
                    tm, head_norm=False, final_norm=True)
```

```python
import functools

import jax
import jax.numpy as jnp
from jax import lax
from jax.experimental import pallas as pl
from jax.experimental.pallas import tpu as pltpu

D_MODEL = 1024
HEAD_DIM = 64
MIX_HEADS = 12
MEM_HEADS = 4
MIX_WIDTH = MIX_HEADS * HEAD_DIM
MEM_WIDTH = MEM_HEADS * HEAD_DIM
GATE_WIDTH = MIX_WIDTH + MEM_WIDTH
N_MEM = 256
CONV_WIDTH = 4
MOBA_BLOCK = 256
MOBA_TOPK = 3
ROPE_THETA = 10000.0
NORM_EPS = 1e-6
MASK_VALUE = -1e30

LANES = 128
SUBLANES = 8
PAIR = 2 * HEAD_DIM
N_PAIRS = MIX_HEADS // 2
DELTA_CHUNK = 128
DELTA_GROUP = 8
INV_BASE = 16
QK_SCALE = HEAD_DIM ** -0.5
LOG2E = 1.4426950408889634
MOBA_PAIRS = 2
NB_PAD = 16
VT_ROWS = HEAD_DIM + 16
VMEM_LIMIT = 56 * 1024 * 1024

BF16 = jnp.bfloat16
F32 = jnp.float32

_NT = (((1,), (1,)), ((), ()))


def _dot(a, b):
    return jnp.dot(a.astype(BF16), b.astype(BF16), preferred_element_type=F32)


def _dot_nt(a, b):
    return lax.dot_general(a.astype(BF16), b.astype(BF16), _NT, preferred_element_type=F32)


def _dot_hi(a, b):
    return jnp.dot(a, b, precision=lax.Precision.HIGHEST, preferred_element_type=F32)


def _dot_nt_hi(a, b):
    return lax.dot_general(a, b, _NT, precision=lax.Precision.HIGHEST,
                           preferred_element_type=F32)


def _rms_scale(x):
    return lax.rsqrt(jnp.mean(x * x, axis=-1, keepdims=True) + NORM_EPS)


def _sigmoid(x):
    return 1.0 / (1.0 + jnp.exp(-x))


def _silu(x):
    half = 0.5 * x
    return half + half * jnp.tanh(half)


def _softplus(x):
    return jnp.maximum(x, 0.0) + jnp.log(1.0 + jnp.exp(-jnp.abs(x)))


def _head_masks():
    lane = lax.broadcasted_iota(jnp.int32, (1, PAIR), 1)
    return [((lane >> 6) == h).astype(F32) for h in range(2)]


def _memkv_kernel(mem_ref, g0_ref, w0_ref, g1_ref, w1_ref, o0_ref, o1_ref):
    m = mem_ref[0]
    mn = m * _rms_scale(m)
    o0_ref[0] = _dot(mn * g0_ref[...], w0_ref[...]).astype(o0_ref.dtype)
    o1_ref[0] = _dot(mn * g1_ref[...], w1_ref[...]).astype(o1_ref.dtype)


def _memkv(mem, g0, w0, g1, w1):
    b = mem.shape[0]
    full = lambda shp: pl.BlockSpec(shp, lambda i: (0,) * len(shp))
    return pl.pallas_call(
        _memkv_kernel,
        grid=(b,),
        in_specs=[pl.BlockSpec((1, N_MEM, D_MODEL), lambda i: (i, 0, 0)),
                  full((1, D_MODEL)), full((D_MODEL, 2 * MEM_WIDTH)),
                  full((1, D_MODEL)), full((D_MODEL, 2 * MEM_WIDTH))],
        out_specs=[pl.BlockSpec((1, N_MEM, 2 * MEM_WIDTH), lambda i: (i, 0, 0))] * 2,
        out_shape=[jax.ShapeDtypeStruct((b, N_MEM, 2 * MEM_WIDTH), BF16)] * 2,
        compiler_params=pltpu.CompilerParams(vmem_limit_bytes=VMEM_LIMIT),
        name="memkv",
    )(mem, g0, w0, g1, w1)


def _inproj0_kernel(x_ref, g_ref, wqkv_ref, wz_ref, wmq_ref, wba_ref,
                    qkv_ref, z_ref, mq_ref, ba_ref):
    x = x_ref[0]
    xn = (x * _rms_scale(x) * g_ref[...]).astype(BF16)
    for w_ref, o_ref in ((wqkv_ref, qkv_ref), (wz_ref, z_ref),
                         (wmq_ref, mq_ref), (wba_ref, ba_ref)):
        o_ref[0] = jnp.dot(xn, w_ref[...], preferred_element_type=F32).astype(o_ref.dtype)


def _inproj0(x, g, wqkv, wz, wmq, wba, tm):
    b, s, _ = x.shape
    widths = (3 * MIX_WIDTH, GATE_WIDTH, MEM_WIDTH, LANES)
    wspec = lambda n: pl.BlockSpec((D_MODEL, n), lambda i, j: (0, 0))
    return pl.pallas_call(
        _inproj0_kernel,
        grid=(b, s // tm),
        in_specs=[pl.BlockSpec((1, tm, D_MODEL), lambda i, j: (i, j, 0)),
                  pl.BlockSpec((1, D_MODEL), lambda i, j: (0, 0))]
                 + [wspec(n) for n in widths],
        out_specs=[pl.BlockSpec((1, tm, n), lambda i, j: (i, j, 0)) for n in widths],
        out_shape=[jax.ShapeDtypeStruct((b, s, n), dt)
                   for n, dt in zip(widths, (BF16, BF16, BF16, F32))],
        compiler_params=pltpu.CompilerParams(vmem_limit_bytes=VMEM_LIMIT),
        name="inproj0",
    )(x, g, wqkv, wz, wmq, wba)


def _inproj1_kernel(x_ref, g_ref, pos_ref, invf_ref, wq_ref, wk_ref, wv_ref, wz_ref, wmq_ref,
                    q_ref, k_ref, v_ref, z_ref, mq_ref):
    x = x_ref[0]
    xn = (x * _rms_scale(x) * g_ref[...]).astype(BF16)
    n_freq = HEAD_DIM // 2
    per_pack = LANES // n_freq
    tm = x.shape[0]
    pos = pos_ref[0].astype(F32)
    lane_p = lax.broadcasted_iota(jnp.int32, (LANES, LANES), 1)
    block = lane_p // n_freq
    cos_parts, sin_parts = [], []
    for base in range(0, tm, per_pack * LANES):
        pos_p = jnp.zeros((LANES, LANES), F32)
        for gidx in range(per_pack):
            rows = slice(base + gidx * LANES, base + (gidx + 1) * LANES)
            pos_p = jnp.where(block == gidx, pos[rows], pos_p)
        ang_p = pos_p * invf_ref[...]
        for trig, parts in ((jnp.cos(ang_p), cos_parts), (jnp.sin(ang_p), sin_parts)):
            for gidx in range(per_pack):
                one = jnp.where(block == gidx, trig, 0.0)
                spread = one
                for k in range(1, per_pack):
                    spread = spread + pltpu.roll(one, k * n_freq, 1)
                parts.append(spread)
    cos = jnp.concatenate(cos_parts, axis=0)
    sin = jnp.concatenate(sin_parts, axis=0)
    lane = lax.broadcasted_iota(jnp.int32, (tm, LANES), 1)
    first_half = (lane & (HEAD_DIM - 1)) < (HEAD_DIM // 2)
    sin_signed = jnp.where(first_half, -sin, sin)
    for w_ref, o_ref in ((wq_ref, q_ref), (wk_ref, k_ref)):
        y = jnp.dot(xn, w_ref[...], preferred_element_type=F32)
        for sl in range(MIX_WIDTH // LANES):
            ys = y[:, sl * LANES:(sl + 1) * LANES]
            rot = jnp.where(first_half,
                            pltpu.roll(ys, LANES - HEAD_DIM // 2, 1),
                            pltpu.roll(ys, HEAD_DIM // 2, 1))
            o_ref[0, :, sl * LANES:(sl + 1) * LANES] = (ys * cos + rot * sin_signed).astype(o_ref.dtype)
    for w_ref, o_ref in ((wv_ref, v_ref), (wz_ref, z_ref), (wmq_ref, mq_ref)):
        o_ref[0] = jnp.dot(xn, w_ref[...], preferred_element_type=F32).astype(o_ref.dtype)


def _inproj1(x, g, pos3, invf, wq, wk, wv, wz, wmq, tm):
    b, s, _ = x.shape
    widths = (MIX_WIDTH, MIX_WIDTH, MIX_WIDTH, GATE_WIDTH, MEM_WIDTH)
    wspec = lambda n: pl.BlockSpec((D_MODEL, n), lambda i, j: (0, 0))
    return pl.pallas_call(
        _inproj1_kernel,
        grid=(b, s // tm),
        in_specs=[pl.BlockSpec((1, tm, D_MODEL), lambda i, j: (i, j, 0)),
                  pl.BlockSpec((1, D_MODEL), lambda i, j: (0, 0)),
                  pl.BlockSpec((1, tm, 1), lambda i, j: (i, j, 0)),
                  pl.BlockSpec((1, LANES), lambda i, j: (0, 0))]
                 + [wspec(n) for n in widths],
        out_specs=[pl.BlockSpec((1, tm, n), lambda i, j: (i, j, 0)) for n in widths],
        out_shape=[jax.ShapeDtypeStruct((b, s, n), BF16) for n in widths],
        compiler_params=pltpu.CompilerParams(vmem_limit_bytes=VMEM_LIMIT),
        name="inproj1",
    )(x, g, pos3, invf, wq, wk, wv, wz, wmq)


def _unit_lower_inverses(lows, row, col, eye, between):
    c = lows[0].shape[0]
    shift = INV_BASE.bit_length() - 1
    base = (row >> shift) == (col >> shift)
    ns = [jnp.where(base, -low, 0.0) for low in lows]
    xs = [eye + n for n in ns]
    ns = [_dot(n, n) for n in ns]
    between()
    power = 2
    while 2 * power < INV_BASE:
        prods = [_dot(jnp.concatenate([n, x], axis=0), n) for x, n in zip(xs, ns)]
        between()
        xs = [x + p[c:] for x, p in zip(xs, prods)]
        ns = [p[:c] for p in prods]
        power *= 2
    xs = [x + _dot(x, n) for x, n in zip(xs, ns)]
    between()
    size = INV_BASE
    while size < c:
        sh = size.bit_length() - 1
        sel = (((row >> (sh + 1)) == (col >> (sh + 1)))
               & (((row >> sh) & 1) == 1) & (((col >> sh) & 1) == 0))
        odd = [slice(r, r + size) for r in range(size, c, 2 * size)]
        x_odd = [jnp.concatenate([x[sl] for sl in odd], axis=0) for x in xs]
        ys = [_dot(xo, jnp.where(sel, low, 0.0)) for xo, low in zip(x_odd, lows)]
        between()
        new_odd = [xo - _dot(y, x) for xo, y, x in zip(x_odd, ys, xs)]
        between()
        xs = [jnp.concatenate(
            [blk for k, sl in enumerate(odd)
             for blk in (x[sl.start - size:sl.start], no[k * size:(k + 1) * size])], axis=0)
            for x, no in zip(xs, new_odd)]
        size *= 2
    return xs


def _delta_kernel(q_ref, k_ref, v_ref, ba_ref, cwq_ref, cwk_ref, cwv_ref,
                  alog_ref, dt_ref, o_ref,
                  ext_ref, p_ref, qp_ref, n_ref, op_ref, egl_ref):
    c = DELTA_CHUNK
    grp = DELTA_GROUP
    rows = grp * c
    seq = q_ref.shape[1]
    pair = pl.program_id(1)
    masks = _head_masks()
    row = lax.broadcasted_iota(jnp.int32, (c, c), 0)
    col = lax.broadcasted_iota(jnp.int32, (c, c), 1)
    tri_incl = row >= col
    tri_strict = row > col
    eye = (row == col).astype(F32)
    tril_b = tri_incl.astype(BF16)
    head_block = ((row >> 6) == (col >> 6)).astype(F32)
    head_block_b = head_block.astype(BF16)
    row2 = lax.broadcasted_iota(jnp.int32, (LANES, 2 * PAIR), 0)
    col2 = lax.broadcasted_iota(jnp.int32, (LANES, 2 * PAIR), 1)
    expand = (row2 == jnp.where(col2 < PAIR, 0, MIX_HEADS) + 2 * pair + ((col2 >> 6) & 1)).astype(BF16)
    lane_n = lax.broadcasted_iota(jnp.int32, (1, LANES), 1)
    is_beta = lane_n < MIX_HEADS
    neg_a = -jnp.exp(alog_ref[...])
    dt_row = dt_ref[...]

    ext_ref[:, 0:SUBLANES, :] = jnp.zeros((3, SUBLANES, PAIR), F32)

    def conv_silu(idx, x_ref, cw_ref, t0):
        raw = x_ref[0, pl.ds(t0, rows), :].astype(F32)
        ext_ref[idx, SUBLANES:SUBLANES + rows, :] = raw
        acc = raw * cw_ref[CONV_WIDTH - 1:CONV_WIDTH, :]
        for tap in range(CONV_WIDTH - 1):
            back = CONV_WIDTH - 1 - tap
            acc = acc + ext_ref[idx, SUBLANES - back:SUBLANES - back + rows, :] * cw_ref[tap:tap + 1, :]
        ext_ref[idx, 0:SUBLANES, :] = raw[rows - SUBLANES:rows, :]
        return _silu(acc)

    def l2norm(x):
        return x * lax.rsqrt(_dot(x * x, head_block_b) + NORM_EPS)

    def build_group(gi, between):
        t0 = pl.multiple_of(gi * rows, rows)
        qn = l2norm(conv_silu(0, q_ref, cwq_ref, t0))
        kn = l2norm(conv_silu(1, k_ref, cwk_ref, t0))
        vc = conv_silu(2, v_ref, cwv_ref, t0)
        ba = ba_ref[0, pl.ds(t0, rows), :]
        act = jnp.where(is_beta, _sigmoid(ba), neg_a * _softplus(ba + dt_row))
        bg = jnp.dot(act.astype(BF16), expand, preferred_element_type=F32)
        beta = bg[:, :PAIR]
        g = bg[:, PAIR:]
        kb = kn * beta
        vb = vc * beta
        sls = [slice(ci * c, (ci + 1) * c) for ci in range(grp)]
        gcs = [jnp.dot(tril_b, g[sl].astype(BF16), preferred_element_type=F32) * LOG2E for sl in sls]
        between()
        gcts = [gc.T for gc in gcs]
        egs = [jnp.exp2(gc) for gc in gcs]
        probs = [(ci, h) for ci in range(grp) for h in range(2)]
        decays = []
        for ci, h in probs:
            gcol = jnp.broadcast_to(gcs[ci][:, h * HEAD_DIM:h * HEAD_DIM + 1], (c, c))
            grow = gcts[ci][h * HEAD_DIM:h * HEAD_DIM + 1, :]
            decays.append(jnp.exp2(jnp.where(tri_incl, gcol - grow, -jnp.inf)))
        kbhs = [kb[sls[ci]] * masks[h] for ci, h in probs]
        kq = [_dot_nt(jnp.concatenate(
            [kbhs[2 * ci], kbhs[2 * ci + 1]]
            + [qn[sls[ci]] * (masks[h] * QK_SCALE) for h in range(2)], axis=0), kn[sls[ci]])
            for ci in range(grp)]
        between()
        lows = [jnp.where(tri_strict, kq[ci][h * c:(h + 1) * c] * d, 0.0)
                for (ci, h), d in zip(probs, decays)]
        aqks = [jnp.where(tri_incl, kq[ci][(2 + h) * c:(3 + h) * c] * d, 0.0)
                for (ci, h), d in zip(probs, decays)]
        tinvs = _unit_lower_inverses(lows, row, col, eye, between)
        wus = [_dot(t, jnp.concatenate([kbh * egs[ci], vb[sls[ci]] * masks[h]], axis=1))
               for t, kbh, (ci, h) in zip(tinvs, kbhs, probs)]
        aus = [_dot(a, wu) for a, wu in zip(aqks, wus)]
        kds = [kn[sls[ci]] * jnp.exp2(gcs[ci][c - 1:c, :] - gcs[ci]) for ci in range(grp)]
        pns = [_dot(kds[ci].T, wus[2 * ci] + wus[2 * ci + 1]) for ci in range(grp)]
        for ci in range(grp):
            cidx = gi * grp + ci
            au = aus[2 * ci] + aus[2 * ci + 1]
            p_ref[cidx] = (head_block * pns[ci][:, :PAIR]).astype(BF16)
            n_ref[cidx] = head_block * pns[ci][:, PAIR:]
            qp_ref[cidx] = (qn[sls[ci]] * (QK_SCALE * egs[ci]) - au[:, :PAIR]).astype(BF16)
            op_ref[cidx] = au[:, PAIR:]
            egl_ref[pl.ds(cidx, 1), :] = jnp.exp2(gcs[ci][c - 1:c, :])

    def scan_step(ci, state):
        sb = state.astype(BF16)
        o = jnp.dot(qp_ref[ci], sb, preferred_element_type=F32) + op_ref[ci]
        new_state = (state * egl_ref[pl.ds(ci, 1), :]
                     - jnp.dot(p_ref[ci], sb, preferred_element_type=F32) + n_ref[ci])
        o_ref[0, pl.ds(pl.multiple_of(ci * c, c), c), :] = o.astype(o_ref.dtype)
        return new_state

    def build_and_scan(gi, state):
        box = [state]
        pending = iter(range(grp))

        def one_scan_step():
            ci = next(pending, None)
            if ci is not None:
                box[0] = scan_step((gi - 1) * grp + ci, box[0])

        build_group(gi, one_scan_step)
        for ci in pending:
            box[0] = scan_step((gi - 1) * grp + ci, box[0])
        return box[0]

    n_groups = seq // rows
    build_group(0, lambda: None)
    state = lax.fori_loop(1, n_groups, build_and_scan, jnp.zeros((PAIR, PAIR), F32))
    lax.fori_loop((n_groups - 1) * grp, n_groups * grp, scan_step, state)


def _delta_mixer(qkv, ba, conv_w, alog_row, dt_row):
    b, s, _ = qkv.shape
    n_chunks = s // DELTA_CHUNK
    seq_spec = lambda off: pl.BlockSpec((1, s, PAIR), lambda i, p, off=off: (i, 0, off + p))
    cw_spec = lambda off: pl.BlockSpec((CONV_WIDTH, PAIR), lambda i, p, off=off: (0, off + p))
    row_spec = pl.BlockSpec((1, LANES), lambda i, p: (0, 0))
    return pl.pallas_call(
        _delta_kernel,
        grid=(b, N_PAIRS),
        in_specs=[seq_spec(0), seq_spec(N_PAIRS), seq_spec(2 * N_PAIRS),
                  pl.BlockSpec((1, s, LANES), lambda i, p: (i, 0, 0)),
                  cw_spec(0), cw_spec(N_PAIRS), cw_spec(2 * N_PAIRS),
                  row_spec, row_spec],
        out_specs=pl.BlockSpec((1, s, PAIR), lambda i, p: (i, 0, p)),
        out_shape=jax.ShapeDtypeStruct((b, s, MIX_WIDTH), BF16),
        scratch_shapes=[pltpu.VMEM((3, SUBLANES + DELTA_GROUP * DELTA_CHUNK, PAIR), F32),
                        pltpu.VMEM((n_chunks, PAIR, PAIR), BF16),
                        pltpu.VMEM((n_chunks, DELTA_CHUNK, PAIR), BF16),
                        pltpu.VMEM((n_chunks, PAIR, PAIR), F32),
                        pltpu.VMEM((n_chunks, DELTA_CHUNK, PAIR), F32),
                        pltpu.VMEM((n_chunks, PAIR), F32)],
        compiler_params=pltpu.CompilerParams(vmem_limit_bytes=VMEM_LIMIT),
        name="delta_mixer",
    )(qkv, qkv, qkv, ba, conv_w, conv_w, conv_w, alog_row, dt_row)


def _moba_kernel(q_ref, k_ref, v_ref, o_ref, kaug_ref, vt_ref, kmean_ref, qaug_ref, sa_ref, sb_ref):
    blk = MOBA_BLOCK
    qw = 2 * blk
    seq = k_ref.shape[1]
    nb = seq // blk
    step = pl.program_id(2)
    masks = _head_masks()
    n_pairs = MOBA_PAIRS
    heads = range(2 * n_pairs)

    @pl.when(step == 0)
    def _prepare():
        lane = lax.broadcasted_iota(jnp.int32, (blk, LANES), 1)
        ones_row = (lax.broadcasted_iota(jnp.int32, (VT_ROWS - HEAD_DIM, LANES), 0) == 0).astype(BF16)

        def key_block(j, carry):
            r0 = pl.multiple_of(j * blk, blk)
            onehot = (lane == j).astype(BF16)
            for pr in range(n_pairs):
                kj = k_ref[0, pl.ds(r0, blk), pr * PAIR:(pr + 1) * PAIR]
                vj = v_ref[0, pl.ds(r0, blk), pr * PAIR:(pr + 1) * PAIR].astype(F32)
                kaug_ref[pr, pl.ds(r0, blk), 0:LANES] = kj
                kaug_ref[pr, pl.ds(r0, blk), LANES:2 * LANES] = onehot
                for t in range(blk // LANES):
                    c0 = pl.multiple_of(r0 + t * LANES, LANES)
                    vt = vj[t * LANES:(t + 1) * LANES, :].T.astype(BF16)
                    for h in range(2):
                        vt_ref[2 * pr + h, 0:HEAD_DIM, pl.ds(c0, LANES)] = vt[h * HEAD_DIM:(h + 1) * HEAD_DIM, :]
                        vt_ref[2 * pr + h, HEAD_DIM:VT_ROWS, pl.ds(c0, LANES)] = ones_row
                kmean_ref[pr, pl.ds(j, 1), :] = jnp.sum(kj.astype(F32), axis=0, keepdims=True) * (1.0 / blk)
            return carry
        kmean_ref[...] = jnp.zeros((n_pairs, NB_PAD, PAIR), F32)
        lax.fori_loop(0, nb, key_block, 0)

        kstacks = []
        for pr in range(n_pairs):
            pieces = []
            for h in range(2):
                rest = kmean_ref[pr] * masks[h]
                for _ in range(3):
                    part = rest.astype(BF16)
                    pieces.append(part)
                    rest = rest - part.astype(F32)
            kstacks.append(jnp.concatenate(pieces, axis=0))
        blk_id = lax.broadcasted_iota(jnp.int32, (NB_PAD, qw), 0)
        blk_f = blk_id.astype(F32)
        half = lax.broadcasted_iota(jnp.int32, (NB_PAD, qw), 1) >> (blk.bit_length() - 1)
        head_rows = lax.broadcasted_iota(jnp.int32, (PAIR, qw), 0) >> 6
        pad_rows = jnp.zeros((PAIR - NB_PAD, qw), BF16)

        def query_blocks(m, carry):
            c0 = pl.multiple_of(m * qw, qw)
            own_blk = 2 * m + half
            valid = blk_id < own_blk
            q_ts, gs, opens = [], [], []
            for pr in range(n_pairs):
                q = q_ref[0, pl.ds(c0, qw), pr * PAIR:(pr + 1) * PAIR]
                qf = q.astype(F32) * (QK_SCALE * LOG2E)
                q_ts.append(jnp.concatenate(
                    [qf[t * LANES:(t + 1) * LANES, :].T for t in range(qw // LANES)], axis=1))
                gate_parts = lax.dot_general(kstacks[pr], q, _NT, preferred_element_type=F32)
                for h in range(2):
                    g3 = [gate_parts[(3 * h + i) * NB_PAD:(3 * h + i + 1) * NB_PAD] for i in range(3)]
                    gs.append(jnp.where(valid, g3[0] + g3[1] + g3[2], -jnp.inf))
                    opens.append((blk_id == own_blk).astype(F32))
            for _ in range(MOBA_TOPK):
                for hh in heads:
                    top = jnp.max(gs[hh], axis=0, keepdims=True)
                    first = jnp.min(jnp.where(gs[hh] == top, blk_f, float(NB_PAD)), axis=0, keepdims=True)
                    pick = (blk_f == first) & (top > -jnp.inf)
                    opens[hh] = jnp.where(pick, 1.0, opens[hh])
                    gs[hh] = jnp.where(pick, -jnp.inf, gs[hh])
            for hh in heads:
                bias = jnp.where(opens[hh] > 0.5, 0.0, MASK_VALUE).astype(BF16)
                qaug_ref[hh, 0:PAIR, pl.ds(c0, qw)] = jnp.where(
                    head_rows == hh % 2, q_ts[hh // 2], 0.0).astype(BF16)
                qaug_ref[hh, PAIR:PAIR + NB_PAD, pl.ds(c0, qw)] = bias
                qaug_ref[hh, PAIR + NB_PAD:2 * PAIR, pl.ds(c0, qw)] = pad_rows
            return carry
        lax.fori_loop(0, nb // 2, query_blocks, 0)

    qa = 2 * step
    qb = qa + 1
    qs_a = [qaug_ref[hh, :, pl.ds(pl.multiple_of(qa * qw, qw), qw)] for hh in heads]
    qs_b = [qaug_ref[hh, :, pl.ds(pl.multiple_of(qb * qw, qw), qw)] for hh in heads]

    key_pos = lax.broadcasted_iota(jnp.int32, (blk, blk), 0)
    qry_pos = lax.broadcasted_iota(jnp.int32, (blk, blk), 1)
    always = jnp.int32(1 << 20)

    def scores(qi, qs, t, s_ref, where):
        r0 = pl.multiple_of(t * (2 * blk), 2 * blk)
        k2 = [kaug_ref[pr, pl.ds(r0, 2 * blk), :] for pr in range(n_pairs)]
        sts = [jnp.dot(k2[hh // 2], qs[hh], preferred_element_type=F32) for hh in heads]
        maxima = []
        for hh, st in enumerate(sts):
            if where != "past":
                causal = key_pos <= qry_pos + (0 if where == "own" else jnp.where(t == qi, 0, always))
                st = jnp.concatenate(
                    [jnp.concatenate([jnp.where(causal, st[:blk, :blk], MASK_VALUE), st[:blk, blk:]], axis=1),
                     jnp.concatenate([st[blk:, :blk], jnp.where(causal, st[blk:, blk:], MASK_VALUE)], axis=1)],
                    axis=0)
            s_ref[hh] = st
            maxima.append(jnp.max(st, axis=0, keepdims=True))
        return maxima

    def update(t, state, maxima, s_ref):
        r0 = pl.multiple_of(t * (2 * blk), 2 * blk)
        m_new = [jnp.maximum(state[2 * hh], maxima[hh]) for hh in heads]
        ps = [jnp.exp2(s_ref[hh] - m_new[hh]).astype(BF16) for hh in heads]
        pvs = [jnp.dot(vt_ref[hh, :, pl.ds(r0, 2 * blk)], ps[hh], preferred_element_type=F32)
               for hh in heads]
        out = []
        for hh in heads:
            alpha = jnp.exp2(state[2 * hh] - m_new[hh])
            out += [m_new[hh], state[2 * hh + 1] * alpha + pvs[hh]]
        return out

    init = []
    for hh in heads:
        init += [jnp.full((1, qw), MASK_VALUE, F32), jnp.zeros((VT_ROWS, qw), F32)]
    n_state = len(init)

    def write_rows(which, fin):
        for pr in range(n_pairs):
            out_t = jnp.concatenate(
                [fin[2 * hh + 1][:HEAD_DIM] / fin[2 * hh + 1][HEAD_DIM:HEAD_DIM + 1]
                 for hh in (2 * pr, 2 * pr + 1)], axis=0)
            o_ref[0, which * qw:(which + 1) * qw, pr * PAIR:(pr + 1) * PAIR] = out_t.T.astype(o_ref.dtype)

    def pairs_a(u, carry):
        t0 = 2 * u
        max_b = scores(qa, qs_a, t0 + 1, sb_ref, "past")
        state = update(t0, carry[:n_state], carry[n_state:], sa_ref)
        max_a = scores(qa, qs_a, t0 + 2, sa_ref, "any")
        state = update(t0 + 1, state, max_b, sb_ref)
        return tuple(state + max_a)

    carry = lax.fori_loop(0, step, pairs_a, tuple(init + scores(qa, qs_a, 0, sa_ref, "any")))
    first_b = scores(qb, qs_b, 0, sb_ref, "past")
    write_rows(0, update(qa, carry[:n_state], carry[n_state:], sa_ref))

    def pairs_b(u, carry):
        t0 = 2 * u
        max_a = scores(qb, qs_b, t0 + 1, sa_ref, "past")
        state = update(t0, carry[:n_state], carry[n_state:], sb_ref)
        max_b = scores(qb, qs_b, t0 + 2, sb_ref, "past")
        state = update(t0 + 1, state, max_a, sa_ref)
        return tuple(state + max_b)

    carry = lax.fori_loop(0, step, pairs_b, tuple(init + first_b))
    last_b = scores(qb, qs_b, qb, sa_ref, "own")
    state = update(qb - 1, carry[:n_state], carry[n_state:], sb_ref)
    write_rows(1, update(qb, state, last_b, sa_ref))


def _moba(q, k, v):
    b, s, _ = q.shape
    nb = s // MOBA_BLOCK
    width = MOBA_PAIRS * PAIR
    n_heads = 2 * MOBA_PAIRS
    seq_spec = pl.BlockSpec((1, s, width), lambda i, p, j: (i, 0, p))
    return pl.pallas_call(
        _moba_kernel,
        grid=(b, N_PAIRS // MOBA_PAIRS, nb // 4),
        in_specs=[seq_spec, seq_spec, seq_spec],
        out_specs=pl.BlockSpec((1, 4 * MOBA_BLOCK, width), lambda i, p, j: (i, j, p)),
        out_shape=jax.ShapeDtypeStruct((b, s, MIX_WIDTH), BF16),
        scratch_shapes=[pltpu.VMEM((MOBA_PAIRS, s, 2 * LANES), BF16),
                        pltpu.VMEM((n_heads, VT_ROWS, s), BF16),
                        pltpu.VMEM((MOBA_PAIRS, NB_PAD, PAIR), F32),
                        pltpu.VMEM((n_heads, 2 * LANES, s), BF16),
                        pltpu.VMEM((n_heads, 2 * MOBA_BLOCK, 2 * MOBA_BLOCK), F32),
                        pltpu.VMEM((n_heads, 2 * MOBA_BLOCK, 2 * MOBA_BLOCK), F32)],
        compiler_params=pltpu.CompilerParams(
            dimension_semantics=("arbitrary", "arbitrary", "arbitrary"),
            vmem_limit_bytes=VMEM_LIMIT),
        name="moba",
    )(q, k, v)


def _outproj_kernel(mix_ref, z_ref, mq_ref, mkv_ref, h_ref, wmix_ref, wmem_ref, hg_ref, fg_ref,
                    o_ref, *, head_norm, final_norm):
    mq = mq_ref[0].astype(F32)
    mk = mkv_ref[0, :, :MEM_WIDTH]
    mv = mkv_ref[0, :, MEM_WIDTH:]
    row = lax.broadcasted_iota(jnp.int32, (MEM_WIDTH, MEM_WIDTH), 0)
    col = lax.broadcasted_iota(jnp.int32, (MEM_WIDTH, MEM_WIDTH), 1)
    lane = lax.broadcasted_iota(jnp.int32, (1, MEM_WIDTH), 1)
    mhs = [((lane >> 6) == h).astype(F32) for h in range(MEM_HEADS)]
    logits = [_dot_nt(mq * (mh * (QK_SCALE * LOG2E)), mk) for mh in mhs]
    es = [jnp.exp2(lg - jnp.max(lg, axis=-1, keepdims=True)) for lg in logits]
    inv = [1.0 / jnp.sum(e, axis=-1, keepdims=True) for e in es]
    pvs = [jnp.dot(e.astype(BF16), mv[:, (h // 2) * PAIR:(h // 2 + 1) * PAIR], preferred_element_type=F32)
           for h, e in enumerate(es)]
    pair_masks = _head_masks()
    memo = jnp.concatenate(
        [pvs[2 * pp] * (pair_masks[0] * inv[2 * pp]) + pvs[2 * pp + 1] * (pair_masks[1] * inv[2 * pp + 1])
         for pp in range(MEM_HEADS // 2)], axis=1)
    z = z_ref[0].astype(F32)
    gate = _silu(z)
    mix = mix_ref[0].astype(F32)
    if head_norm:
        same_head = ((row >> 6) == (col >> 6)).astype(BF16)
        slabs = []
        for sl in range(MIX_WIDTH // MEM_WIDTH):
            ms = mix[:, sl * MEM_WIDTH:(sl + 1) * MEM_WIDTH]
            ss = _dot(ms * ms, same_head) * (1.0 / HEAD_DIM)
            slabs.append(ms * lax.rsqrt(ss + NORM_EPS))
        mix = jnp.concatenate(slabs, axis=1) * hg_ref[...]
    y_mix = mix * gate[:, :MIX_WIDTH]
    y_mem = memo * gate[:, MIX_WIDTH:]
    out = h_ref[0] + _dot(y_mix, wmix_ref[...]) + _dot(y_mem, wmem_ref[...])
    if final_norm:
        out = out * _rms_scale(out) * fg_ref[...]
    o_ref[0] = out


def _outproj(mix, z, mq, mkv, h, wmix, wmem, hg, fg, tm, head_norm, final_norm):
    b, s, _ = h.shape
    tile = lambda n: pl.BlockSpec((1, tm, n), lambda i, j: (i, j, 0))
    return pl.pallas_call(
        functools.partial(_outproj_kernel, head_norm=head_norm, final_norm=final_norm),
        grid=(b, s // tm),
        in_specs=[tile(MIX_WIDTH), tile(GATE_WIDTH), tile(MEM_WIDTH),
                  pl.BlockSpec((1, N_MEM, 2 * MEM_WIDTH), lambda i, j: (i, 0, 0)),
                  tile(D_MODEL),
                  pl.BlockSpec((MIX_WIDTH, D_MODEL), lambda i, j: (0, 0)),
                  pl.BlockSpec((MEM_WIDTH, D_MODEL), lambda i, j: (0, 0)),
                  pl.BlockSpec((1, MIX_WIDTH), lambda i, j: (0, 0)),
                  pl.BlockSpec((1, D_MODEL), lambda i, j: (0, 0))],
        out_specs=tile(D_MODEL),
        out_shape=jax.ShapeDtypeStruct((b, s, D_MODEL), F32),
        compiler_params=pltpu.CompilerParams(vmem_limit_bytes=VMEM_LIMIT),
        name="outproj_final" if final_norm else "outproj",
    )(mix, z, mq, mkv, h, wmix, wmem, hg, fg)


def _row(v):
    return v.reshape(1, -1).astype(F32)


def kernel(x, mem, positions, norm_0, w_in_0, conv_w_0, a_log_0, dt_bias_0, o_norm_0,
           mem_norm_0, w_mem_kv_0, w_out_0, norm_1, w_in_1, mem_norm_1, w_mem_kv_1,
           w_out_1, final_norm):
    b, s, _ = x.shape
    assert s % (4 * MOBA_BLOCK) == 0 and MOBA_TOPK <= s // MOBA_BLOCK <= NB_PAD
    assert s % (DELTA_GROUP * DELTA_CHUNK) == 0
    tm = 512

    mkv0, mkv1 = _memkv(mem, _row(mem_norm_0), w_mem_kv_0.astype(BF16),
                        _row(mem_norm_1), w_mem_kv_1.astype(BF16))

    i1 = 3 * MIX_WIDTH
    i2 = i1 + GATE_WIDTH
    i3 = i2 + MEM_WIDTH
    w0 = w_in_0.astype(BF16)
    wba = jnp.pad(w0[:, i3:], ((0, 0), (0, LANES - 2 * MIX_HEADS)))
    qkv, z0, mq0, ba = _inproj0(x, _row(norm_0), w0[:, :i1], w0[:, i1:i2], w0[:, i2:i3], wba, tm)
    ba_cols = lambda v: _row(jnp.pad(v, (MIX_HEADS, LANES - 2 * MIX_HEADS)))
    o0 = _delta_mixer(qkv, ba, conv_w_0.astype(F32), ba_cols(a_log_0), ba_cols(dt_bias_0))
    wo0 = w_out_0.astype(BF16)
    h1 = _outproj(o0, z0, mq0, mkv0, x, wo0[:MIX_WIDTH], wo0[MIX_WIDTH:],
                  _row(jnp.tile(o_norm_0, MIX_HEADS)), _row(final_norm),
                  tm, head_norm=True, final_norm=False)

    w1 = w_in_1.astype(BF16)
    half = HEAD_DIM // 2
    inv_freq = ROPE_THETA ** (-jnp.arange(half, dtype=F32) * (2.0 / HEAD_DIM))
    invf = _row(jnp.tile(inv_freq, LANES // half))
    q1, k1, v1, z1, mq1 = _inproj1(
        h1, _row(norm_1), positions.reshape(b, s, 1), invf,
        w1[:, :MIX_WIDTH], w1[:, MIX_WIDTH:2 * MIX_WIDTH], w1[:, 2 * MIX_WIDTH:i1],
        w1[:, i1:i2], w1[:, i2:], tm)
    o1 = _moba(q1, k1, v1)
    wo1 = w_out_1.astype(BF16)
    return _outproj(o1, z1, mq1, mkv1, h1, wo1[:MIX_WIDTH], wo1[MIX_WIDTH:],
                    jnp.ones((1, MIX_WIDTH), F32), _row(final_norm),
                    tm, head_norm=False, final_norm=True)
```

```python
import functools

import jax
import jax.numpy as jnp
from jax import lax
from jax.experimental import pallas as pl
from jax.experimental.pallas import tpu as pltpu

D_MODEL = 1024
HEAD_DIM = 64
HEAD_SHIFT = HEAD_DIM.bit_length() - 1
MIX_HEADS = 12
MEM_HEADS = 4
MIX_WIDTH = MIX_HEADS * HEAD_DIM
MEM_WIDTH = MEM_HEADS * HEAD_DIM
GATE_WIDTH = MIX_WIDTH + MEM_WIDTH
N_MEM = 256
CONV_WIDTH = 4
MOBA_BLOCK = 256
MOBA_TOPK = 3
ROPE_THETA = 10000.0
NORM_EPS = 1e-6
MASK_VALUE = -1e30

LANES = 128
SUBLANES = 8
PAIR = 2 * HEAD_DIM
N_PAIRS = MIX_HEADS // 2
DELTA_CHUNK = 128
DELTA_GROUP = 8
INV_BASE = 16
QK_SCALE = HEAD_DIM ** -0.5
LOG2E = 1.4426950408889634
MOBA_QSTEP = 4
MOBA_PAIRS = 2
NB_PAD = 16
VT_ROWS = HEAD_DIM + 16
VMEM_LIMIT = 56 * 1024 * 1024

BF16 = jnp.bfloat16
F32 = jnp.float32

_NT = (((1,), (1,)), ((), ()))


def _dot(a, b):
    return jnp.dot(a.astype(BF16), b.astype(BF16), preferred_element_type=F32)


def _dot_nt(a, b):
    return lax.dot_general(a.astype(BF16), b.astype(BF16), _NT, preferred_element_type=F32)


def _dot_hi(a, b):
    return jnp.dot(a, b, precision=lax.Precision.HIGHEST, preferred_element_type=F32)


def _dot_nt_hi(a, b):
    return lax.dot_general(a, b, _NT, precision=lax.Precision.HIGHEST,
                           preferred_element_type=F32)


def _rms_scale(x):
    return lax.rsqrt(jnp.mean(x * x, axis=-1, keepdims=True) + NORM_EPS)


def _sigmoid(x):
    return 1.0 / (1.0 + jnp.exp(-x))


def _silu(x):
    half = 0.5 * x
    return half + half * jnp.tanh(half)


def _softplus(x):
    return jnp.maximum(x, 0.0) + jnp.log(1.0 + jnp.exp(-jnp.abs(x)))


def _head_masks():
    lane = lax.broadcasted_iota(jnp.int32, (1, PAIR), 1)
    return [((lane >> HEAD_SHIFT) == h).astype(F32) for h in range(2)]


def _memkv_kernel(mem_ref, g0_ref, w0_ref, g1_ref, w1_ref, o0_ref, o1_ref):
    m = mem_ref[0]
    mn = m * _rms_scale(m)
    o0_ref[0] = _dot(mn * g0_ref[...], w0_ref[...]).astype(o0_ref.dtype)
    o1_ref[0] = _dot(mn * g1_ref[...], w1_ref[...]).astype(o1_ref.dtype)


def _memkv(mem, g0, w0, g1, w1):
    b = mem.shape[0]
    full = lambda shp: pl.BlockSpec(shp, lambda i: (0,) * len(shp))
    return pl.pallas_call(
        _memkv_kernel,
        grid=(b,),
        in_specs=[pl.BlockSpec((1, N_MEM, D_MODEL), lambda i: (i, 0, 0)),
                  full((1, D_MODEL)), full((D_MODEL, 2 * MEM_WIDTH)),
                  full((1, D_MODEL)), full((D_MODEL, 2 * MEM_WIDTH))],
        out_specs=[pl.BlockSpec((1, N_MEM, 2 * MEM_WIDTH), lambda i: (i, 0, 0))] * 2,
        out_shape=[jax.ShapeDtypeStruct((b, N_MEM, 2 * MEM_WIDTH), BF16)] * 2,
        compiler_params=pltpu.CompilerParams(vmem_limit_bytes=VMEM_LIMIT),
        name="memkv",
    )(mem, g0, w0, g1, w1)


def _inproj0_kernel(x_ref, g_ref, wqkv_ref, wz_ref, wmq_ref, wba_ref,
                    qkv_ref, z_ref, mq_ref, ba_ref):
    x = x_ref[0]
    xn = (x * _rms_scale(x) * g_ref[...]).astype(BF16)
    for w_ref, o_ref in ((wqkv_ref, qkv_ref), (wz_ref, z_ref),
                         (wmq_ref, mq_ref), (wba_ref, ba_ref)):
        o_ref[0] = jnp.dot(xn, w_ref[...], preferred_element_type=F32).astype(o_ref.dtype)


def _inproj0(x, g, wqkv, wz, wmq, wba, tm):
    b, s, _ = x.shape
    widths = (3 * MIX_WIDTH, GATE_WIDTH, MEM_WIDTH, LANES)
    wspec = lambda n: pl.BlockSpec((D_MODEL, n), lambda i, j: (0, 0))
    return pl.pallas_call(
        _inproj0_kernel,
        grid=(b, s // tm),
        in_specs=[pl.BlockSpec((1, tm, D_MODEL), lambda i, j: (i, j, 0)),
                  pl.BlockSpec((1, D_MODEL), lambda i, j: (0, 0))]
                 + [wspec(n) for n in widths],
        out_specs=[pl.BlockSpec((1, tm, n), lambda i, j: (i, j, 0)) for n in widths],
        out_shape=[jax.ShapeDtypeStruct((b, s, n), dt)
                   for n, dt in zip(widths, (BF16, BF16, BF16, F32))],
        compiler_params=pltpu.CompilerParams(vmem_limit_bytes=VMEM_LIMIT),
        name="inproj0",
    )(x, g, wqkv, wz, wmq, wba)


def _inproj1_kernel(x_ref, g_ref, pos_ref, invf_ref, wq_ref, wk_ref, wv_ref, wz_ref, wmq_ref,
                    q_ref, k_ref, v_ref, z_ref, mq_ref):
    x = x_ref[0]
    xn = (x * _rms_scale(x) * g_ref[...]).astype(BF16)
    n_freq = HEAD_DIM // 2
    per_pack = LANES // n_freq
    tm = x.shape[0]
    pos = pos_ref[0].astype(F32)
    lane_p = lax.broadcasted_iota(jnp.int32, (LANES, LANES), 1)
    block = lane_p // n_freq
    cos_parts, sin_parts = [], []
    for base in range(0, tm, per_pack * LANES):
        pos_p = jnp.zeros((LANES, LANES), F32)
        for gidx in range(per_pack):
            rows = slice(base + gidx * LANES, base + (gidx + 1) * LANES)
            pos_p = jnp.where(block == gidx, pos[rows], pos_p)
        ang_p = pos_p * invf_ref[...]
        for trig, parts in ((jnp.cos(ang_p), cos_parts), (jnp.sin(ang_p), sin_parts)):
            for gidx in range(per_pack):
                one = jnp.where(block == gidx, trig, 0.0)
                spread = one
                for k in range(1, per_pack):
                    spread = spread + pltpu.roll(one, k * n_freq, 1)
                parts.append(spread)
    cos = jnp.concatenate(cos_parts, axis=0)
    sin = jnp.concatenate(sin_parts, axis=0)
    lane = lax.broadcasted_iota(jnp.int32, (tm, LANES), 1)
    first_half = (lane & (HEAD_DIM - 1)) < (HEAD_DIM // 2)
    sin_signed = jnp.where(first_half, -sin, sin)
    for w_ref, o_ref in ((wq_ref, q_ref), (wk_ref, k_ref)):
        y = jnp.dot(xn, w_ref[...], preferred_element_type=F32)
        for sl in range(MIX_WIDTH // LANES):
            ys = y[:, sl * LANES:(sl + 1) * LANES]
            rot = jnp.where(first_half,
                            pltpu.roll(ys, LANES - HEAD_DIM // 2, 1),
                            pltpu.roll(ys, HEAD_DIM // 2, 1))
            o_ref[0, :, sl * LANES:(sl + 1) * LANES] = (ys * cos + rot * sin_signed).astype(o_ref.dtype)
    for w_ref, o_ref in ((wv_ref, v_ref), (wz_ref, z_ref), (wmq_ref, mq_ref)):
        o_ref[0] = jnp.dot(xn, w_ref[...], preferred_element_type=F32).astype(o_ref.dtype)


def _inproj1(x, g, pos3, invf, wq, wk, wv, wz, wmq, tm):
    b, s, _ = x.shape
    widths = (MIX_WIDTH, MIX_WIDTH, MIX_WIDTH, GATE_WIDTH, MEM_WIDTH)
    wspec = lambda n: pl.BlockSpec((D_MODEL, n), lambda i, j: (0, 0))
    return pl.pallas_call(
        _inproj1_kernel,
        grid=(b, s // tm),
        in_specs=[pl.BlockSpec((1, tm, D_MODEL), lambda i, j: (i, j, 0)),
                  pl.BlockSpec((1, D_MODEL), lambda i, j: (0, 0)),
                  pl.BlockSpec((1, tm, 1), lambda i, j: (i, j, 0)),
                  pl.BlockSpec((1, LANES), lambda i, j: (0, 0))]
                 + [wspec(n) for n in widths],
        out_specs=[pl.BlockSpec((1, tm, n), lambda i, j: (i, j, 0)) for n in widths],
        out_shape=[jax.ShapeDtypeStruct((b, s, n), BF16) for n in widths],
        compiler_params=pltpu.CompilerParams(vmem_limit_bytes=VMEM_LIMIT),
        name="inproj1",
    )(x, g, pos3, invf, wq, wk, wv, wz, wmq)


def _unit_lower_inverses(lows, row, col, eye, between):
    c = lows[0].shape[0]
    shift = INV_BASE.bit_length() - 1
    base = (row >> shift) == (col >> shift)
    ns = [jnp.where(base, -low, 0.0) for low in lows]
    xs = [eye + n for n in ns]
    ns = [_dot(n, n) for n in ns]
    between()
    power = 2
    while 2 * power < INV_BASE:
        prods = [_dot(jnp.concatenate([n, x], axis=0), n) for x, n in zip(xs, ns)]
        between()
        xs = [x + p[c:] for x, p in zip(xs, prods)]
        ns = [p[:c] for p in prods]
        power *= 2
    xs = [x + _dot(x, n) for x, n in zip(xs, ns)]
    between()
    size = INV_BASE
    while size < c:
        sh = size.bit_length() - 1
        sel = (((row >> (sh + 1)) == (col >> (sh + 1)))
               & (((row >> sh) & 1) == 1) & (((col >> sh) & 1) == 0))
        odd = [slice(r, r + size) for r in range(size, c, 2 * size)]
        x_odd = [jnp.concatenate([x[sl] for sl in odd], axis=0) for x in xs]
        ys = [_dot(xo, jnp.where(sel, low, 0.0)) for xo, low in zip(x_odd, lows)]
        between()
        new_odd = [xo - _dot(y, x) for xo, y, x in zip(x_odd, ys, xs)]
        between()
        xs = [jnp.concatenate(
            [blk for k, sl in enumerate(odd)
             for blk in (x[sl.start - size:sl.start], no[k * size:(k + 1) * size])], axis=0)
            for x, no in zip(xs, new_odd)]
        size *= 2
    return xs


def _delta_kernel(q_ref, k_ref, v_ref, ba_ref, cwq_ref, cwk_ref, cwv_ref,
                  alog_ref, dt_ref, o_ref,
                  ext_ref, p_ref, qp_ref, n_ref, op_ref, egl_ref):
    c = DELTA_CHUNK
    grp = DELTA_GROUP
    rows = grp * c
    seq = q_ref.shape[1]
    pair = pl.program_id(1)
    masks = _head_masks()
    row = lax.broadcasted_iota(jnp.int32, (c, c), 0)
    col = lax.broadcasted_iota(jnp.int32, (c, c), 1)
    tri_incl = row >= col
    tri_strict = row > col
    eye = (row == col).astype(F32)
    tril_b = tri_incl.astype(BF16)
    head_block = ((row >> HEAD_SHIFT) == (col >> HEAD_SHIFT)).astype(F32)
    head_block_b = head_block.astype(BF16)
    row2 = lax.broadcasted_iota(jnp.int32, (LANES, 2 * PAIR), 0)
    col2 = lax.broadcasted_iota(jnp.int32, (LANES, 2 * PAIR), 1)
    expand = (row2 == jnp.where(col2 < PAIR, 0, MIX_HEADS) + 2 * pair + ((col2 >> HEAD_SHIFT) & 1)).astype(BF16)
    lane_n = lax.broadcasted_iota(jnp.int32, (1, LANES), 1)
    is_beta = lane_n < MIX_HEADS
    neg_a = -jnp.exp(alog_ref[...])
    dt_row = dt_ref[...]

    ext_ref[:, 0:SUBLANES, :] = jnp.zeros((3, SUBLANES, PAIR), F32)

    def conv_silu(idx, x_ref, cw_ref, t0):
        raw = x_ref[0, pl.ds(t0, rows), :].astype(F32)
        ext_ref[idx, SUBLANES:SUBLANES + rows, :] = raw
        acc = raw * cw_ref[CONV_WIDTH - 1:CONV_WIDTH, :]
        for tap in range(CONV_WIDTH - 1):
            back = CONV_WIDTH - 1 - tap
            acc = acc + ext_ref[idx, SUBLANES - back:SUBLANES - back + rows, :] * cw_ref[tap:tap + 1, :]
        ext_ref[idx, 0:SUBLANES, :] = raw[rows - SUBLANES:rows, :]
        return _silu(acc)

    def l2norm(x):
        return x * lax.rsqrt(_dot(x * x, head_block_b) + NORM_EPS)

    def build_group(gi, between):
        t0 = pl.multiple_of(gi * rows, rows)
        qn = l2norm(conv_silu(0, q_ref, cwq_ref, t0))
        kn = l2norm(conv_silu(1, k_ref, cwk_ref, t0))
        vc = conv_silu(2, v_ref, cwv_ref, t0)
        ba = ba_ref[0, pl.ds(t0, rows), :]
        act = jnp.where(is_beta, _sigmoid(ba), neg_a * _softplus(ba + dt_row))
        bg = jnp.dot(act.astype(BF16), expand, preferred_element_type=F32)
        beta = bg[:, :PAIR]
        g = bg[:, PAIR:]
        kb = kn * beta
        vb = vc * beta
        sls = [slice(ci * c, (ci + 1) * c) for ci in range(grp)]
        gcs = [jnp.dot(tril_b, g[sl].astype(BF16), preferred_element_type=F32) * LOG2E for sl in sls]
        between()
        gcts = [gc.T for gc in gcs]
        egs = [jnp.exp2(gc) for gc in gcs]
        probs = [(ci, h) for ci in range(grp) for h in range(2)]
        decays = []
        for ci, h in probs:
            gcol = jnp.broadcast_to(gcs[ci][:, h * HEAD_DIM:h * HEAD_DIM + 1], (c, c))
            grow = gcts[ci][h * HEAD_DIM:h * HEAD_DIM + 1, :]
            decays.append(jnp.exp2(jnp.where(tri_incl, gcol - grow, -jnp.inf)))
        kbhs = [kb[sls[ci]] * masks[h] for ci, h in probs]
        kq = [_dot_nt(jnp.concatenate(
            [kbhs[2 * ci], kbhs[2 * ci + 1]]
            + [qn[sls[ci]] * (masks[h] * QK_SCALE) for h in range(2)], axis=0), kn[sls[ci]])
            for ci in range(grp)]
        between()
        lows = [jnp.where(tri_strict, kq[ci][h * c:(h + 1) * c] * d, 0.0)
                for (ci, h), d in zip(probs, decays)]
        aqks = [jnp.where(tri_incl, kq[ci][(2 + h) * c:(3 + h) * c] * d, 0.0)
                for (ci, h), d in zip(probs, decays)]
        tinvs = _unit_lower_inverses(lows, row, col, eye, between)
        wus = [_dot(t, jnp.concatenate([kbh * egs[ci], vb[sls[ci]] * masks[h]], axis=1))
               for t, kbh, (ci, h) in zip(tinvs, kbhs, probs)]
        aus = [_dot(a, wu) for a, wu in zip(aqks, wus)]
        kds = [kn[sls[ci]] * jnp.exp2(gcs[ci][c - 1:c, :] - gcs[ci]) for ci in range(grp)]
        pns = [_dot(kds[ci].T, wus[2 * ci] + wus[2 * ci + 1]) for ci in range(grp)]
        for ci in range(grp):
            cidx = gi * grp + ci
            au = aus[2 * ci] + aus[2 * ci + 1]
            p_ref[cidx] = (head_block * pns[ci][:, :PAIR]).astype(BF16)
            n_ref[cidx] = head_block * pns[ci][:, PAIR:]
            qp_ref[cidx] = (qn[sls[ci]] * (QK_SCALE * egs[ci]) - au[:, :PAIR]).astype(BF16)
            op_ref[cidx] = au[:, PAIR:]
            egl_ref[pl.ds(cidx, 1), :] = jnp.exp2(gcs[ci][c - 1:c, :])

    def scan_step(ci, state):
        sb = state.astype(BF16)
        o = jnp.dot(qp_ref[ci], sb, preferred_element_type=F32) + op_ref[ci]
        new_state = (state * egl_ref[pl.ds(ci, 1), :]
                     - jnp.dot(p_ref[ci], sb, preferred_element_type=F32) + n_ref[ci])
        o_ref[0, pl.ds(pl.multiple_of(ci * c, c), c), :] = o.astype(o_ref.dtype)
        return new_state

    def build_and_scan(gi, state):
        box = [state]
        pending = iter(range(grp))

        def one_scan_step():
            ci = next(pending, None)
            if ci is not None:
                box[0] = scan_step((gi - 1) * grp + ci, box[0])

        build_group(gi, one_scan_step)
        for ci in pending:
            box[0] = scan_step((gi - 1) * grp + ci, box[0])
        return box[0]

    n_groups = seq // rows
    build_group(0, lambda: None)
    state = lax.fori_loop(1, n_groups, build_and_scan, jnp.zeros((PAIR, PAIR), F32))
    lax.fori_loop((n_groups - 1) * grp, n_groups * grp, scan_step, state)


def _delta_mixer(qkv, ba, conv_w, alog_row, dt_row):
    b, s, _ = qkv.shape
    n_chunks = s // DELTA_CHUNK
    seq_spec = lambda off: pl.BlockSpec((1, s, PAIR), lambda i, p, off=off: (i, 0, off + p))
    cw_spec = lambda off: pl.BlockSpec((CONV_WIDTH, PAIR), lambda i, p, off=off: (0, off + p))
    row_spec = pl.BlockSpec((1, LANES), lambda i, p: (0, 0))
    return pl.pallas_call(
        _delta_kernel,
        grid=(b, N_PAIRS),
        in_specs=[seq_spec(0), seq_spec(N_PAIRS), seq_spec(2 * N_PAIRS),
                  pl.BlockSpec((1, s, LANES), lambda i, p: (i, 0, 0)),
                  cw_spec(0), cw_spec(N_PAIRS), cw_spec(2 * N_PAIRS),
                  row_spec, row_spec],
        out_specs=pl.BlockSpec((1, s, PAIR), lambda i, p: (i, 0, p)),
        out_shape=jax.ShapeDtypeStruct((b, s, MIX_WIDTH), BF16),
        scratch_shapes=[pltpu.VMEM((3, SUBLANES + DELTA_GROUP * DELTA_CHUNK, PAIR), F32),
                        pltpu.VMEM((n_chunks, PAIR, PAIR), BF16),
                        pltpu.VMEM((n_chunks, DELTA_CHUNK, PAIR), BF16),
                        pltpu.VMEM((n_chunks, PAIR, PAIR), F32),
                        pltpu.VMEM((n_chunks, DELTA_CHUNK, PAIR), F32),
                        pltpu.VMEM((n_chunks, PAIR), F32)],
        compiler_params=pltpu.CompilerParams(vmem_limit_bytes=VMEM_LIMIT),
        name="delta_mixer",
    )(qkv, qkv, qkv, ba, conv_w, conv_w, conv_w, alog_row, dt_row)


def _moba_kernel(q_ref, k_ref, v_ref, o_ref, kaug_ref, vt_ref, kmean_ref, qaug_ref, sa_ref, sb_ref):
    blk = MOBA_BLOCK
    qw = 2 * blk
    seq = k_ref.shape[1]
    nb = seq // blk
    step = pl.program_id(2)
    masks = _head_masks()
    n_pairs = MOBA_PAIRS
    heads = range(2 * n_pairs)

    @pl.when(step == 0)
    def _prepare():
        lane = lax.broadcasted_iota(jnp.int32, (blk, LANES), 1)
        ones_row = (lax.broadcasted_iota(jnp.int32, (VT_ROWS - HEAD_DIM, LANES), 0) == 0).astype(BF16)

        def key_block(j, carry):
            r0 = pl.multiple_of(j * blk, blk)
            onehot = (lane == j).astype(BF16)
            for pr in range(n_pairs):
                kj = k_ref[0, pl.ds(r0, blk), pr * PAIR:(pr + 1) * PAIR]
                vj = v_ref[0, pl.ds(r0, blk), pr * PAIR:(pr + 1) * PAIR].astype(F32)
                kaug_ref[pr, pl.ds(r0, blk), 0:LANES] = kj
                kaug_ref[pr, pl.ds(r0, blk), LANES:2 * LANES] = onehot
                for t in range(blk // LANES):
                    c0 = pl.multiple_of(r0 + t * LANES, LANES)
                    vt = vj[t * LANES:(t + 1) * LANES, :].T.astype(BF16)
                    for h in range(2):
                        vt_ref[2 * pr + h, 0:HEAD_DIM, pl.ds(c0, LANES)] = vt[h * HEAD_DIM:(h + 1) * HEAD_DIM, :]
                        vt_ref[2 * pr + h, HEAD_DIM:VT_ROWS, pl.ds(c0, LANES)] = ones_row
                kmean_ref[pr, pl.ds(j, 1), :] = jnp.sum(kj.astype(F32), axis=0, keepdims=True) * (1.0 / blk)
            return carry
        kmean_ref[...] = jnp.zeros((n_pairs, NB_PAD, PAIR), F32)
        lax.fori_loop(0, nb, key_block, 0)

        kstacks = []
        for pr in range(n_pairs):
            pieces = []
            for h in range(2):
                rest = kmean_ref[pr] * masks[h]
                for _ in range(3):
                    part = rest.astype(BF16)
                    pieces.append(part)
                    rest = rest - part.astype(F32)
            kstacks.append(jnp.concatenate(pieces, axis=0))
        blk_id = lax.broadcasted_iota(jnp.int32, (NB_PAD, qw), 0)
        blk_f = blk_id.astype(F32)
        half = lax.broadcasted_iota(jnp.int32, (NB_PAD, qw), 1) >> (blk.bit_length() - 1)
        other_head_rows = jnp.zeros((HEAD_DIM, qw), BF16)
        pad_rows = jnp.zeros((PAIR - NB_PAD, qw), BF16)

        def query_blocks(m, carry):
            c0 = pl.multiple_of(m * qw, qw)
            own_blk = 2 * m + half
            valid = blk_id < own_blk
            q_ts, gs, opens = [], [], []
            for pr in range(n_pairs):
                q = q_ref[0, pl.ds(c0, qw), pr * PAIR:(pr + 1) * PAIR]
                qf = q.astype(F32) * (QK_SCALE * LOG2E)
                q_ts.append(jnp.concatenate(
                    [qf[t * LANES:(t + 1) * LANES, :].T for t in range(qw // LANES)], axis=1))
                gate_parts = lax.dot_general(kstacks[pr], q, _NT, preferred_element_type=F32)
                for h in range(2):
                    g3 = [gate_parts[(3 * h + i) * NB_PAD:(3 * h + i + 1) * NB_PAD] for i in range(3)]
                    gs.append(jnp.where(valid, g3[0] + g3[1] + g3[2], -jnp.inf))
                    opens.append((blk_id == own_blk).astype(F32))
            for _ in range(MOBA_TOPK):
                for hh in heads:
                    top = jnp.max(gs[hh], axis=0, keepdims=True)
                    first = jnp.min(jnp.where(gs[hh] == top, blk_f, float(NB_PAD)), axis=0, keepdims=True)
                    pick = (blk_f == first) & (top > -jnp.inf)
                    opens[hh] = jnp.where(pick, 1.0, opens[hh])
                    gs[hh] = jnp.where(pick, -jnp.inf, gs[hh])
            for hh in heads:
                bias = jnp.where(opens[hh] > 0.5, 0.0, MASK_VALUE).astype(BF16)
                own = slice((hh % 2) * HEAD_DIM, (hh % 2 + 1) * HEAD_DIM)
                other = slice((1 - hh % 2) * HEAD_DIM, (2 - hh % 2) * HEAD_DIM)
                qaug_ref[hh, own, pl.ds(c0, qw)] = q_ts[hh // 2][own].astype(BF16)
                qaug_ref[hh, other, pl.ds(c0, qw)] = other_head_rows
                qaug_ref[hh, PAIR:PAIR + NB_PAD, pl.ds(c0, qw)] = bias
                qaug_ref[hh, PAIR + NB_PAD:2 * PAIR, pl.ds(c0, qw)] = pad_rows
            return carry
        lax.fori_loop(0, nb // 2, query_blocks, 0)


    key_pos = lax.broadcasted_iota(jnp.int32, (blk, blk), 0)
    qry_pos = lax.broadcasted_iota(jnp.int32, (blk, blk), 1)
    always = jnp.int32(1 << 20)

    def scores(qi, qs, t, s_ref, where):
        r0 = pl.multiple_of(t * (2 * blk), 2 * blk)
        k2 = [kaug_ref[pr, pl.ds(r0, 2 * blk), :] for pr in range(n_pairs)]
        sts = [jnp.dot(k2[hh // 2], qs[hh], preferred_element_type=F32) for hh in heads]
        maxima = []
        for hh, st in enumerate(sts):
            if where != "past":
                causal = key_pos <= qry_pos + (0 if where == "own" else jnp.where(t == qi, 0, always))
                st = jnp.concatenate(
                    [jnp.concatenate([jnp.where(causal, st[:blk, :blk], MASK_VALUE), st[:blk, blk:]], axis=1),
                     jnp.concatenate([st[blk:, :blk], jnp.where(causal, st[blk:, blk:], MASK_VALUE)], axis=1)],
                    axis=0)
            s_ref[hh] = st
            maxima.append(jnp.max(st, axis=0, keepdims=True))
        return maxima

    def update(t, state, maxima, s_ref):
        r0 = pl.multiple_of(t * (2 * blk), 2 * blk)
        m_new = [jnp.maximum(state[2 * hh], maxima[hh]) for hh in heads]
        ps = [jnp.exp2(s_ref[hh] - m_new[hh]).astype(BF16) for hh in heads]
        pvs = [jnp.dot(vt_ref[hh, :, pl.ds(r0, 2 * blk)], ps[hh], preferred_element_type=F32)
               for hh in heads]
        out = []
        for hh in heads:
            alpha = jnp.exp2(state[2 * hh] - m_new[hh])
            out += [m_new[hh], state[2 * hh + 1] * alpha + pvs[hh]]
        return out

    init = []
    for hh in heads:
        init += [jnp.full((1, qw), MASK_VALUE, F32), jnp.zeros((VT_ROWS, qw), F32)]
    n_state = len(init)

    def write_rows(which, fin):
        for pr in range(n_pairs):
            out_t = jnp.concatenate(
                [fin[2 * hh + 1][:HEAD_DIM] / fin[2 * hh + 1][HEAD_DIM:HEAD_DIM + 1]
                 for hh in (2 * pr, 2 * pr + 1)], axis=0)
            o_ref[0, which * qw:(which + 1) * qw, pr * PAIR:(pr + 1) * PAIR] = out_t.T.astype(o_ref.dtype)

    def query_columns(qi):
        return [qaug_ref[hh, :, pl.ds(pl.multiple_of(qi * qw, qw), qw)] for hh in heads]

    def run_super_block(which, first_max, bufs, start_next):
        qi = MOBA_QSTEP * step + which
        qs = query_columns(qi)
        trips = (MOBA_QSTEP // 2) * step + which // 2
        b0, b1 = bufs
        last_where = "past" if which % 2 else "any"

        def two_pairs(u, carry):
            t0 = 2 * u
            max1 = scores(qi, qs, t0 + 1, b1, "past")
            state = update(t0, carry[:n_state], carry[n_state:], b0)
            max0 = scores(qi, qs, t0 + 2, b0, last_where)
            state = update(t0 + 1, state, max1, b1)
            return tuple(state + max0)

        carry = lax.fori_loop(0, trips, two_pairs, tuple(init + first_max))
        state, max0 = list(carry[:n_state]), carry[n_state:]
        if which % 2 == 0:
            nxt = start_next(b1)
            write_rows(which, update(qi, state, max0, b0))
            return nxt, (b1, b0)
        own = scores(qi, qs, qi, b1, "own")
        state = update(qi - 1, state, max0, b0)
        nxt = start_next(b0)
        write_rows(which, update(qi, state, own, b1))
        return nxt, (b0, b1)

    bufs = (sa_ref, sb_ref)
    q_first = MOBA_QSTEP * step
    first_max = scores(q_first, query_columns(q_first), 0, sa_ref, "any")
    for which in range(MOBA_QSTEP):
        if which + 1 < MOBA_QSTEP:
            q_next = MOBA_QSTEP * step + which + 1
            start_next = lambda s_ref, q_next=q_next: scores(q_next, query_columns(q_next), 0, s_ref, "past")
        else:
            start_next = lambda s_ref: None
        first_max, bufs = run_super_block(which, first_max, bufs, start_next)


def _moba(q, k, v):
    b, s, _ = q.shape
    nb = s // MOBA_BLOCK
    width = MOBA_PAIRS * PAIR
    n_heads = 2 * MOBA_PAIRS
    seq_spec = pl.BlockSpec((1, s, width), lambda i, p, j: (i, 0, p))
    return pl.pallas_call(
        _moba_kernel,
        grid=(b, N_PAIRS // MOBA_PAIRS, nb // (2 * MOBA_QSTEP)),
        in_specs=[seq_spec, seq_spec, seq_spec],
        out_specs=pl.BlockSpec((1, 2 * MOBA_QSTEP * MOBA_BLOCK, width), lambda i, p, j: (i, j, p)),
        out_shape=jax.ShapeDtypeStruct((b, s, MIX_WIDTH), BF16),
        scratch_shapes=[pltpu.VMEM((MOBA_PAIRS, s, 2 * LANES), BF16),
                        pltpu.VMEM((n_heads, VT_ROWS, s), BF16),
                        pltpu.VMEM((MOBA_PAIRS, NB_PAD, PAIR), F32),
                        pltpu.VMEM((n_heads, 2 * LANES, s), BF16),
                        pltpu.VMEM((n_heads, 2 * MOBA_BLOCK, 2 * MOBA_BLOCK), F32),
                        pltpu.VMEM((n_heads, 2 * MOBA_BLOCK, 2 * MOBA_BLOCK), F32)],
        compiler_params=pltpu.CompilerParams(
            dimension_semantics=("arbitrary", "arbitrary", "arbitrary"),
            vmem_limit_bytes=VMEM_LIMIT),
        name="moba",
    )(q, k, v)


def _outproj_kernel(mix_ref, z_ref, mq_ref, mkv_ref, h_ref, wmix_ref, wmem_ref, hg_ref, fg_ref,
                    o_ref, *, head_norm, final_norm):
    mq = mq_ref[0].astype(F32)
    mk = mkv_ref[0, :, :MEM_WIDTH]
    mv = mkv_ref[0, :, MEM_WIDTH:]
    row = lax.broadcasted_iota(jnp.int32, (MEM_WIDTH, MEM_WIDTH), 0)
    col = lax.broadcasted_iota(jnp.int32, (MEM_WIDTH, MEM_WIDTH), 1)
    lane = lax.broadcasted_iota(jnp.int32, (1, MEM_WIDTH), 1)
    mhs = [((lane >> HEAD_SHIFT) == h).astype(F32) for h in range(MEM_HEADS)]
    logits = [_dot_nt(mq * (mh * (QK_SCALE * LOG2E)), mk) for mh in mhs]
    es = [jnp.exp2(lg - jnp.max(lg, axis=-1, keepdims=True)) for lg in logits]
    inv = [1.0 / jnp.sum(e, axis=-1, keepdims=True) for e in es]
    pvs = [jnp.dot(e.astype(BF16), mv[:, (h // 2) * PAIR:(h // 2 + 1) * PAIR], preferred_element_type=F32)
           for h, e in enumerate(es)]
    pair_masks = _head_masks()
    memo = jnp.concatenate(
        [pvs[2 * pp] * (pair_masks[0] * inv[2 * pp]) + pvs[2 * pp + 1] * (pair_masks[1] * inv[2 * pp + 1])
         for pp in range(MEM_HEADS // 2)], axis=1)
    z = z_ref[0].astype(F32)
    gate = _silu(z)
    mix = mix_ref[0].astype(F32)
    if head_norm:
        same_head = ((row >> HEAD_SHIFT) == (col >> HEAD_SHIFT)).astype(BF16)
        slabs = []
        for sl in range(MIX_WIDTH // MEM_WIDTH):
            ms = mix[:, sl * MEM_WIDTH:(sl + 1) * MEM_WIDTH]
            ss = _dot(ms * ms, same_head) * (1.0 / HEAD_DIM)
            slabs.append(ms * lax.rsqrt(ss + NORM_EPS))
        mix = jnp.concatenate(slabs, axis=1) * hg_ref[...]
    y_mix = mix * gate[:, :MIX_WIDTH]
    y_mem = memo * gate[:, MIX_WIDTH:]
    out = h_ref[0] + _dot(y_mix, wmix_ref[...]) + _dot(y_mem, wmem_ref[...])
    if final_norm:
        out = out * _rms_scale(out) * fg_ref[...]
    o_ref[0] = out


def _outproj(mix, z, mq, mkv, h, wmix, wmem, hg, fg, tm, head_norm, final_norm):
    b, s, _ = h.shape
    tile = lambda n: pl.BlockSpec((1, tm, n), lambda i, j: (i, j, 0))
    return pl.pallas_call(
        functools.partial(_outproj_kernel, head_norm=head_norm, final_norm=final_norm),
        grid=(b, s // tm),
        in_specs=[tile(MIX_WIDTH), tile(GATE_WIDTH), tile(MEM_WIDTH),
                  pl.BlockSpec((1, N_MEM, 2 * MEM_WIDTH), lambda i, j: (i, 0, 0)),
                  tile(D_MODEL),
                  pl.BlockSpec((MIX_WIDTH, D_MODEL), lambda i, j: (0, 0)),
                  pl.BlockSpec((MEM_WIDTH, D_MODEL), lambda i, j: (0, 0)),
                  pl.BlockSpec((1, MIX_WIDTH), lambda i, j: (0, 0)),
                  pl.BlockSpec((1, D_MODEL), lambda i, j: (0, 0))],
        out_specs=tile(D_MODEL),
        out_shape=jax.ShapeDtypeStruct((b, s, D_MODEL), F32),
        compiler_params=pltpu.CompilerParams(vmem_limit_bytes=VMEM_LIMIT),
        name="outproj_final" if final_norm else "outproj",
    )(mix, z, mq, mkv, h, wmix, wmem, hg, fg)


def _row(v):
    return v.reshape(1, -1).astype(F32)


def kernel(x, mem, positions, norm_0, w_in_0, conv_w_0, a_log_0, dt_bias_0, o_norm_0,
           mem_norm_0, w_mem_kv_0, w_out_0, norm_1, w_in_1, mem_norm_1, w_mem_kv_1,
           w_out_1, final_norm):
    b, s, _ = x.shape
    assert s % (2 * MOBA_QSTEP * MOBA_BLOCK) == 0 and MOBA_TOPK <= s // MOBA_BLOCK <= NB_PAD
    assert s % (DELTA_GROUP * DELTA_CHUNK) == 0
    tm = 512

    mkv0, mkv1 = _memkv(mem, _row(mem_norm_0), w_mem_kv_0.astype(BF16),
                        _row(mem_norm_1), w_mem_kv_1.astype(BF16))

    i1 = 3 * MIX_WIDTH
    i2 = i1 + GATE_WIDTH
    i3 = i2 + MEM_WIDTH
    w0 = w_in_0.astype(BF16)
    wba = jnp.pad(w0[:, i3:], ((0, 0), (0, LANES - 2 * MIX_HEADS)))
    qkv, z0, mq0, ba = _inproj0(x, _row(norm_0), w0[:, :i1], w0[:, i1:i2], w0[:, i2:i3], wba, tm)
    ba_cols = lambda v: _row(jnp.pad(v, (MIX_HEADS, LANES - 2 * MIX_HEADS)))
    o0 = _delta_mixer(qkv, ba, conv_w_0.astype(F32), ba_cols(a_log_0), ba_cols(dt_bias_0))
    wo0 = w_out_0.astype(BF16)
    h1 = _outproj(o0, z0, mq0, mkv0, x, wo0[:MIX_WIDTH], wo0[MIX_WIDTH:],
                  _row(jnp.tile(o_norm_0, MIX_HEADS)), _row(final_norm),
                  tm, head_norm=True, final_norm=False)

    w1 = w_in_1.astype(BF16)
    half = HEAD_DIM // 2
    inv_freq = ROPE_THETA ** (-jnp.arange(half, dtype=F32) * (2.0 / HEAD_DIM))
    invf = _row(jnp.tile(inv_freq, LANES // half))
    q1, k1, v1, z1, mq1 = _inproj1(
        h1, _row(norm_1), positions.reshape(b, s, 1), invf,
        w1[:, :MIX_WIDTH], w1[:, MIX_WIDTH:2 * MIX_WIDTH], w1[:, 2 * MIX_WIDTH:i1],
        w1[:, i1:i2], w1[:, i2:], tm)
    o1 = _moba(q1, k1, v1)
    wo1 = w_out_1.astype(BF16)
    return _outproj(o1, z1, mq1, mkv1, h1, wo1[:MIX_WIDTH], wo1[MIX_WIDTH:],
                    jnp.ones((1, MIX_WIDTH), F32), _row(final_norm),
                    tm, head_norm=False, final_norm=True)
```

```python
import functools

import jax
import jax.numpy as jnp
from jax import lax
from jax.experimental import pallas as pl
from jax.experimental.pallas import tpu as pltpu

D_MODEL = 1024
HEAD_DIM = 64
HEAD_SHIFT = HEAD_DIM.bit_length() - 1
MIX_HEADS = 12
MEM_HEADS = 4
MIX_WIDTH = MIX_HEADS * HEAD_DIM
MEM_WIDTH = MEM_HEADS * HEAD_DIM
GATE_WIDTH = MIX_WIDTH + MEM_WIDTH
N_MEM = 256
CONV_WIDTH = 4
MOBA_BLOCK = 256
MOBA_TOPK = 3
ROPE_THETA = 10000.0
NORM_EPS = 1e-6
MASK_VALUE = -1e30

LANES = 128
SUBLANES = 8
PAIR = 2 * HEAD_DIM
N_PAIRS = MIX_HEADS // 2
DELTA_CHUNK = 128
DELTA_GROUP = 8
DELTA_PAIRS = 2
INV_BASE = 16
QK_SCALE = HEAD_DIM ** -0.5
LOG2E = 1.4426950408889634
MOBA_QSTEP = 8
MOBA_PAIRS = 2
NB_PAD = 16
VT_ROWS = HEAD_DIM + 16
VMEM_LIMIT = 56 * 1024 * 1024

BF16 = jnp.bfloat16
F32 = jnp.float32

_NT = (((1,), (1,)), ((), ()))


def _dot(a, b):
    return jnp.dot(a.astype(BF16), b.astype(BF16), preferred_element_type=F32)


def _dot_nt(a, b):
    return lax.dot_general(a.astype(BF16), b.astype(BF16), _NT, preferred_element_type=F32)


def _dot_hi(a, b):
    return jnp.dot(a, b, precision=lax.Precision.HIGHEST, preferred_element_type=F32)


def _dot_nt_hi(a, b):
    return lax.dot_general(a, b, _NT, precision=lax.Precision.HIGHEST,
                           preferred_element_type=F32)


def _rms_scale(x):
    return lax.rsqrt(jnp.mean(x * x, axis=-1, keepdims=True) + NORM_EPS)


def _sigmoid(x):
    return 1.0 / (1.0 + jnp.exp(-x))


def _silu(x):
    half = 0.5 * x
    return half + half * jnp.tanh(half)


def _softplus(x):
    return jnp.maximum(x, 0.0) + jnp.log(1.0 + jnp.exp(-jnp.abs(x)))


def _head_masks():
    lane = lax.broadcasted_iota(jnp.int32, (1, PAIR), 1)
    return [((lane >> HEAD_SHIFT) == h).astype(F32) for h in range(2)]


def _memkv_kernel(mem_ref, g0_ref, w0_ref, g1_ref, w1_ref, o0_ref, o1_ref):
    m = mem_ref[0]
    mn = m * _rms_scale(m)
    o0_ref[0] = _dot(mn * g0_ref[...], w0_ref[...]).astype(o0_ref.dtype)
    o1_ref[0] = _dot(mn * g1_ref[...], w1_ref[...]).astype(o1_ref.dtype)


def _memkv(mem, g0, w0, g1, w1):
    b = mem.shape[0]
    full = lambda shp: pl.BlockSpec(shp, lambda i: (0,) * len(shp))
    return pl.pallas_call(
        _memkv_kernel,
        grid=(b,),
        in_specs=[pl.BlockSpec((1, N_MEM, D_MODEL), lambda i: (i, 0, 0)),
                  full((1, D_MODEL)), full((D_MODEL, 2 * MEM_WIDTH)),
                  full((1, D_MODEL)), full((D_MODEL, 2 * MEM_WIDTH))],
        out_specs=[pl.BlockSpec((1, N_MEM, 2 * MEM_WIDTH), lambda i: (i, 0, 0))] * 2,
        out_shape=[jax.ShapeDtypeStruct((b, N_MEM, 2 * MEM_WIDTH), BF16)] * 2,
        compiler_params=pltpu.CompilerParams(vmem_limit_bytes=VMEM_LIMIT),
        name="memkv",
    )(mem, g0, w0, g1, w1)


def _inproj0_kernel(x_ref, g_ref, wqkv_ref, wz_ref, wmq_ref, wba_ref,
                    qkv_ref, z_ref, mq_ref, ba_ref):
    x = x_ref[0]
    xn = (x * _rms_scale(x) * g_ref[...]).astype(BF16)
    for w_ref, o_ref in ((wqkv_ref, qkv_ref), (wz_ref, z_ref),
                         (wmq_ref, mq_ref), (wba_ref, ba_ref)):
        o_ref[0] = jnp.dot(xn, w_ref[...], preferred_element_type=F32).astype(o_ref.dtype)


def _inproj0(x, g, wqkv, wz, wmq, wba, tm):
    b, s, _ = x.shape
    widths = (3 * MIX_WIDTH, GATE_WIDTH, MEM_WIDTH, LANES)
    wspec = lambda n: pl.BlockSpec((D_MODEL, n), lambda i, j: (0, 0))
    return pl.pallas_call(
        _inproj0_kernel,
        grid=(b, s // tm),
        in_specs=[pl.BlockSpec((1, tm, D_MODEL), lambda i, j: (i, j, 0)),
                  pl.BlockSpec((1, D_MODEL), lambda i, j: (0, 0))]
                 + [wspec(n) for n in widths],
        out_specs=[pl.BlockSpec((1, tm, n), lambda i, j: (i, j, 0)) for n in widths],
        out_shape=[jax.ShapeDtypeStruct((b, s, n), dt)
                   for n, dt in zip(widths, (BF16, BF16, BF16, F32))],
        compiler_params=pltpu.CompilerParams(vmem_limit_bytes=VMEM_LIMIT),
        name="inproj0",
    )(x, g, wqkv, wz, wmq, wba)


def _inproj1_kernel(x_ref, g_ref, pos_ref, invf_ref, wq_ref, wk_ref, wv_ref, wz_ref, wmq_ref,
                    q_ref, k_ref, v_ref, z_ref, mq_ref):
    x = x_ref[0]
    xn = (x * _rms_scale(x) * g_ref[...]).astype(BF16)
    n_freq = HEAD_DIM // 2
    per_pack = LANES // n_freq
    tm = x.shape[0]
    pos = pos_ref[0].astype(F32)
    lane_p = lax.broadcasted_iota(jnp.int32, (LANES, LANES), 1)
    block = lane_p // n_freq
    cos_parts, sin_parts = [], []
    for base in range(0, tm, per_pack * LANES):
        pos_p = jnp.zeros((LANES, LANES), F32)
        for gidx in range(per_pack):
            rows = slice(base + gidx * LANES, base + (gidx + 1) * LANES)
            pos_p = jnp.where(block == gidx, pos[rows], pos_p)
        ang_p = pos_p * invf_ref[...]
        for trig, parts in ((jnp.cos(ang_p), cos_parts), (jnp.sin(ang_p), sin_parts)):
            for gidx in range(per_pack):
                one = jnp.where(block == gidx, trig, 0.0)
                spread = one
                for k in range(1, per_pack):
                    spread = spread + pltpu.roll(one, k * n_freq, 1)
                parts.append(spread)
    cos = jnp.concatenate(cos_parts, axis=0)
    sin = jnp.concatenate(sin_parts, axis=0)
    lane = lax.broadcasted_iota(jnp.int32, (tm, LANES), 1)
    first_half = (lane & (HEAD_DIM - 1)) < (HEAD_DIM // 2)
    sin_signed = jnp.where(first_half, -sin, sin)
    for w_ref, o_ref in ((wq_ref, q_ref), (wk_ref, k_ref)):
        y = jnp.dot(xn, w_ref[...], preferred_element_type=F32)
        for sl in range(MIX_WIDTH // LANES):
            ys = y[:, sl * LANES:(sl + 1) * LANES]
            rot = jnp.where(first_half,
                            pltpu.roll(ys, LANES - HEAD_DIM // 2, 1),
                            pltpu.roll(ys, HEAD_DIM // 2, 1))
            o_ref[0, :, sl * LANES:(sl + 1) * LANES] = (ys * cos + rot * sin_signed).astype(o_ref.dtype)
    for w_ref, o_ref in ((wv_ref, v_ref), (wz_ref, z_ref), (wmq_ref, mq_ref)):
        o_ref[0] = jnp.dot(xn, w_ref[...], preferred_element_type=F32).astype(o_ref.dtype)


def _inproj1(x, g, pos3, invf, wq, wk, wv, wz, wmq, tm):
    b, s, _ = x.shape
    widths = (MIX_WIDTH, MIX_WIDTH, MIX_WIDTH, GATE_WIDTH, MEM_WIDTH)
    wspec = lambda n: pl.BlockSpec((D_MODEL, n), lambda i, j: (0, 0))
    return pl.pallas_call(
        _inproj1_kernel,
        grid=(b, s // tm),
        in_specs=[pl.BlockSpec((1, tm, D_MODEL), lambda i, j: (i, j, 0)),
                  pl.BlockSpec((1, D_MODEL), lambda i, j: (0, 0)),
                  pl.BlockSpec((1, tm, 1), lambda i, j: (i, j, 0)),
                  pl.BlockSpec((1, LANES), lambda i, j: (0, 0))]
                 + [wspec(n) for n in widths],
        out_specs=[pl.BlockSpec((1, tm, n), lambda i, j: (i, j, 0)) for n in widths],
        out_shape=[jax.ShapeDtypeStruct((b, s, n), BF16) for n in widths],
        compiler_params=pltpu.CompilerParams(vmem_limit_bytes=VMEM_LIMIT),
        name="inproj1",
    )(x, g, pos3, invf, wq, wk, wv, wz, wmq)


def _unit_lower_inverses(lows, row, col, eye, between):
    c = lows[0].shape[0]
    shift = INV_BASE.bit_length() - 1
    base = (row >> shift) == (col >> shift)
    ns = [jnp.where(base, -low, 0.0) for low in lows]
    xs = [eye + n for n in ns]
    ns = [_dot(n, n) for n in ns]
    between()
    power = 2
    while 2 * power < INV_BASE:
        prods = [_dot(jnp.concatenate([n, x], axis=0), n) for x, n in zip(xs, ns)]
        between()
        xs = [x + p[c:] for x, p in zip(xs, prods)]
        ns = [p[:c] for p in prods]
        power *= 2
    xs = [x + _dot(x, n) for x, n in zip(xs, ns)]
    between()
    size = INV_BASE
    while size < c:
        sh = size.bit_length() - 1
        sel = (((row >> (sh + 1)) == (col >> (sh + 1)))
               & (((row >> sh) & 1) == 1) & (((col >> sh) & 1) == 0))
        odd = [slice(r, r + size) for r in range(size, c, 2 * size)]
        x_odd = [jnp.concatenate([x[sl] for sl in odd], axis=0) for x in xs]
        ys = [_dot(xo, jnp.where(sel, low, 0.0)) for xo, low in zip(x_odd, lows)]
        between()
        new_odd = [xo - _dot(y, x) for xo, y, x in zip(x_odd, ys, xs)]
        between()
        xs = [jnp.concatenate(
            [blk for k, sl in enumerate(odd)
             for blk in (x[sl.start - size:sl.start], no[k * size:(k + 1) * size])], axis=0)
            for x, no in zip(xs, new_odd)]
        size *= 2
    return xs


def _delta_kernel(q_ref, k_ref, v_ref, ba_ref, cwq_ref, cwk_ref, cwv_ref,
                  alog_ref, dt_ref, o_ref,
                  ext_ref, p_ref, qp_ref, n_ref, op_ref, egl_ref):
    c = DELTA_CHUNK
    grp = DELTA_GROUP
    rows = grp * c
    seq = q_ref.shape[1]
    first_pair = DELTA_PAIRS * pl.program_id(1)
    masks = _head_masks()
    row = lax.broadcasted_iota(jnp.int32, (c, c), 0)
    col = lax.broadcasted_iota(jnp.int32, (c, c), 1)
    tri_incl = row >= col
    tri_strict = row > col
    eye = (row == col).astype(F32)
    tril_b = tri_incl.astype(BF16)
    head_block = ((row >> HEAD_SHIFT) == (col >> HEAD_SHIFT)).astype(F32)
    head_block_b = head_block.astype(BF16)
    row2 = lax.broadcasted_iota(jnp.int32, (LANES, 2 * PAIR), 0)
    col2 = lax.broadcasted_iota(jnp.int32, (LANES, 2 * PAIR), 1)
    expands = [(row2 == jnp.where(col2 < PAIR, 0, MIX_HEADS) + 2 * (first_pair + pp)
                + ((col2 >> HEAD_SHIFT) & 1)).astype(BF16) for pp in range(DELTA_PAIRS)]
    lane_n = lax.broadcasted_iota(jnp.int32, (1, LANES), 1)
    is_beta = lane_n < MIX_HEADS
    neg_a = -jnp.exp(alog_ref[...])
    dt_row = dt_ref[...]

    def conv_silu(idx, x_ref, cw_ref, t0, lanes):
        raw = x_ref[0, pl.ds(t0, rows), lanes].astype(F32)
        ext_ref[idx, SUBLANES:SUBLANES + rows, :] = raw
        acc = raw * cw_ref[CONV_WIDTH - 1:CONV_WIDTH, lanes]
        for tap in range(CONV_WIDTH - 1):
            back = CONV_WIDTH - 1 - tap
            acc = acc + ext_ref[idx, SUBLANES - back:SUBLANES - back + rows, :] * cw_ref[tap:tap + 1, lanes]
        ext_ref[idx, 0:SUBLANES, :] = raw[rows - SUBLANES:rows, :]
        return _silu(acc)

    def l2norm(x):
        return x * lax.rsqrt(_dot(x * x, head_block_b) + NORM_EPS)

    def build_group(pp, gi, between):
        lanes = slice(pp * PAIR, (pp + 1) * PAIR)
        t0 = pl.multiple_of(gi * rows, rows)
        qn = l2norm(conv_silu(0, q_ref, cwq_ref, t0, lanes))
        kn = l2norm(conv_silu(1, k_ref, cwk_ref, t0, lanes))
        vc = conv_silu(2, v_ref, cwv_ref, t0, lanes)
        ba = ba_ref[0, pl.ds(t0, rows), :]
        act = jnp.where(is_beta, _sigmoid(ba), neg_a * _softplus(ba + dt_row))
        bg = jnp.dot(act.astype(BF16), expands[pp], preferred_element_type=F32)
        beta = bg[:, :PAIR]
        g = bg[:, PAIR:]
        kb = kn * beta
        vb = vc * beta
        sls = [slice(ci * c, (ci + 1) * c) for ci in range(grp)]
        gcs = [jnp.dot(tril_b, g[sl].astype(BF16), preferred_element_type=F32) * LOG2E for sl in sls]
        between()
        gcts = [gc.T for gc in gcs]
        egs = [jnp.exp2(gc) for gc in gcs]
        probs = [(ci, h) for ci in range(grp) for h in range(2)]
        decays = []
        for ci, h in probs:
            gcol = jnp.broadcast_to(gcs[ci][:, h * HEAD_DIM:h * HEAD_DIM + 1], (c, c))
            grow = gcts[ci][h * HEAD_DIM:h * HEAD_DIM + 1, :]
            decays.append(jnp.exp2(jnp.where(tri_incl, gcol - grow, -jnp.inf)))
        kbhs = [kb[sls[ci]] * masks[h] for ci, h in probs]
        kq = [_dot_nt(jnp.concatenate(
            [kbhs[2 * ci], kbhs[2 * ci + 1]]
            + [qn[sls[ci]] * (masks[h] * QK_SCALE) for h in range(2)], axis=0), kn[sls[ci]])
            for ci in range(grp)]
        between()
        lows = [jnp.where(tri_strict, kq[ci][h * c:(h + 1) * c] * d, 0.0)
                for (ci, h), d in zip(probs, decays)]
        aqks = [jnp.where(tri_incl, kq[ci][(2 + h) * c:(3 + h) * c] * d, 0.0)
                for (ci, h), d in zip(probs, decays)]
        tinvs = _unit_lower_inverses(lows, row, col, eye, between)
        wus = [_dot(t, jnp.concatenate([kbh * egs[ci], vb[sls[ci]] * masks[h]], axis=1))
               for t, kbh, (ci, h) in zip(tinvs, kbhs, probs)]
        aus = [_dot(a, wu) for a, wu in zip(aqks, wus)]
        kds = [kn[sls[ci]] * jnp.exp2(gcs[ci][c - 1:c, :] - gcs[ci]) for ci in range(grp)]
        pns = [_dot(kds[ci].T, wus[2 * ci] + wus[2 * ci + 1]) for ci in range(grp)]
        for ci in range(grp):
            cidx = gi * grp + ci
            au = aus[2 * ci] + aus[2 * ci + 1]
            p_ref[cidx] = (head_block * pns[ci][:, :PAIR]).astype(BF16)
            n_ref[cidx] = head_block * pns[ci][:, PAIR:]
            qp_ref[cidx] = (qn[sls[ci]] * (QK_SCALE * egs[ci]) - au[:, :PAIR]).astype(BF16)
            op_ref[cidx] = au[:, PAIR:]
            egl_ref[pl.ds(cidx, 1), :] = jnp.exp2(gcs[ci][c - 1:c, :])

    def scan_step(pp, ci, state):
        sb = state.astype(BF16)
        o = jnp.dot(qp_ref[ci], sb, preferred_element_type=F32) + op_ref[ci]
        new_state = (state * egl_ref[pl.ds(ci, 1), :]
                     - jnp.dot(p_ref[ci], sb, preferred_element_type=F32) + n_ref[ci])
        o_ref[0, pl.ds(pl.multiple_of(ci * c, c), c), pp * PAIR:(pp + 1) * PAIR] = o.astype(o_ref.dtype)
        return new_state

    def build_and_scan(build_pp, build_gi, scan_pp, scan_gi, state):
        box = [state]
        pending = iter(range(grp))

        def one_scan_step():
            ci = next(pending, None)
            if ci is not None:
                box[0] = scan_step(scan_pp, scan_gi * grp + ci, box[0])

        build_group(build_pp, build_gi, one_scan_step)
        for ci in pending:
            box[0] = scan_step(scan_pp, scan_gi * grp + ci, box[0])
        return box[0]

    n_groups = seq // rows
    zero_state = jnp.zeros((PAIR, PAIR), F32)
    zero_halo = jnp.zeros((3, SUBLANES, PAIR), F32)
    state = zero_state
    for pp in range(DELTA_PAIRS):
        ext_ref[:, 0:SUBLANES, :] = zero_halo
        if pp == 0:
            build_group(0, 0, lambda: None)
        else:
            build_and_scan(pp, 0, pp - 1, n_groups - 1, state)
        state = lax.fori_loop(
            1, n_groups, lambda gi, st, pp=pp: build_and_scan(pp, gi, pp, gi - 1, st), zero_state)
    last = DELTA_PAIRS - 1
    lax.fori_loop((n_groups - 1) * grp, n_groups * grp, lambda ci, st: scan_step(last, ci, st), state)


def _delta_mixer(qkv, ba, conv_w, alog_row, dt_row):
    b, s, _ = qkv.shape
    n_chunks = s // DELTA_CHUNK
    width = DELTA_PAIRS * PAIR
    n_steps = N_PAIRS // DELTA_PAIRS
    seq_spec = lambda off: pl.BlockSpec((1, s, width), lambda i, p, off=off: (i, 0, off + p))
    cw_spec = lambda off: pl.BlockSpec((CONV_WIDTH, width), lambda i, p, off=off: (0, off + p))
    row_spec = pl.BlockSpec((1, LANES), lambda i, p: (0, 0))
    return pl.pallas_call(
        _delta_kernel,
        grid=(b, n_steps),
        in_specs=[seq_spec(0), seq_spec(n_steps), seq_spec(2 * n_steps),
                  pl.BlockSpec((1, s, LANES), lambda i, p: (i, 0, 0)),
                  cw_spec(0), cw_spec(n_steps), cw_spec(2 * n_steps),
                  row_spec, row_spec],
        out_specs=pl.BlockSpec((1, s, width), lambda i, p: (i, 0, p)),
        out_shape=jax.ShapeDtypeStruct((b, s, MIX_WIDTH), BF16),
        scratch_shapes=[pltpu.VMEM((3, SUBLANES + DELTA_GROUP * DELTA_CHUNK, PAIR), F32),
                        pltpu.VMEM((n_chunks, PAIR, PAIR), BF16),
                        pltpu.VMEM((n_chunks, DELTA_CHUNK, PAIR), BF16),
                        pltpu.VMEM((n_chunks, PAIR, PAIR), F32),
                        pltpu.VMEM((n_chunks, DELTA_CHUNK, PAIR), F32),
                        pltpu.VMEM((n_chunks, PAIR), F32)],
        compiler_params=pltpu.CompilerParams(vmem_limit_bytes=VMEM_LIMIT),
        name="delta_mixer",
    )(qkv, qkv, qkv, ba, conv_w, conv_w, conv_w, alog_row, dt_row)


def _moba_kernel(q_ref, k_ref, v_ref, o_ref, kaug_ref, vt_ref, kmean_ref, qaug_ref, sa_ref, sb_ref):
    blk = MOBA_BLOCK
    qw = 2 * blk
    seq = k_ref.shape[1]
    nb = seq // blk
    step = pl.program_id(2)
    masks = _head_masks()
    n_pairs = MOBA_PAIRS
    heads = range(2 * n_pairs)

    @pl.when(step == 0)
    def _prepare():
        lane = lax.broadcasted_iota(jnp.int32, (blk, LANES), 1)
        ones_row = (lax.broadcasted_iota(jnp.int32, (VT_ROWS - HEAD_DIM, LANES), 0) == 0).astype(BF16)

        def key_block(j, carry):
            r0 = pl.multiple_of(j * blk, blk)
            onehot = (lane == j).astype(BF16)
            for pr in range(n_pairs):
                kj = k_ref[0, pl.ds(r0, blk), pr * PAIR:(pr + 1) * PAIR]
                vj = v_ref[0, pl.ds(r0, blk), pr * PAIR:(pr + 1) * PAIR].astype(F32)
                kaug_ref[pr, pl.ds(r0, blk), 0:LANES] = kj
                kaug_ref[pr, pl.ds(r0, blk), LANES:2 * LANES] = onehot
                for t in range(blk // LANES):
                    c0 = pl.multiple_of(r0 + t * LANES, LANES)
                    vt = vj[t * LANES:(t + 1) * LANES, :].T.astype(BF16)
                    for h in range(2):
                        vt_ref[2 * pr + h, 0:HEAD_DIM, pl.ds(c0, LANES)] = vt[h * HEAD_DIM:(h + 1) * HEAD_DIM, :]
                        vt_ref[2 * pr + h, HEAD_DIM:VT_ROWS, pl.ds(c0, LANES)] = ones_row
                kmean_ref[pr, pl.ds(j, 1), :] = jnp.sum(kj.astype(F32), axis=0, keepdims=True) * (1.0 / blk)
            return carry
        kmean_ref[...] = jnp.zeros((n_pairs, NB_PAD, PAIR), F32)
        lax.fori_loop(0, nb, key_block, 0)

        kstacks = []
        for pr in range(n_pairs):
            pieces = []
            for h in range(2):
                rest = kmean_ref[pr] * masks[h]
                for _ in range(3):
                    part = rest.astype(BF16)
                    pieces.append(part)
                    rest = rest - part.astype(F32)
            kstacks.append(jnp.concatenate(pieces, axis=0))
        blk_id = lax.broadcasted_iota(jnp.int32, (NB_PAD, qw), 0)
        blk_f = blk_id.astype(F32)
        half = lax.broadcasted_iota(jnp.int32, (NB_PAD, qw), 1) >> (blk.bit_length() - 1)
        other_head_rows = jnp.zeros((HEAD_DIM, qw), BF16)
        pad_rows = jnp.zeros((PAIR - NB_PAD, qw), BF16)

        def query_blocks(m, carry):
            c0 = pl.multiple_of(m * qw, qw)
            own_blk = 2 * m + half
            valid = blk_id < own_blk
            q_ts, gs, opens = [], [], []
            for pr in range(n_pairs):
                q = q_ref[0, pl.ds(c0, qw), pr * PAIR:(pr + 1) * PAIR]
                qf = q.astype(F32) * (QK_SCALE * LOG2E)
                q_ts.append(jnp.concatenate(
                    [qf[t * LANES:(t + 1) * LANES, :].T for t in range(qw // LANES)], axis=1))
                gate_parts = lax.dot_general(kstacks[pr], q, _NT, preferred_element_type=F32)
                for h in range(2):
                    g3 = [gate_parts[(3 * h + i) * NB_PAD:(3 * h + i + 1) * NB_PAD] for i in range(3)]
                    gs.append(jnp.where(valid, g3[0] + g3[1] + g3[2], -jnp.inf))
                    opens.append((blk_id == own_blk).astype(F32))
            for _ in range(MOBA_TOPK):
                for hh in heads:
                    top = jnp.max(gs[hh], axis=0, keepdims=True)
                    first = jnp.min(jnp.where(gs[hh] == top, blk_f, float(NB_PAD)), axis=0, keepdims=True)
                    pick = (blk_f == first) & (top > -jnp.inf)
                    opens[hh] = jnp.where(pick, 1.0, opens[hh])
                    gs[hh] = jnp.where(pick, -jnp.inf, gs[hh])
            for hh in heads:
                bias = jnp.where(opens[hh] > 0.5, 0.0, MASK_VALUE).astype(BF16)
                own = slice((hh % 2) * HEAD_DIM, (hh % 2 + 1) * HEAD_DIM)
                other = slice((1 - hh % 2) * HEAD_DIM, (2 - hh % 2) * HEAD_DIM)
                qaug_ref[hh, own, pl.ds(c0, qw)] = q_ts[hh // 2][own].astype(BF16)
                qaug_ref[hh, other, pl.ds(c0, qw)] = other_head_rows
                qaug_ref[hh, PAIR:PAIR + NB_PAD, pl.ds(c0, qw)] = bias
                qaug_ref[hh, PAIR + NB_PAD:2 * PAIR, pl.ds(c0, qw)] = pad_rows
            return carry
        lax.fori_loop(0, nb // 2, query_blocks, 0)


    key_pos = lax.broadcasted_iota(jnp.int32, (blk, blk), 0)
    qry_pos = lax.broadcasted_iota(jnp.int32, (blk, blk), 1)
    always = jnp.int32(1 << 20)

    def scores(qi, qs, t, s_ref, where):
        r0 = pl.multiple_of(t * (2 * blk), 2 * blk)
        k2 = [kaug_ref[pr, pl.ds(r0, 2 * blk), :] for pr in range(n_pairs)]
        sts = [jnp.dot(k2[hh // 2], qs[hh], preferred_element_type=F32) for hh in heads]
        maxima = []
        for hh, st in enumerate(sts):
            if where != "past":
                causal = key_pos <= qry_pos + (0 if where == "own" else jnp.where(t == qi, 0, always))
                st = jnp.concatenate(
                    [jnp.concatenate([jnp.where(causal, st[:blk, :blk], MASK_VALUE), st[:blk, blk:]], axis=1),
                     jnp.concatenate([st[blk:, :blk], jnp.where(causal, st[blk:, blk:], MASK_VALUE)], axis=1)],
                    axis=0)
            s_ref[hh] = st
            maxima.append(jnp.max(st, axis=0, keepdims=True))
        return maxima

    def update(t, state, maxima, s_ref):
        r0 = pl.multiple_of(t * (2 * blk), 2 * blk)
        m_new = [jnp.maximum(state[2 * hh], maxima[hh]) for hh in heads]
        ps = [jnp.exp2(s_ref[hh] - m_new[hh]).astype(BF16) for hh in heads]
        pvs = [jnp.dot(vt_ref[hh, :, pl.ds(r0, 2 * blk)], ps[hh], preferred_element_type=F32)
               for hh in heads]
        out = []
        for hh in heads:
            alpha = jnp.exp2(state[2 * hh] - m_new[hh])
            out += [m_new[hh], state[2 * hh + 1] * alpha + pvs[hh]]
        return out

    init = []
    for hh in heads:
        init += [jnp.full((1, qw), MASK_VALUE, F32), jnp.zeros((VT_ROWS, qw), F32)]
    n_state = len(init)

    def write_rows(which, fin):
        for pr in range(n_pairs):
            out_t = jnp.concatenate(
                [fin[2 * hh + 1][:HEAD_DIM] / fin[2 * hh + 1][HEAD_DIM:HEAD_DIM + 1]
                 for hh in (2 * pr, 2 * pr + 1)], axis=0)
            o_ref[0, which * qw:(which + 1) * qw, pr * PAIR:(pr + 1) * PAIR] = out_t.T.astype(o_ref.dtype)

    def query_columns(qi):
        return [qaug_ref[hh, :, pl.ds(pl.multiple_of(qi * qw, qw), qw)] for hh in heads]

    def run_super_block(which, first_max, bufs, start_next):
        qi = MOBA_QSTEP * step + which
        qs = query_columns(qi)
        trips = (MOBA_QSTEP // 2) * step + which // 2
        b0, b1 = bufs
        last_where = "past" if which % 2 else "any"

        def two_pairs(u, carry):
            t0 = 2 * u
            max1 = scores(qi, qs, t0 + 1, b1, "past")
            state = update(t0, carry[:n_state], carry[n_state:], b0)
            max0 = scores(qi, qs, t0 + 2, b0, last_where)
            state = update(t0 + 1, state, max1, b1)
            return tuple(state + max0)

        carry = lax.fori_loop(0, trips, two_pairs, tuple(init + first_max))
        state, max0 = list(carry[:n_state]), carry[n_state:]
        if which % 2 == 0:
            nxt = start_next(b1)
            write_rows(which, update(qi, state, max0, b0))
            return nxt, (b1, b0)
        own = scores(qi, qs, qi, b1, "own")
        state = update(qi - 1, state, max0, b0)
        nxt = start_next(b0)
        write_rows(which, update(qi, state, own, b1))
        return nxt, (b0, b1)

    bufs = (sa_ref, sb_ref)
    q_first = MOBA_QSTEP * step
    first_max = scores(q_first, query_columns(q_first), 0, sa_ref, "any")
    for which in range(MOBA_QSTEP):
        if which + 1 < MOBA_QSTEP:
            q_next = MOBA_QSTEP * step + which + 1
            start_next = lambda s_ref, q_next=q_next: scores(q_next, query_columns(q_next), 0, s_ref, "past")
        else:
            start_next = lambda s_ref: None
        first_max, bufs = run_super_block(which, first_max, bufs, start_next)


def _moba(q, k, v):
    b, s, _ = q.shape
    nb = s // MOBA_BLOCK
    width = MOBA_PAIRS * PAIR
    n_heads = 2 * MOBA_PAIRS
    seq_spec = pl.BlockSpec((1, s, width), lambda i, p, j: (i, 0, p))
    return pl.pallas_call(
        _moba_kernel,
        grid=(b, N_PAIRS // MOBA_PAIRS, nb // (2 * MOBA_QSTEP)),
        in_specs=[seq_spec, seq_spec, seq_spec],
        out_specs=pl.BlockSpec((1, 2 * MOBA_QSTEP * MOBA_BLOCK, width), lambda i, p, j: (i, j, p)),
        out_shape=jax.ShapeDtypeStruct((b, s, MIX_WIDTH), BF16),
        scratch_shapes=[pltpu.VMEM((MOBA_PAIRS, s, 2 * LANES), BF16),
                        pltpu.VMEM((n_heads, VT_ROWS, s), BF16),
                        pltpu.VMEM((MOBA_PAIRS, NB_PAD, PAIR), F32),
                        pltpu.VMEM((n_heads, 2 * LANES, s), BF16),
                        pltpu.VMEM((n_heads, 2 * MOBA_BLOCK, 2 * MOBA_BLOCK), F32),
                        pltpu.VMEM((n_heads, 2 * MOBA_BLOCK, 2 * MOBA_BLOCK), F32)],
        compiler_params=pltpu.CompilerParams(
            dimension_semantics=("arbitrary", "arbitrary", "arbitrary"),
            vmem_limit_bytes=VMEM_LIMIT),
        name="moba",
    )(q, k, v)


def _outproj_kernel(mix_ref, z_ref, mq_ref, mkv_ref, h_ref, wmix_ref, wmem_ref, hg_ref, fg_ref,
                    o_ref, *, head_norm, final_norm):
    mq = mq_ref[0].astype(F32)
    mk = mkv_ref[0, :, :MEM_WIDTH]
    mv = mkv_ref[0, :, MEM_WIDTH:]
    row = lax.broadcasted_iota(jnp.int32, (MEM_WIDTH, MEM_WIDTH), 0)
    col = lax.broadcasted_iota(jnp.int32, (MEM_WIDTH, MEM_WIDTH), 1)
    lane = lax.broadcasted_iota(jnp.int32, (1, MEM_WIDTH), 1)
    mhs = [((lane >> HEAD_SHIFT) == h).astype(F32) for h in range(MEM_HEADS)]
    logits = [_dot_nt(mq * (mh * (QK_SCALE * LOG2E)), mk) for mh in mhs]
    es = [jnp.exp2(lg - jnp.max(lg, axis=-1, keepdims=True)) for lg in logits]
    inv = [1.0 / jnp.sum(e, axis=-1, keepdims=True) for e in es]
    pvs = [jnp.dot(e.astype(BF16), mv[:, (h // 2) * PAIR:(h // 2 + 1) * PAIR], preferred_element_type=F32)
           for h, e in enumerate(es)]
    pair_masks = _head_masks()
    memo = jnp.concatenate(
        [pvs[2 * pp] * (pair_masks[0] * inv[2 * pp]) + pvs[2 * pp + 1] * (pair_masks[1] * inv[2 * pp + 1])
         for pp in range(MEM_HEADS // 2)], axis=1)
    z = z_ref[0].astype(F32)
    gate = _silu(z)
    mix = mix_ref[0].astype(F32)
    if head_norm:
        same_head = ((row >> HEAD_SHIFT) == (col >> HEAD_SHIFT)).astype(BF16)
        slabs = []
        for sl in range(MIX_WIDTH // MEM_WIDTH):
            ms = mix[:, sl * MEM_WIDTH:(sl + 1) * MEM_WIDTH]
            ss = _dot(ms * ms, same_head) * (1.0 / HEAD_DIM)
            slabs.append(ms * lax.rsqrt(ss + NORM_EPS))
        mix = jnp.concatenate(slabs, axis=1) * hg_ref[...]
    y_mix = mix * gate[:, :MIX_WIDTH]
    y_mem = memo * gate[:, MIX_WIDTH:]
    out = h_ref[0] + _dot(y_mix, wmix_ref[...]) + _dot(y_mem, wmem_ref[...])
    if final_norm:
        out = out * _rms_scale(out) * fg_ref[...]
    o_ref[0] = out


def _outproj(mix, z, mq, mkv, h, wmix, wmem, hg, fg, tm, head_norm, final_norm):
    b, s, _ = h.shape
    tile = lambda n: pl.BlockSpec((1, tm, n), lambda i, j: (i, j, 0))
    return pl.pallas_call(
        functools.partial(_outproj_kernel, head_norm=head_norm, final_norm=final_norm),
        grid=(b, s // tm),
        in_specs=[tile(MIX_WIDTH), tile(GATE_WIDTH), tile(MEM_WIDTH),
                  pl.BlockSpec((1, N_MEM, 2 * MEM_WIDTH), lambda i, j: (i, 0, 0)),
                  tile(D_MODEL),
                  pl.BlockSpec((MIX_WIDTH, D_MODEL), lambda i, j: (0, 0)),
                  pl.BlockSpec((MEM_WIDTH, D_MODEL), lambda i, j: (0, 0)),
                  pl.BlockSpec((1, MIX_WIDTH), lambda i, j: (0, 0)),
                  pl.BlockSpec((1, D_MODEL), lambda i, j: (0, 0))],
        out_specs=tile(D_MODEL),
        out_shape=jax.ShapeDtypeStruct((b, s, D_MODEL), F32),
        compiler_params=pltpu.CompilerParams(vmem_limit_bytes=VMEM_LIMIT),
        name="outproj_final" if final_norm else "outproj",
    )(mix, z, mq, mkv, h, wmix, wmem, hg, fg)


def _row(v):
    return v.reshape(1, -1).astype(F32)


def kernel(x, mem, positions, norm_0, w_in_0, conv_w_0, a_log_0, dt_bias_0, o_norm_0,
           mem_norm_0, w_mem_kv_0, w_out_0, norm_1, w_in_1, mem_norm_1, w_mem_kv_1,
           w_out_1, final_norm):
    b, s, _ = x.shape
    assert s % (2 * MOBA_QSTEP * MOBA_BLOCK) == 0 and MOBA_TOPK <= s // MOBA_BLOCK <= NB_PAD
    assert s % (DELTA_GROUP * DELTA_CHUNK) == 0
    tm = 512

    mkv0, mkv1 = _memkv(mem, _row(mem_norm_0), w_mem_kv_0.astype(BF16),
                        _row(mem_norm_1), w_mem_kv_1.astype(BF16))

    i1 = 3 * MIX_WIDTH
    i2 = i1 + GATE_WIDTH
    i3 = i2 + MEM_WIDTH
    w0 = w_in_0.astype(BF16)
    wba = jnp.pad(w0[:, i3:], ((0, 0), (0, LANES - 2 * MIX_HEADS)))
    qkv, z0, mq0, ba = _inproj0(x, _row(norm_0), w0[:, :i1], w0[:, i1:i2], w0[:, i2:i3], wba, tm)
    ba_cols = lambda v: _row(jnp.pad(v, (MIX_HEADS, LANES - 2 * MIX_HEADS)))
    o0 = _delta_mixer(qkv, ba, conv_w_0.astype(F32), ba_cols(a_log_0), ba_cols(dt_bias_0))
    wo0 = w_out_0.astype(BF16)
    h1 = _outproj(o0, z0, mq0, mkv0, x, wo0[:MIX_WIDTH], wo0[MIX_WIDTH:],
                  _row(jnp.tile(o_norm_0, MIX_HEADS)), _row(final_norm),
                  tm, head_norm=True, final_norm=False)

    w1 = w_in_1.astype(BF16)
    half = HEAD_DIM // 2
    inv_freq = ROPE_THETA ** (-jnp.arange(half, dtype=F32) * (2.0 / HEAD_DIM))
    invf = _row(jnp.tile(inv_freq, LANES // half))
    q1, k1, v1, z1, mq1 = _inproj1(
        h1, _row(norm_1), positions.reshape(b, s, 1), invf,
        w1[:, :MIX_WIDTH], w1[:, MIX_WIDTH:2 * MIX_WIDTH], w1[:, 2 * MIX_WIDTH:i1],
        w1[:, i1:i2], w1[:, i2:], tm)
    o1 = _moba(q1, k1, v1)
    wo1 = w_out_1.astype(BF16)
    return _outproj(o1, z1, mq1, mkv1, h1, wo1[:MIX_WIDTH], wo1[MIX_WIDTH:],
                    jnp.ones((1, MIX_WIDTH), F32), _row(final_norm),
                    tm, head_norm=False, final_norm=True)
```

```python
import functools

import jax
import jax.numpy as jnp
from jax import lax
from jax.experimental import pallas as pl
from jax.experimental.pallas import tpu as pltpu

D_MODEL = 1024
HEAD_DIM = 64
HEAD_SHIFT = HEAD_DIM.bit_length() - 1
MIX_HEADS = 12
MEM_HEADS = 4
MIX_WIDTH = MIX_HEADS * HEAD_DIM
MEM_WIDTH = MEM_HEADS * HEAD_DIM
GATE_WIDTH = MIX_WIDTH + MEM_WIDTH
N_MEM = 256
CONV_WIDTH = 4
MOBA_BLOCK = 256
MOBA_TOPK = 3
ROPE_THETA = 10000.0
NORM_EPS = 1e-6
MASK_VALUE = -1e30

LANES = 128
SUBLANES = 8
PAIR = 2 * HEAD_DIM
N_PAIRS = MIX_HEADS // 2
DELTA_CHUNK = 128
DELTA_GROUP = 8
DELTA_PAIRS = 2
INV_BASE = 16
QK_SCALE = HEAD_DIM ** -0.5
LOG2E = 1.4426950408889634
MOBA_QSTEP = 8
MOBA_PAIRS = 2
NB_PAD = 16
VT_ROWS = HEAD_DIM + 16
VMEM_LIMIT = 56 * 1024 * 1024

BF16 = jnp.bfloat16
F32 = jnp.float32

_NT = (((1,), (1,)), ((), ()))


def _dot(a, b):
    return jnp.dot(a.astype(BF16), b.astype(BF16), preferred_element_type=F32)


def _dot_nt(a, b):
    return lax.dot_general(a.astype(BF16), b.astype(BF16), _NT, preferred_element_type=F32)


def _dot_hi(a, b):
    return jnp.dot(a, b, precision=lax.Precision.HIGHEST, preferred_element_type=F32)


def _dot_nt_hi(a, b):
    return lax.dot_general(a, b, _NT, precision=lax.Precision.HIGHEST,
                           preferred_element_type=F32)


def _rms_scale(x):
    return lax.rsqrt(jnp.mean(x * x, axis=-1, keepdims=True) + NORM_EPS)


def _sigmoid(x):
    return 1.0 / (1.0 + jnp.exp(-x))


def _silu(x):
    half = 0.5 * x
    return half + half * jnp.tanh(half)


def _softplus(x):
    return jnp.maximum(x, 0.0) + jnp.log(1.0 + jnp.exp(-jnp.abs(x)))


def _head_masks():
    lane = lax.broadcasted_iota(jnp.int32, (1, PAIR), 1)
    return [((lane >> HEAD_SHIFT) == h).astype(F32) for h in range(2)]


def _memkv_kernel(mem_ref, g0_ref, w0_ref, g1_ref, w1_ref, o0_ref, o1_ref):
    m = mem_ref[0]
    mn = m * _rms_scale(m)
    o0_ref[0] = _dot(mn * g0_ref[...], w0_ref[...]).astype(o0_ref.dtype)
    o1_ref[0] = _dot(mn * g1_ref[...], w1_ref[...]).astype(o1_ref.dtype)


def _memkv(mem, g0, w0, g1, w1):
    b = mem.shape[0]
    full = lambda shp: pl.BlockSpec(shp, lambda i: (0,) * len(shp))
    return pl.pallas_call(
        _memkv_kernel,
        grid=(b,),
        in_specs=[pl.BlockSpec((1, N_MEM, D_MODEL), lambda i: (i, 0, 0)),
                  full((1, D_MODEL)), full((D_MODEL, 2 * MEM_WIDTH)),
                  full((1, D_MODEL)), full((D_MODEL, 2 * MEM_WIDTH))],
        out_specs=[pl.BlockSpec((1, N_MEM, 2 * MEM_WIDTH), lambda i: (i, 0, 0))] * 2,
        out_shape=[jax.ShapeDtypeStruct((b, N_MEM, 2 * MEM_WIDTH), BF16)] * 2,
        compiler_params=pltpu.CompilerParams(vmem_limit_bytes=VMEM_LIMIT),
        name="memkv",
    )(mem, g0, w0, g1, w1)


def _inproj0_kernel(x_ref, g_ref, wqkv_ref, wz_ref, wmq_ref, wba_ref,
                    qkv_ref, z_ref, mq_ref, ba_ref):
    x = x_ref[0]
    xn = (x * _rms_scale(x) * g_ref[...]).astype(BF16)
    for w_ref, o_ref in ((wqkv_ref, qkv_ref), (wz_ref, z_ref),
                         (wmq_ref, mq_ref), (wba_ref, ba_ref)):
        o_ref[0] = jnp.dot(xn, w_ref[...], preferred_element_type=F32).astype(o_ref.dtype)


def _inproj0(x, g, wqkv, wz, wmq, wba, tm):
    b, s, _ = x.shape
    widths = (3 * MIX_WIDTH, GATE_WIDTH, MEM_WIDTH, LANES)
    wspec = lambda n: pl.BlockSpec((D_MODEL, n), lambda i, j: (0, 0))
    return pl.pallas_call(
        _inproj0_kernel,
        grid=(b, s // tm),
        in_specs=[pl.BlockSpec((1, tm, D_MODEL), lambda i, j: (i, j, 0)),
                  pl.BlockSpec((1, D_MODEL), lambda i, j: (0, 0))]
                 + [wspec(n) for n in widths],
        out_specs=[pl.BlockSpec((1, tm, n), lambda i, j: (i, j, 0)) for n in widths],
        out_shape=[jax.ShapeDtypeStruct((b, s, n), dt)
                   for n, dt in zip(widths, (BF16, BF16, BF16, F32))],
        compiler_params=pltpu.CompilerParams(vmem_limit_bytes=VMEM_LIMIT),
        name="inproj0",
    )(x, g, wqkv, wz, wmq, wba)


def _inproj1_kernel(x_ref, g_ref, pos_ref, invf_ref, wq_ref, wk_ref, wv_ref, wz_ref, wmq_ref,
                    q_ref, k_ref, v_ref, z_ref, mq_ref):
    x = x_ref[0]
    xn = (x * _rms_scale(x) * g_ref[...]).astype(BF16)
    n_freq = HEAD_DIM // 2
    per_pack = LANES // n_freq
    tm = x.shape[0]
    pos = pos_ref[0].astype(F32)
    lane_p = lax.broadcasted_iota(jnp.int32, (LANES, LANES), 1)
    block = lane_p // n_freq
    cos_parts, sin_parts = [], []
    for base in range(0, tm, per_pack * LANES):
        pos_p = jnp.zeros((LANES, LANES), F32)
        for gidx in range(per_pack):
            rows = slice(base + gidx * LANES, base + (gidx + 1) * LANES)
            pos_p = jnp.where(block == gidx, pos[rows], pos_p)
        ang_p = pos_p * invf_ref[...]
        for trig, parts in ((jnp.cos(ang_p), cos_parts), (jnp.sin(ang_p), sin_parts)):
            for gidx in range(per_pack):
                one = jnp.where(block == gidx, trig, 0.0)
                spread = one
                for k in range(1, per_pack):
                    spread = spread + pltpu.roll(one, k * n_freq, 1)
                parts.append(spread)
    cos = jnp.concatenate(cos_parts, axis=0)
    sin = jnp.concatenate(sin_parts, axis=0)
    lane = lax.broadcasted_iota(jnp.int32, (tm, LANES), 1)
    first_half = (lane & (HEAD_DIM - 1)) < (HEAD_DIM // 2)
    sin_signed = jnp.where(first_half, -sin, sin)
    for w_ref, o_ref in ((wq_ref, q_ref), (wk_ref, k_ref)):
        y = jnp.dot(xn, w_ref[...], preferred_element_type=F32)
        for sl in range(MIX_WIDTH // LANES):
            ys = y[:, sl * LANES:(sl + 1) * LANES]
            rot = jnp.where(first_half,
                            pltpu.roll(ys, LANES - HEAD_DIM // 2, 1),
                            pltpu.roll(ys, HEAD_DIM // 2, 1))
            o_ref[0, :, sl * LANES:(sl + 1) * LANES] = (ys * cos + rot * sin_signed).astype(o_ref.dtype)
    for w_ref, o_ref in ((wv_ref, v_ref), (wz_ref, z_ref), (wmq_ref, mq_ref)):
        o_ref[0] = jnp.dot(xn, w_ref[...], preferred_element_type=F32).astype(o_ref.dtype)


def _inproj1(x, g, pos3, invf, wq, wk, wv, wz, wmq, tm):
    b, s, _ = x.shape
    widths = (MIX_WIDTH, MIX_WIDTH, MIX_WIDTH, GATE_WIDTH, MEM_WIDTH)
    wspec = lambda n: pl.BlockSpec((D_MODEL, n), lambda i, j: (0, 0))
    return pl.pallas_call(
        _inproj1_kernel,
        grid=(b, s // tm),
        in_specs=[pl.BlockSpec((1, tm, D_MODEL), lambda i, j: (i, j, 0)),
                  pl.BlockSpec((1, D_MODEL), lambda i, j: (0, 0)),
                  pl.BlockSpec((1, tm, 1), lambda i, j: (i, j, 0)),
                  pl.BlockSpec((1, LANES), lambda i, j: (0, 0))]
                 + [wspec(n) for n in widths],
        out_specs=[pl.BlockSpec((1, tm, n), lambda i, j: (i, j, 0)) for n in widths],
        out_shape=[jax.ShapeDtypeStruct((b, s, n), BF16) for n in widths],
        compiler_params=pltpu.CompilerParams(vmem_limit_bytes=VMEM_LIMIT),
        name="inproj1",
    )(x, g, pos3, invf, wq, wk, wv, wz, wmq)


def _unit_lower_inverses(lows, row, col, eye, between):
    c = lows[0].shape[0]
    shift = INV_BASE.bit_length() - 1
    base = (row >> shift) == (col >> shift)
    ns = [jnp.where(base, -low, 0.0) for low in lows]
    xs = [eye + n for n in ns]
    ns = [_dot(n, n) for n in ns]
    between()
    power = 2
    while 2 * power < INV_BASE:
        prods = [_dot(jnp.concatenate([n, x], axis=0), n) for x, n in zip(xs, ns)]
        between()
        xs = [x + p[c:] for x, p in zip(xs, prods)]
        ns = [p[:c] for p in prods]
        power *= 2
    xs = [x + _dot(x, n) for x, n in zip(xs, ns)]
    between()
    size = INV_BASE
    while size < c:
        sh = size.bit_length() - 1
        sel = (((row >> (sh + 1)) == (col >> (sh + 1)))
               & (((row >> sh) & 1) == 1) & (((col >> sh) & 1) == 0))
        odd = [slice(r, r + size) for r in range(size, c, 2 * size)]
        x_odd = [jnp.concatenate([x[sl] for sl in odd], axis=0) for x in xs]
        ys = [_dot(xo, jnp.where(sel, low, 0.0)) for xo, low in zip(x_odd, lows)]
        between()
        new_odd = [xo - _dot(y, x) for xo, y, x in zip(x_odd, ys, xs)]
        between()
        xs = [jnp.concatenate(
            [blk for k, sl in enumerate(odd)
             for blk in (x[sl.start - size:sl.start], no[k * size:(k + 1) * size])], axis=0)
            for x, no in zip(xs, new_odd)]
        size *= 2
    return xs


def _delta_kernel(q_ref, k_ref, v_ref, ba_ref, cwq_ref, cwk_ref, cwv_ref,
                  alog_ref, dt_ref, o_ref,
                  ext_ref, p_ref, qp_ref, n_ref, op_ref, egl_ref):
    c = DELTA_CHUNK
    grp = DELTA_GROUP
    rows = grp * c
    seq = q_ref.shape[1]
    first_pair = DELTA_PAIRS * pl.program_id(1)
    masks = _head_masks()
    row = lax.broadcasted_iota(jnp.int32, (c, c), 0)
    col = lax.broadcasted_iota(jnp.int32, (c, c), 1)
    tri_incl = row >= col
    tri_strict = row > col
    eye = (row == col).astype(F32)
    tril_b = tri_incl.astype(BF16)
    head_block = ((row >> HEAD_SHIFT) == (col >> HEAD_SHIFT)).astype(F32)
    head_block_b = head_block.astype(BF16)
    row2 = lax.broadcasted_iota(jnp.int32, (LANES, 2 * PAIR), 0)
    col2 = lax.broadcasted_iota(jnp.int32, (LANES, 2 * PAIR), 1)
    expands = [(row2 == jnp.where(col2 < PAIR, 0, MIX_HEADS) + 2 * (first_pair + pp)
                + ((col2 >> HEAD_SHIFT) & 1)).astype(BF16) for pp in range(DELTA_PAIRS)]
    lane_n = lax.broadcasted_iota(jnp.int32, (1, LANES), 1)
    is_beta = lane_n < MIX_HEADS
    neg_a = -jnp.exp(alog_ref[...])
    dt_row = dt_ref[...]

    def conv_silu(idx, x_ref, cw_ref, t0, lanes):
        raw = x_ref[0, pl.ds(t0, rows), lanes].astype(F32)
        ext_ref[idx, SUBLANES:SUBLANES + rows, :] = raw
        acc = raw * cw_ref[CONV_WIDTH - 1:CONV_WIDTH, lanes]
        for tap in range(CONV_WIDTH - 1):
            back = CONV_WIDTH - 1 - tap
            acc = acc + ext_ref[idx, SUBLANES - back:SUBLANES - back + rows, :] * cw_ref[tap:tap + 1, lanes]
        ext_ref[idx, 0:SUBLANES, :] = raw[rows - SUBLANES:rows, :]
        return _silu(acc)

    def l2norm(x):
        return x * lax.rsqrt(_dot(x * x, head_block_b) + NORM_EPS)

    def build_group(pp, gi, between):
        lanes = slice(pp * PAIR, (pp + 1) * PAIR)
        t0 = pl.multiple_of(gi * rows, rows)
        qn = l2norm(conv_silu(0, q_ref, cwq_ref, t0, lanes))
        kn = l2norm(conv_silu(1, k_ref, cwk_ref, t0, lanes))
        vc = conv_silu(2, v_ref, cwv_ref, t0, lanes)
        ba = ba_ref[0, pl.ds(t0, rows), :]
        act = jnp.where(is_beta, _sigmoid(ba), neg_a * _softplus(ba + dt_row))
        bg = jnp.dot(act.astype(BF16), expands[pp], preferred_element_type=F32)
        beta = bg[:, :PAIR]
        g = bg[:, PAIR:]
        kb = kn * beta
        vb = vc * beta
        sls = [slice(ci * c, (ci + 1) * c) for ci in range(grp)]
        gcs = [jnp.dot(tril_b, g[sl].astype(BF16), preferred_element_type=F32) * LOG2E for sl in sls]
        between()
        gcts = [gc.T for gc in gcs]
        egs = [jnp.exp2(gc) for gc in gcs]
        probs = [(ci, h) for ci in range(grp) for h in range(2)]
        decays = []
        for ci, h in probs:
            gcol = jnp.broadcast_to(gcs[ci][:, h * HEAD_DIM:h * HEAD_DIM + 1], (c, c))
            grow = gcts[ci][h * HEAD_DIM:h * HEAD_DIM + 1, :]
            decays.append(jnp.exp2(jnp.where(tri_incl, gcol - grow, -jnp.inf)))
        kbhs = [kb[sls[ci]] * masks[h] for ci, h in probs]
        kq = [_dot_nt(jnp.concatenate(
            [kbhs[2 * ci], kbhs[2 * ci + 1]]
            + [qn[sls[ci]] * (masks[h] * QK_SCALE) for h in range(2)], axis=0), kn[sls[ci]])
            for ci in range(grp)]
        between()
        lows = [jnp.where(tri_strict, kq[ci][h * c:(h + 1) * c] * d, 0.0)
                for (ci, h), d in zip(probs, decays)]
        aqks = [jnp.where(tri_incl, kq[ci][(2 + h) * c:(3 + h) * c] * d, 0.0)
                for (ci, h), d in zip(probs, decays)]
        tinvs = _unit_lower_inverses(lows, row, col, eye, between)
        wus = [_dot(t, jnp.concatenate([kbh * egs[ci], vb[sls[ci]] * masks[h]], axis=1))
               for t, kbh, (ci, h) in zip(tinvs, kbhs, probs)]
        aus = [_dot(a, wu) for a, wu in zip(aqks, wus)]
        kds = [kn[sls[ci]] * jnp.exp2(gcs[ci][c - 1:c, :] - gcs[ci]) for ci in range(grp)]
        pns = [_dot(kds[ci].T, wus[2 * ci] + wus[2 * ci + 1]) for ci in range(grp)]
        for ci in range(grp):
            cidx = gi * grp + ci
            au = aus[2 * ci] + aus[2 * ci + 1]
            p_ref[cidx] = (head_block * pns[ci][:, :PAIR]).astype(BF16)
            n_ref[cidx] = head_block * pns[ci][:, PAIR:]
            qp_ref[cidx] = (qn[sls[ci]] * (QK_SCALE * egs[ci]) - au[:, :PAIR]).astype(BF16)
            op_ref[cidx] = au[:, PAIR:]
            egl_ref[pl.ds(cidx, 1), :] = jnp.exp2(gcs[ci][c - 1:c, :])

    def scan_step(pp, ci, state):
        sb = state.astype(BF16)
        o = jnp.dot(qp_ref[ci], sb, preferred_element_type=F32) + op_ref[ci]
        new_state = (state * egl_ref[pl.ds(ci, 1), :]
                     - jnp.dot(p_ref[ci], sb, preferred_element_type=F32) + n_ref[ci])
        o_ref[0, pl.ds(pl.multiple_of(ci * c, c), c), pp * PAIR:(pp + 1) * PAIR] = o.astype(o_ref.dtype)
        return new_state

    def build_and_scan(build_pp, build_gi, scan_pp, scan_gi, state):
        box = [state]
        pending = iter(range(grp))

        def one_scan_step():
            ci = next(pending, None)
            if ci is not None:
                box[0] = scan_step(scan_pp, scan_gi * grp + ci, box[0])

        build_group(build_pp, build_gi, one_scan_step)
        for ci in pending:
            box[0] = scan_step(scan_pp, scan_gi * grp + ci, box[0])
        return box[0]

    n_groups = seq // rows
    zero_state = jnp.zeros((PAIR, PAIR), F32)
    zero_halo = jnp.zeros((3, SUBLANES, PAIR), F32)
    state = zero_state
    for pp in range(DELTA_PAIRS):
        ext_ref[:, 0:SUBLANES, :] = zero_halo
        if pp == 0:
            build_group(0, 0, lambda: None)
        else:
            build_and_scan(pp, 0, pp - 1, n_groups - 1, state)
        state = lax.fori_loop(
            1, n_groups, lambda gi, st, pp=pp: build_and_scan(pp, gi, pp, gi - 1, st), zero_state)
    last = DELTA_PAIRS - 1
    lax.fori_loop((n_groups - 1) * grp, n_groups * grp, lambda ci, st: scan_step(last, ci, st), state)


def _delta_mixer(qkv, ba, conv_w, alog_row, dt_row):
    b, s, _ = qkv.shape
    n_chunks = s // DELTA_CHUNK
    width = DELTA_PAIRS * PAIR
    n_steps = N_PAIRS // DELTA_PAIRS
    seq_spec = lambda off: pl.BlockSpec((1, s, width), lambda i, p, off=off: (i, 0, off + p))
    cw_spec = lambda off: pl.BlockSpec((CONV_WIDTH, width), lambda i, p, off=off: (0, off + p))
    row_spec = pl.BlockSpec((1, LANES), lambda i, p: (0, 0))
    return pl.pallas_call(
        _delta_kernel,
        grid=(b, n_steps),
        in_specs=[seq_spec(0), seq_spec(n_steps), seq_spec(2 * n_steps),
                  pl.BlockSpec((1, s, LANES), lambda i, p: (i, 0, 0)),
                  cw_spec(0), cw_spec(n_steps), cw_spec(2 * n_steps),
                  row_spec, row_spec],
        out_specs=pl.BlockSpec((1, s, width), lambda i, p: (i, 0, p)),
        out_shape=jax.ShapeDtypeStruct((b, s, MIX_WIDTH), BF16),
        scratch_shapes=[pltpu.VMEM((3, SUBLANES + DELTA_GROUP * DELTA_CHUNK, PAIR), F32),
                        pltpu.VMEM((n_chunks, PAIR, PAIR), BF16),
                        pltpu.VMEM((n_chunks, DELTA_CHUNK, PAIR), BF16),
                        pltpu.VMEM((n_chunks, PAIR, PAIR), F32),
                        pltpu.VMEM((n_chunks, DELTA_CHUNK, PAIR), F32),
                        pltpu.VMEM((n_chunks, PAIR), F32)],
        compiler_params=pltpu.CompilerParams(vmem_limit_bytes=VMEM_LIMIT),
        name="delta_mixer",
    )(qkv, qkv, qkv, ba, conv_w, conv_w, conv_w, alog_row, dt_row)


def _moba_kernel(q_ref, k_ref, v_ref, o_ref, kaug_ref, vt_ref, kmean_ref, qaug_ref, sa_ref, sb_ref):
    blk = MOBA_BLOCK
    qw = 2 * blk
    seq = k_ref.shape[1]
    nb = seq // blk
    step = pl.program_id(2)
    masks = _head_masks()
    n_pairs = MOBA_PAIRS
    heads = range(2 * n_pairs)

    @pl.when(step == 0)
    def _prepare():
        lane = lax.broadcasted_iota(jnp.int32, (blk, LANES), 1)
        ones_row = (lax.broadcasted_iota(jnp.int32, (VT_ROWS - HEAD_DIM, LANES), 0) == 0).astype(BF16)

        def key_block(j):
            r0 = pl.multiple_of(j * blk, blk)
            onehot = (lane == j).astype(BF16)
            for pr in range(n_pairs):
                kj = k_ref[0, pl.ds(r0, blk), pr * PAIR:(pr + 1) * PAIR]
                vj = v_ref[0, pl.ds(r0, blk), pr * PAIR:(pr + 1) * PAIR].astype(F32)
                kaug_ref[pr, pl.ds(r0, blk), 0:LANES] = kj
                kaug_ref[pr, pl.ds(r0, blk), LANES:2 * LANES] = onehot
                for t in range(blk // LANES):
                    c0 = pl.multiple_of(r0 + t * LANES, LANES)
                    vt = vj[t * LANES:(t + 1) * LANES, :].T.astype(BF16)
                    for h in range(2):
                        vt_ref[2 * pr + h, 0:HEAD_DIM, pl.ds(c0, LANES)] = vt[h * HEAD_DIM:(h + 1) * HEAD_DIM, :]
                        vt_ref[2 * pr + h, HEAD_DIM:VT_ROWS, pl.ds(c0, LANES)] = ones_row

        def block_mean(j, carry):
            kj = k_ref[0, pl.ds(pl.multiple_of(j * blk, blk), blk), :].astype(F32)
            mean = jnp.sum(kj, axis=0, keepdims=True) * (1.0 / blk)
            for pr in range(n_pairs):
                kmean_ref[pr, pl.ds(j, 1), :] = mean[:, pr * PAIR:(pr + 1) * PAIR]
            return carry
        kmean_ref[...] = jnp.zeros((n_pairs, NB_PAD, PAIR), F32)
        lax.fori_loop(0, nb, block_mean, 0)

        kstacks = []
        for pr in range(n_pairs):
            pieces = []
            for h in range(2):
                rest = kmean_ref[pr] * masks[h]
                for _ in range(3):
                    part = rest.astype(BF16)
                    pieces.append(part)
                    rest = rest - part.astype(F32)
            kstacks.append(jnp.concatenate(pieces, axis=0))
        blk_id = lax.broadcasted_iota(jnp.int32, (NB_PAD, qw), 0)
        blk_f = blk_id.astype(F32)
        half = lax.broadcasted_iota(jnp.int32, (NB_PAD, qw), 1) >> (blk.bit_length() - 1)
        other_head_rows = jnp.zeros((HEAD_DIM, qw), BF16)
        pad_rows = jnp.zeros((PAIR - NB_PAD, qw), BF16)

        def query_blocks(m, carry):
            key_block(2 * m)
            key_block(2 * m + 1)
            c0 = pl.multiple_of(m * qw, qw)
            own_blk = 2 * m + half
            valid = blk_id < own_blk
            q_ts, gs, opens = [], [], []
            for pr in range(n_pairs):
                q = q_ref[0, pl.ds(c0, qw), pr * PAIR:(pr + 1) * PAIR]
                qf = q.astype(F32) * (QK_SCALE * LOG2E)
                q_ts.append(jnp.concatenate(
                    [qf[t * LANES:(t + 1) * LANES, :].T for t in range(qw // LANES)], axis=1))
                gate_parts = lax.dot_general(kstacks[pr], q, _NT, preferred_element_type=F32)
                for h in range(2):
                    g3 = [gate_parts[(3 * h + i) * NB_PAD:(3 * h + i + 1) * NB_PAD] for i in range(3)]
                    gs.append(jnp.where(valid, g3[0] + g3[1] + g3[2], -jnp.inf))
                    opens.append((blk_id == own_blk).astype(F32))
            for _ in range(MOBA_TOPK):
                for hh in heads:
                    top = jnp.max(gs[hh], axis=0, keepdims=True)
                    first = jnp.min(jnp.where(gs[hh] == top, blk_f, float(NB_PAD)), axis=0, keepdims=True)
                    pick = (blk_f == first) & (top > -jnp.inf)
                    opens[hh] = jnp.where(pick, 1.0, opens[hh])
                    gs[hh] = jnp.where(pick, -jnp.inf, gs[hh])
            for hh in heads:
                bias = jnp.where(opens[hh] > 0.5, 0.0, MASK_VALUE).astype(BF16)
                own = slice((hh % 2) * HEAD_DIM, (hh % 2 + 1) * HEAD_DIM)
                other = slice((1 - hh % 2) * HEAD_DIM, (2 - hh % 2) * HEAD_DIM)
                qaug_ref[hh, own, pl.ds(c0, qw)] = q_ts[hh // 2][own].astype(BF16)
                qaug_ref[hh, other, pl.ds(c0, qw)] = other_head_rows
                qaug_ref[hh, PAIR:PAIR + NB_PAD, pl.ds(c0, qw)] = bias
                qaug_ref[hh, PAIR + NB_PAD:2 * PAIR, pl.ds(c0, qw)] = pad_rows
            return carry
        lax.fori_loop(0, nb // 2, query_blocks, 0)


    key_pos = lax.broadcasted_iota(jnp.int32, (blk, blk), 0)
    qry_pos = lax.broadcasted_iota(jnp.int32, (blk, blk), 1)
    always = jnp.int32(1 << 20)

    def scores(qi, qs, t, s_ref, where):
        r0 = pl.multiple_of(t * (2 * blk), 2 * blk)
        k2 = [kaug_ref[pr, pl.ds(r0, 2 * blk), :] for pr in range(n_pairs)]
        sts = [jnp.dot(k2[hh // 2], qs[hh], preferred_element_type=F32) for hh in heads]
        maxima = []
        for hh, st in enumerate(sts):
            if where != "past":
                causal = key_pos <= qry_pos + (0 if where == "own" else jnp.where(t == qi, 0, always))
                st = jnp.concatenate(
                    [jnp.concatenate([jnp.where(causal, st[:blk, :blk], MASK_VALUE), st[:blk, blk:]], axis=1),
                     jnp.concatenate([st[blk:, :blk], jnp.where(causal, st[blk:, blk:], MASK_VALUE)], axis=1)],
                    axis=0)
            s_ref[hh] = st
            maxima.append(jnp.max(st, axis=0, keepdims=True))
        return maxima

    def update(t, state, maxima, s_ref):
        r0 = pl.multiple_of(t * (2 * blk), 2 * blk)
        m_new = [jnp.maximum(state[2 * hh], maxima[hh]) for hh in heads]
        ps = [jnp.exp2(s_ref[hh] - m_new[hh]).astype(BF16) for hh in heads]
        pvs = [jnp.dot(vt_ref[hh, :, pl.ds(r0, 2 * blk)], ps[hh], preferred_element_type=F32)
               for hh in heads]
        out = []
        for hh in heads:
            alpha = jnp.exp2(state[2 * hh] - m_new[hh])
            out += [m_new[hh], state[2 * hh + 1] * alpha + pvs[hh]]
        return out

    init = []
    for hh in heads:
        init += [jnp.full((1, qw), MASK_VALUE, F32), jnp.zeros((VT_ROWS, qw), F32)]
    n_state = len(init)

    def write_rows(which, fin):
        for pr in range(n_pairs):
            out_t = jnp.concatenate(
                [fin[2 * hh + 1][:HEAD_DIM] / fin[2 * hh + 1][HEAD_DIM:HEAD_DIM + 1]
                 for hh in (2 * pr, 2 * pr + 1)], axis=0)
            o_ref[0, which * qw:(which + 1) * qw, pr * PAIR:(pr + 1) * PAIR] = out_t.T.astype(o_ref.dtype)

    def query_columns(qi):
        return [qaug_ref[hh, :, pl.ds(pl.multiple_of(qi * qw, qw), qw)] for hh in heads]

    def run_super_block(which, first_max, bufs, start_next):
        qi = MOBA_QSTEP * step + which
        qs = query_columns(qi)
        trips = (MOBA_QSTEP // 2) * step + which // 2
        b0, b1 = bufs
        last_where = "past" if which % 2 else "any"

        def two_pairs(u, carry):
            t0 = 2 * u
            max1 = scores(qi, qs, t0 + 1, b1, "past")
            state = update(t0, carry[:n_state], carry[n_state:], b0)
            max0 = scores(qi, qs, t0 + 2, b0, last_where)
            state = update(t0 + 1, state, max1, b1)
            return tuple(state + max0)

        carry = lax.fori_loop(0, trips, two_pairs, tuple(init + first_max))
        state, max0 = list(carry[:n_state]), carry[n_state:]
        if which % 2 == 0:
            nxt = start_next(b1)
            write_rows(which, update(qi, state, max0, b0))
            return nxt, (b1, b0)
        own = scores(qi, qs, qi, b1, "own")
        state = update(qi - 1, state, max0, b0)
        nxt = start_next(b0)
        write_rows(which, update(qi, state, own, b1))
        return nxt, (b0, b1)

    bufs = (sa_ref, sb_ref)
    q_first = MOBA_QSTEP * step
    first_max = scores(q_first, query_columns(q_first), 0, sa_ref, "any")
    for which in range(MOBA_QSTEP):
        if which + 1 < MOBA_QSTEP:
            q_next = MOBA_QSTEP * step + which + 1
            start_next = lambda s_ref, q_next=q_next: scores(q_next, query_columns(q_next), 0, s_ref, "past")
        else:
            start_next = lambda s_ref: None
        first_max, bufs = run_super_block(which, first_max, bufs, start_next)


def _moba(q, k, v):
    b, s, _ = q.shape
    nb = s // MOBA_BLOCK
    width = MOBA_PAIRS * PAIR
    n_heads = 2 * MOBA_PAIRS
    seq_spec = pl.BlockSpec((1, s, width), lambda i, p, j: (i, 0, p))
    return pl.pallas_call(
        _moba_kernel,
        grid=(b, N_PAIRS // MOBA_PAIRS, nb // (2 * MOBA_QSTEP)),
        in_specs=[seq_spec, seq_spec, seq_spec],
        out_specs=pl.BlockSpec((1, 2 * MOBA_QSTEP * MOBA_BLOCK, width), lambda i, p, j: (i, j, p)),
        out_shape=jax.ShapeDtypeStruct((b, s, MIX_WIDTH), BF16),
        scratch_shapes=[pltpu.VMEM((MOBA_PAIRS, s, 2 * LANES), BF16),
                        pltpu.VMEM((n_heads, VT_ROWS, s), BF16),
                        pltpu.VMEM((MOBA_PAIRS, NB_PAD, PAIR), F32),
                        pltpu.VMEM((n_heads, 2 * LANES, s), BF16),
                        pltpu.VMEM((n_heads, 2 * MOBA_BLOCK, 2 * MOBA_BLOCK), F32),
                        pltpu.VMEM((n_heads, 2 * MOBA_BLOCK, 2 * MOBA_BLOCK), F32)],
        compiler_params=pltpu.CompilerParams(
            dimension_semantics=("arbitrary", "arbitrary", "arbitrary"),
            vmem_limit_bytes=VMEM_LIMIT),
        name="moba",
    )(q, k, v)


def _outproj_kernel(mix_ref, z_ref, mq_ref, mkv_ref, h_ref, wmix_ref, wmem_ref, hg_ref, fg_ref,
                    o_ref, *, head_norm, final_norm):
    mq = mq_ref[0].astype(F32)
    mk = mkv_ref[0, :, :MEM_WIDTH]
    mv = mkv_ref[0, :, MEM_WIDTH:]
    row = lax.broadcasted_iota(jnp.int32, (MEM_WIDTH, MEM_WIDTH), 0)
    col = lax.broadcasted_iota(jnp.int32, (MEM_WIDTH, MEM_WIDTH), 1)
    lane = lax.broadcasted_iota(jnp.int32, (1, MEM_WIDTH), 1)
    mhs = [((lane >> HEAD_SHIFT) == h).astype(F32) for h in range(MEM_HEADS)]
    logits = [_dot_nt(mq * (mh * (QK_SCALE * LOG2E)), mk) for mh in mhs]
    es = [jnp.exp2(lg - jnp.max(lg, axis=-1, keepdims=True)) for lg in logits]
    inv = [1.0 / jnp.sum(e, axis=-1, keepdims=True) for e in es]
    pvs = [jnp.dot(e.astype(BF16), mv[:, (h // 2) * PAIR:(h // 2 + 1) * PAIR], preferred_element_type=F32)
           for h, e in enumerate(es)]
    pair_masks = _head_masks()
    memo = jnp.concatenate(
        [pvs[2 * pp] * (pair_masks[0] * inv[2 * pp]) + pvs[2 * pp + 1] * (pair_masks[1] * inv[2 * pp + 1])
         for pp in range(MEM_HEADS // 2)], axis=1)
    z = z_ref[0].astype(F32)
    gate = _silu(z)
    mix = mix_ref[0].astype(F32)
    if head_norm:
        same_head = ((row >> HEAD_SHIFT) == (col >> HEAD_SHIFT)).astype(BF16)
        slabs = []
        for sl in range(MIX_WIDTH // MEM_WIDTH):
            ms = mix[:, sl * MEM_WIDTH:(sl + 1) * MEM_WIDTH]
            ss = _dot(ms * ms, same_head) * (1.0 / HEAD_DIM)
            slabs.append(ms * lax.rsqrt(ss + NORM_EPS))
        mix = jnp.concatenate(slabs, axis=1) * hg_ref[...]
    y_mix = mix * gate[:, :MIX_WIDTH]
    y_mem = memo * gate[:, MIX_WIDTH:]
    out = h_ref[0] + _dot(y_mix, wmix_ref[...]) + _dot(y_mem, wmem_ref[...])
    if final_norm:
        out = out * _rms_scale(out) * fg_ref[...]
    o_ref[0] = out


def _outproj(mix, z, mq, mkv, h, wmix, wmem, hg, fg, tm, head_norm, final_norm):
    b, s, _ = h.shape
    tile = lambda n: pl.BlockSpec((1, tm, n), lambda i, j: (i, j, 0))
    return pl.pallas_call(
        functools.partial(_outproj_kernel, head_norm=head_norm, final_norm=final_norm),
        grid=(b, s // tm),
        in_specs=[tile(MIX_WIDTH), tile(GATE_WIDTH), tile(MEM_WIDTH),
                  pl.BlockSpec((1, N_MEM, 2 * MEM_WIDTH), lambda i, j: (i, 0, 0)),
                  tile(D_MODEL),
                  pl.BlockSpec((MIX_WIDTH, D_MODEL), lambda i, j: (0, 0)),
                  pl.BlockSpec((MEM_WIDTH, D_MODEL), lambda i, j: (0, 0)),
                  pl.BlockSpec((1, MIX_WIDTH), lambda i, j: (0, 0)),
                  pl.BlockSpec((1, D_MODEL), lambda i, j: (0, 0))],
        out_specs=tile(D_MODEL),
        out_shape=jax.ShapeDtypeStruct((b, s, D_MODEL), F32),
        compiler_params=pltpu.CompilerParams(vmem_limit_bytes=VMEM_LIMIT),
        name="outproj_final" if final_norm else "outproj",
    )(mix, z, mq, mkv, h, wmix, wmem, hg, fg)


def _row(v):
    return v.reshape(1, -1).astype(F32)


def kernel(x, mem, positions, norm_0, w_in_0, conv_w_0, a_log_0, dt_bias_0, o_norm_0,
           mem_norm_0, w_mem_kv_0, w_out_0, norm_1, w_in_1, mem_norm_1, w_mem_kv_1,
           w_out_1, final_norm):
    b, s, _ = x.shape
    assert s % (2 * MOBA_QSTEP * MOBA_BLOCK) == 0 and MOBA_TOPK <= s // MOBA_BLOCK <= NB_PAD
    assert s % (DELTA_GROUP * DELTA_CHUNK) == 0
    tm = 512

    mkv0, mkv1 = _memkv(mem, _row(mem_norm_0), w_mem_kv_0.astype(BF16),
                        _row(mem_norm_1), w_mem_kv_1.astype(BF16))

    i1 = 3 * MIX_WIDTH
    i2 = i1 + GATE_WIDTH
    i3 = i2 + MEM_WIDTH
    w0 = w_in_0.astype(BF16)
    wba = jnp.pad(w0[:, i3:], ((0, 0), (0, LANES - 2 * MIX_HEADS)))
    qkv, z0, mq0, ba = _inproj0(x, _row(norm_0), w0[:, :i1], w0[:, i1:i2], w0[:, i2:i3], wba, tm)
    ba_cols = lambda v: _row(jnp.pad(v, (MIX_HEADS, LANES - 2 * MIX_HEADS)))
    o0 = _delta_mixer(qkv, ba, conv_w_0.astype(F32), ba_cols(a_log_0), ba_cols(dt_bias_0))
    wo0 = w_out_0.astype(BF16)
    h1 = _outproj(o0, z0, mq0, mkv0, x, wo0[:MIX_WIDTH], wo0[MIX_WIDTH:],
                  _row(jnp.tile(o_norm_0, MIX_HEADS)), _row(final_norm),
                  tm, head_norm=True, final_norm=False)

    w1 = w_in_1.astype(BF16)
    half = HEAD_DIM // 2
    inv_freq = ROPE_THETA ** (-jnp.arange(half, dtype=F32) * (2.0 / HEAD_DIM))
    invf = _row(jnp.tile(inv_freq, LANES // half))
    q1, k1, v1, z1, mq1 = _inproj1(
        h1, _row(norm_1), positions.reshape(b, s, 1), invf,
        w1[:, :MIX_WIDTH], w1[:, MIX_WIDTH:2 * MIX_WIDTH], w1[:, 2 * MIX_WIDTH:i1],
        w1[:, i1:i2], w1[:, i2:], tm)
    o1 = _moba(q1, k1, v1)
    wo1 = w_out_1.astype(BF16)
    return _outproj(o1, z1, mq1, mkv1, h1, wo1[:MIX_WIDTH], wo1[MIX_WIDTH:],
                    jnp.ones((1, MIX_WIDTH), F32), _row(final_norm),
                    tm, head_norm=False, final_norm=True)
```

```python
import functools

import jax
import jax.numpy as jnp
from jax import lax
from jax.experimental import pallas as pl
from jax.experimental.pallas import tpu as pltpu

D_MODEL = 1024
HEAD_DIM = 64
HEAD_SHIFT = HEAD_DIM.bit_length() - 1
MIX_HEADS = 12
MEM_HEADS = 4
MIX_WIDTH = MIX_HEADS * HEAD_DIM
MEM_WIDTH = MEM_HEADS * HEAD_DIM
GATE_WIDTH = MIX_WIDTH + MEM_WIDTH
N_MEM = 256
CONV_WIDTH = 4
MOBA_BLOCK = 256
MOBA_TOPK = 3
ROPE_THETA = 10000.0
NORM_EPS = 1e-6
MASK_VALUE = -1e30

LANES = 128
SUBLANES = 8
PAIR = 2 * HEAD_DIM
N_PAIRS = MIX_HEADS // 2
DELTA_CHUNK = 128
DELTA_GROUP = 8
DELTA_PAIRS = 2
INV_BASE = 16
QK_SCALE = HEAD_DIM ** -0.5
LOG2E = 1.4426950408889634
MOBA_QSTEP = 8
MOBA_PAIRS = 2
NB_PAD = 16
VT_ROWS = HEAD_DIM + 16
VMEM_LIMIT = 56 * 1024 * 1024

BF16 = jnp.bfloat16
F32 = jnp.float32

_NT = (((1,), (1,)), ((), ()))


def _dot(a, b):
    return jnp.dot(a.astype(BF16), b.astype(BF16), preferred_element_type=F32)


def _dot_nt(a, b):
    return lax.dot_general(a.astype(BF16), b.astype(BF16), _NT, preferred_element_type=F32)


def _dot_hi(a, b):
    return jnp.dot(a, b, precision=lax.Precision.HIGHEST, preferred_element_type=F32)


def _dot_nt_hi(a, b):
    return lax.dot_general(a, b, _NT, precision=lax.Precision.HIGHEST,
                           preferred_element_type=F32)


def _rms_scale(x):
    return lax.rsqrt(jnp.mean(x * x, axis=-1, keepdims=True) + NORM_EPS)


def _sigmoid(x):
    return 1.0 / (1.0 + jnp.exp(-x))


def _silu(x):
    half = 0.5 * x
    return half + half * jnp.tanh(half)


def _softplus(x):
    return jnp.maximum(x, 0.0) + jnp.log(1.0 + jnp.exp(-jnp.abs(x)))


def _head_masks():
    lane = lax.broadcasted_iota(jnp.int32, (1, PAIR), 1)
    return [((lane >> HEAD_SHIFT) == h).astype(F32) for h in range(2)]


def _memkv_kernel(mem_ref, g0_ref, w0_ref, g1_ref, w1_ref, o0_ref, o1_ref):
    m = mem_ref[0]
    mn = m * _rms_scale(m)
    o0_ref[0] = _dot(mn * g0_ref[...], w0_ref[...]).astype(o0_ref.dtype)
    o1_ref[0] = _dot(mn * g1_ref[...], w1_ref[...]).astype(o1_ref.dtype)


def _memkv(mem, g0, w0, g1, w1):
    b = mem.shape[0]
    full = lambda shp: pl.BlockSpec(shp, lambda i: (0,) * len(shp))
    return pl.pallas_call(
        _memkv_kernel,
        grid=(b,),
        in_specs=[pl.BlockSpec((1, N_MEM, D_MODEL), lambda i: (i, 0, 0)),
                  full((1, D_MODEL)), full((D_MODEL, 2 * MEM_WIDTH)),
                  full((1, D_MODEL)), full((D_MODEL, 2 * MEM_WIDTH))],
        out_specs=[pl.BlockSpec((1, N_MEM, 2 * MEM_WIDTH), lambda i: (i, 0, 0))] * 2,
        out_shape=[jax.ShapeDtypeStruct((b, N_MEM, 2 * MEM_WIDTH), BF16)] * 2,
        compiler_params=pltpu.CompilerParams(vmem_limit_bytes=VMEM_LIMIT),
        name="memkv",
    )(mem, g0, w0, g1, w1)


def _inproj0_kernel(x_ref, g_ref, wqkv_ref, wz_ref, wmq_ref, wba_ref,
                    qkv_ref, z_ref, mq_ref, ba_ref):
    x = x_ref[0]
    xn = (x * _rms_scale(x) * g_ref[...]).astype(BF16)
    for w_ref, o_ref in ((wqkv_ref, qkv_ref), (wz_ref, z_ref),
                         (wmq_ref, mq_ref), (wba_ref, ba_ref)):
        o_ref[0] = jnp.dot(xn, w_ref[...], preferred_element_type=F32).astype(o_ref.dtype)


def _inproj0(x, g, wqkv, wz, wmq, wba, tm):
    b, s, _ = x.shape
    widths = (3 * MIX_WIDTH, GATE_WIDTH, MEM_WIDTH, LANES)
    wspec = lambda n: pl.BlockSpec((D_MODEL, n), lambda i, j: (0, 0))
    return pl.pallas_call(
        _inproj0_kernel,
        grid=(b, s // tm),
        in_specs=[pl.BlockSpec((1, tm, D_MODEL), lambda i, j: (i, j, 0)),
                  pl.BlockSpec((1, D_MODEL), lambda i, j: (0, 0))]
                 + [wspec(n) for n in widths],
        out_specs=[pl.BlockSpec((1, tm, n), lambda i, j: (i, j, 0)) for n in widths],
        out_shape=[jax.ShapeDtypeStruct((b, s, n), dt)
                   for n, dt in zip(widths, (BF16, BF16, BF16, F32))],
        compiler_params=pltpu.CompilerParams(vmem_limit_bytes=VMEM_LIMIT),
        name="inproj0",
    )(x, g, wqkv, wz, wmq, wba)


def _inproj1_kernel(x_ref, g_ref, pos_ref, invf_ref, wq_ref, wk_ref, wv_ref, wz_ref, wmq_ref,
                    q_ref, k_ref, v_ref, z_ref, mq_ref):
    x = x_ref[0]
    xn = (x * _rms_scale(x) * g_ref[...]).astype(BF16)
    n_freq = HEAD_DIM // 2
    per_pack = LANES // n_freq
    tm = x.shape[0]
    pos = pos_ref[0].astype(F32)
    lane_p = lax.broadcasted_iota(jnp.int32, (LANES, LANES), 1)
    block = lane_p // n_freq
    cos_parts, sin_parts = [], []
    for base in range(0, tm, per_pack * LANES):
        pos_p = jnp.zeros((LANES, LANES), F32)
        for gidx in range(per_pack):
            rows = slice(base + gidx * LANES, base + (gidx + 1) * LANES)
            pos_p = jnp.where(block == gidx, pos[rows], pos_p)
        ang_p = pos_p * invf_ref[...]
        for trig, parts in ((jnp.cos(ang_p), cos_parts), (jnp.sin(ang_p), sin_parts)):
            for gidx in range(per_pack):
                one = jnp.where(block == gidx, trig, 0.0)
                spread = one
                for k in range(1, per_pack):
                    spread = spread + pltpu.roll(one, k * n_freq, 1)
                parts.append(spread)
    cos = jnp.concatenate(cos_parts, axis=0)
    sin = jnp.concatenate(sin_parts, axis=0)
    lane = lax.broadcasted_iota(jnp.int32, (tm, LANES), 1)
    first_half = (lane & (HEAD_DIM - 1)) < (HEAD_DIM // 2)
    sin_signed = jnp.where(first_half, -sin, sin)
    for w_ref, o_ref in ((wq_ref, q_ref), (wk_ref, k_ref)):
        y = jnp.dot(xn, w_ref[...], preferred_element_type=F32)
        for sl in range(MIX_WIDTH // LANES):
            ys = y[:, sl * LANES:(sl + 1) * LANES]
            rot = jnp.where(first_half,
                            pltpu.roll(ys, LANES - HEAD_DIM // 2, 1),
                            pltpu.roll(ys, HEAD_DIM // 2, 1))
            o_ref[0, :, sl * LANES:(sl + 1) * LANES] = (ys * cos + rot * sin_signed).astype(o_ref.dtype)
    for w_ref, o_ref in ((wv_ref, v_ref), (wz_ref, z_ref), (wmq_ref, mq_ref)):
        o_ref[0] = jnp.dot(xn, w_ref[...], preferred_element_type=F32).astype(o_ref.dtype)


def _inproj1(x, g, pos3, invf, wq, wk, wv, wz, wmq, tm):
    b, s, _ = x.shape
    widths = (MIX_WIDTH, MIX_WIDTH, MIX_WIDTH, GATE_WIDTH, MEM_WIDTH)
    wspec = lambda n: pl.BlockSpec((D_MODEL, n), lambda i, j: (0, 0))
    return pl.pallas_call(
        _inproj1_kernel,
        grid=(b, s // tm),
        in_specs=[pl.BlockSpec((1, tm, D_MODEL), lambda i, j: (i, j, 0)),
                  pl.BlockSpec((1, D_MODEL), lambda i, j: (0, 0)),
                  pl.BlockSpec((1, tm, 1), lambda i, j: (i, j, 0)),
                  pl.BlockSpec((1, LANES), lambda i, j: (0, 0))]
                 + [wspec(n) for n in widths],
        out_specs=[pl.BlockSpec((1, tm, n), lambda i, j: (i, j, 0)) for n in widths],
        out_shape=[jax.ShapeDtypeStruct((b, s, n), BF16) for n in widths],
        compiler_params=pltpu.CompilerParams(vmem_limit_bytes=VMEM_LIMIT),
        name="inproj1",
    )(x, g, pos3, invf, wq, wk, wv, wz, wmq)


def _unit_lower_inverses(lows, row, col, eye, between):
    c = lows[0].shape[0]
    shift = INV_BASE.bit_length() - 1
    base = (row >> shift) == (col >> shift)
    ns = [jnp.where(base, -low, 0.0) for low in lows]
    xs = [eye + n for n in ns]
    ns = [_dot(n, n) for n in ns]
    between()
    power = 2
    while 2 * power < INV_BASE:
        prods = [_dot(jnp.concatenate([n, x], axis=0), n) for x, n in zip(xs, ns)]
        between()
        xs = [x + p[c:] for x, p in zip(xs, prods)]
        ns = [p[:c] for p in prods]
        power *= 2
    xs = [x + _dot(x, n) for x, n in zip(xs, ns)]
    between()
    size = INV_BASE
    while size < c:
        sh = size.bit_length() - 1
        sel = (((row >> (sh + 1)) == (col >> (sh + 1)))
               & (((row >> sh) & 1) == 1) & (((col >> sh) & 1) == 0))
        odd = [slice(r, r + size) for r in range(size, c, 2 * size)]
        x_odd = [jnp.concatenate([x[sl] for sl in odd], axis=0) for x in xs]
        ys = [_dot(xo, jnp.where(sel, low, 0.0)) for xo, low in zip(x_odd, lows)]
        between()
        new_odd = [xo - _dot(y, x) for xo, y, x in zip(x_odd, ys, xs)]
        between()
        xs = [jnp.concatenate(
            [blk for k, sl in enumerate(odd)
             for blk in (x[sl.start - size:sl.start], no[k * size:(k + 1) * size])], axis=0)
            for x, no in zip(xs, new_odd)]
        size *= 2
    return xs


def _delta_kernel(q_ref, k_ref, v_ref, ba_ref, cwq_ref, cwk_ref, cwv_ref,
                  alog_ref, dt_ref, o_ref,
                  ext_ref, p_ref, qp_ref, n_ref, op_ref, egl_ref):
    c = DELTA_CHUNK
    grp = DELTA_GROUP
    rows = grp * c
    seq = q_ref.shape[1]
    first_pair = DELTA_PAIRS * pl.program_id(1)
    masks = _head_masks()
    row = lax.broadcasted_iota(jnp.int32, (c, c), 0)
    col = lax.broadcasted_iota(jnp.int32, (c, c), 1)
    tri_incl = row >= col
    tri_strict = row > col
    eye = (row == col).astype(F32)
    tril_b = tri_incl.astype(BF16)
    head_block = ((row >> HEAD_SHIFT) == (col >> HEAD_SHIFT)).astype(F32)
    head_block_b = head_block.astype(BF16)
    row2 = lax.broadcasted_iota(jnp.int32, (LANES, 2 * PAIR), 0)
    col2 = lax.broadcasted_iota(jnp.int32, (LANES, 2 * PAIR), 1)
    expands = [(row2 == jnp.where(col2 < PAIR, 0, MIX_HEADS) + 2 * (first_pair + pp)
                + ((col2 >> HEAD_SHIFT) & 1)).astype(BF16) for pp in range(DELTA_PAIRS)]
    lane_n = lax.broadcasted_iota(jnp.int32, (1, LANES), 1)
    is_beta = lane_n < MIX_HEADS
    neg_a = -jnp.exp(alog_ref[...])
    dt_row = dt_ref[...]

    def conv_silu(idx, x_ref, cw_ref, t0, lanes):
        raw = x_ref[0, pl.ds(t0, rows), lanes].astype(F32)
        ext_ref[idx, SUBLANES:SUBLANES + rows, :] = raw
        acc = raw * cw_ref[CONV_WIDTH - 1:CONV_WIDTH, lanes]
        for tap in range(CONV_WIDTH - 1):
            back = CONV_WIDTH - 1 - tap
            acc = acc + ext_ref[idx, SUBLANES - back:SUBLANES - back + rows, :] * cw_ref[tap:tap + 1, lanes]
        ext_ref[idx, 0:SUBLANES, :] = raw[rows - SUBLANES:rows, :]
        return _silu(acc)

    def l2norm(x):
        return x * lax.rsqrt(_dot(x * x, head_block_b) + NORM_EPS)

    def build_group(pp, gi, between):
        lanes = slice(pp * PAIR, (pp + 1) * PAIR)
        t0 = pl.multiple_of(gi * rows, rows)
        qn = l2norm(conv_silu(0, q_ref, cwq_ref, t0, lanes))
        kn = l2norm(conv_silu(1, k_ref, cwk_ref, t0, lanes))
        vc = conv_silu(2, v_ref, cwv_ref, t0, lanes)
        ba = ba_ref[0, pl.ds(t0, rows), :]
        act = jnp.where(is_beta, _sigmoid(ba), neg_a * _softplus(ba + dt_row))
        bg = jnp.dot(act.astype(BF16), expands[pp], preferred_element_type=F32)
        beta = bg[:, :PAIR]
        g = bg[:, PAIR:]
        kb = kn * beta
        vb = vc * beta
        sls = [slice(ci * c, (ci + 1) * c) for ci in range(grp)]
        gcs = [jnp.dot(tril_b, g[sl].astype(BF16), preferred_element_type=F32) * LOG2E for sl in sls]
        between()
        gcts = [gc.T for gc in gcs]
        egs = [jnp.exp2(gc) for gc in gcs]
        probs = [(ci, h) for ci in range(grp) for h in range(2)]
        decays = []
        for ci, h in probs:
            gcol = jnp.broadcast_to(gcs[ci][:, h * HEAD_DIM:h * HEAD_DIM + 1], (c, c))
            grow = gcts[ci][h * HEAD_DIM:h * HEAD_DIM + 1, :]
            decays.append(jnp.exp2(jnp.where(tri_incl, gcol - grow, -jnp.inf)))
        kbhs = [kb[sls[ci]] * masks[h] for ci, h in probs]
        kq = [_dot_nt(jnp.concatenate(
            [kbhs[2 * ci], kbhs[2 * ci + 1]]
            + [qn[sls[ci]] * (masks[h] * QK_SCALE) for h in range(2)], axis=0), kn[sls[ci]])
            for ci in range(grp)]
        between()
        lows = [jnp.where(tri_strict, kq[ci][h * c:(h + 1) * c] * d, 0.0)
                for (ci, h), d in zip(probs, decays)]
        aqks = [jnp.where(tri_incl, kq[ci][(2 + h) * c:(3 + h) * c] * d, 0.0)
                for (ci, h), d in zip(probs, decays)]
        tinvs = _unit_lower_inverses(lows, row, col, eye, between)
        wus = [_dot(t, jnp.concatenate([kbh * egs[ci], vb[sls[ci]] * masks[h]], axis=1))
               for t, kbh, (ci, h) in zip(tinvs, kbhs, probs)]
        aus = [_dot(a, wu) for a, wu in zip(aqks, wus)]
        kds = [kn[sls[ci]] * jnp.exp2(gcs[ci][c - 1:c, :] - gcs[ci]) for ci in range(grp)]
        pns = [_dot(kds[ci].T, wus[2 * ci] + wus[2 * ci + 1]) for ci in range(grp)]
        for ci in range(grp):
            cidx = gi * grp + ci
            au = aus[2 * ci] + aus[2 * ci + 1]
            p_ref[cidx] = (head_block * pns[ci][:, :PAIR]).astype(BF16)
            n_ref[cidx] = head_block * pns[ci][:, PAIR:]
            qp_ref[cidx] = (qn[sls[ci]] * (QK_SCALE * egs[ci]) - au[:, :PAIR]).astype(BF16)
            op_ref[cidx] = au[:, PAIR:]
            egl_ref[pl.ds(cidx, 1), :] = jnp.exp2(gcs[ci][c - 1:c, :])

    def scan_step(pp, ci, state):
        sb = state.astype(BF16)
        o = jnp.dot(qp_ref[ci], sb, preferred_element_type=F32) + op_ref[ci]
        new_state = (state * egl_ref[pl.ds(ci, 1), :]
                     - jnp.dot(p_ref[ci], sb, preferred_element_type=F32) + n_ref[ci])
        o_ref[0, pl.ds(pl.multiple_of(ci * c, c), c), pp * PAIR:(pp + 1) * PAIR] = o.astype(o_ref.dtype)
        return new_state

    def build_and_scan(build_pp, build_gi, scan_pp, scan_gi, state):
        box = [state]
        pending = iter(range(grp))

        def one_scan_step():
            ci = next(pending, None)
            if ci is not None:
                box[0] = scan_step(scan_pp, scan_gi * grp + ci, box[0])

        build_group(build_pp, build_gi, one_scan_step)
        for ci in pending:
            box[0] = scan_step(scan_pp, scan_gi * grp + ci, box[0])
        return box[0]

    n_groups = seq // rows
    zero_state = jnp.zeros((PAIR, PAIR), F32)
    zero_halo = jnp.zeros((3, SUBLANES, PAIR), F32)
    state = zero_state
    for pp in range(DELTA_PAIRS):
        ext_ref[:, 0:SUBLANES, :] = zero_halo
        if pp == 0:
            build_group(0, 0, lambda: None)
        else:
            build_and_scan(pp, 0, pp - 1, n_groups - 1, state)
        state = lax.fori_loop(
            1, n_groups, lambda gi, st, pp=pp: build_and_scan(pp, gi, pp, gi - 1, st), zero_state)
    last = DELTA_PAIRS - 1
    lax.fori_loop((n_groups - 1) * grp, n_groups * grp, lambda ci, st: scan_step(last, ci, st), state)


def _delta_mixer(qkv, ba, conv_w, alog_row, dt_row):
    b, s, _ = qkv.shape
    n_chunks = s // DELTA_CHUNK
    width = DELTA_PAIRS * PAIR
    n_steps = N_PAIRS // DELTA_PAIRS
    seq_spec = lambda off: pl.BlockSpec((1, s, width), lambda i, p, off=off: (i, 0, off + p))
    cw_spec = lambda off: pl.BlockSpec((CONV_WIDTH, width), lambda i, p, off=off: (0, off + p))
    row_spec = pl.BlockSpec((1, LANES), lambda i, p: (0, 0))
    return pl.pallas_call(
        _delta_kernel,
        grid=(b, n_steps),
        in_specs=[seq_spec(0), seq_spec(n_steps), seq_spec(2 * n_steps),
                  pl.BlockSpec((1, s, LANES), lambda i, p: (i, 0, 0)),
                  cw_spec(0), cw_spec(n_steps), cw_spec(2 * n_steps),
                  row_spec, row_spec],
        out_specs=pl.BlockSpec((1, s, width), lambda i, p: (i, 0, p)),
        out_shape=jax.ShapeDtypeStruct((b, s, MIX_WIDTH), BF16),
        scratch_shapes=[pltpu.VMEM((3, SUBLANES + DELTA_GROUP * DELTA_CHUNK, PAIR), F32),
                        pltpu.VMEM((n_chunks, PAIR, PAIR), BF16),
                        pltpu.VMEM((n_chunks, DELTA_CHUNK, PAIR), BF16),
                        pltpu.VMEM((n_chunks, PAIR, PAIR), F32),
                        pltpu.VMEM((n_chunks, DELTA_CHUNK, PAIR), F32),
                        pltpu.VMEM((n_chunks, PAIR), F32)],
        compiler_params=pltpu.CompilerParams(vmem_limit_bytes=VMEM_LIMIT),
        name="delta_mixer",
    )(qkv, qkv, qkv, ba, conv_w, conv_w, conv_w, alog_row, dt_row)


def _moba_kernel(q_ref, k_ref, v_ref, o_ref, kaug_ref, vt_ref, kmean_ref, qaug_ref, sa_ref, sb_ref):
    blk = MOBA_BLOCK
    qw = 2 * blk
    seq = k_ref.shape[1]
    nb = seq // blk
    step = pl.program_id(2)
    masks = _head_masks()
    n_pairs = MOBA_PAIRS
    heads = range(2 * n_pairs)

    @pl.when(step == 0)
    def _prepare():
        lane = lax.broadcasted_iota(jnp.int32, (blk, LANES), 1)
        ones_row = (lax.broadcasted_iota(jnp.int32, (VT_ROWS - HEAD_DIM, LANES), 0) == 0).astype(BF16)

        def key_block(j):
            r0 = pl.multiple_of(j * blk, blk)
            onehot = (lane == j).astype(BF16)
            for pr in range(n_pairs):
                kj = k_ref[0, pl.ds(r0, blk), pr * PAIR:(pr + 1) * PAIR]
                vj = v_ref[0, pl.ds(r0, blk), pr * PAIR:(pr + 1) * PAIR].astype(F32)
                kaug_ref[pr, pl.ds(r0, blk), 0:LANES] = kj
                kaug_ref[pr, pl.ds(r0, blk), LANES:2 * LANES] = onehot
                for t in range(blk // LANES):
                    c0 = pl.multiple_of(r0 + t * LANES, LANES)
                    vt = vj[t * LANES:(t + 1) * LANES, :].T.astype(BF16)
                    for h in range(2):
                        vt_ref[2 * pr + h, 0:HEAD_DIM, pl.ds(c0, LANES)] = vt[h * HEAD_DIM:(h + 1) * HEAD_DIM, :]
                        vt_ref[2 * pr + h, HEAD_DIM:VT_ROWS, pl.ds(c0, LANES)] = ones_row

        def block_mean(j, carry):
            kj = k_ref[0, pl.ds(pl.multiple_of(j * blk, blk), blk), :].astype(F32)
            mean = jnp.sum(kj, axis=0, keepdims=True) * (1.0 / blk)
            for pr in range(n_pairs):
                kmean_ref[pr, pl.ds(j, 1), :] = mean[:, pr * PAIR:(pr + 1) * PAIR]
            return carry
        kmean_ref[...] = jnp.zeros((n_pairs, NB_PAD, PAIR), F32)
        lax.fori_loop(0, nb, block_mean, 0)

        kstacks = []
        for pr in range(n_pairs):
            pieces = []
            for h in range(2):
                rest = kmean_ref[pr] * masks[h]
                for _ in range(3):
                    part = rest.astype(BF16)
                    pieces.append(part)
                    rest = rest - part.astype(F32)
            kstacks.append(jnp.concatenate(pieces, axis=0))
        blk_id = lax.broadcasted_iota(jnp.int32, (NB_PAD, qw), 0)
        blk_f = blk_id.astype(F32)
        half = lax.broadcasted_iota(jnp.int32, (NB_PAD, qw), 1) >> (blk.bit_length() - 1)
        other_head_rows = jnp.zeros((HEAD_DIM, qw), BF16)
        pad_rows = jnp.zeros((PAIR - NB_PAD, qw), BF16)

        def query_blocks(m, carry):
            key_block(2 * m)
            key_block(2 * m + 1)
            c0 = pl.multiple_of(m * qw, qw)
            own_blk = 2 * m + half
            valid = blk_id < own_blk
            q_ts, gs, opens = [], [], []
            for pr in range(n_pairs):
                q = q_ref[0, pl.ds(c0, qw), pr * PAIR:(pr + 1) * PAIR]
                qf = q.astype(F32) * (QK_SCALE * LOG2E)
                q_ts.append(jnp.concatenate(
                    [qf[t * LANES:(t + 1) * LANES, :].T for t in range(qw // LANES)], axis=1))
                gate_parts = lax.dot_general(kstacks[pr], q, _NT, preferred_element_type=F32)
                for h in range(2):
                    g3 = [gate_parts[(3 * h + i) * NB_PAD:(3 * h + i + 1) * NB_PAD] for i in range(3)]
                    gs.append(jnp.where(valid, g3[0] + g3[1] + g3[2], -jnp.inf))
                    opens.append((blk_id == own_blk).astype(F32))
            for _ in range(MOBA_TOPK):
                for hh in heads:
                    top = jnp.max(gs[hh], axis=0, keepdims=True)
                    first = jnp.min(jnp.where(gs[hh] == top, blk_f, float(NB_PAD)), axis=0, keepdims=True)
                    pick = (blk_f == first) & (top > -jnp.inf)
                    opens[hh] = jnp.where(pick, 1.0, opens[hh])
                    gs[hh] = jnp.where(pick, -jnp.inf, gs[hh])
            for hh in heads:
                bias = jnp.where(opens[hh] > 0.5, 0.0, MASK_VALUE).astype(BF16)
                own = slice((hh % 2) * HEAD_DIM, (hh % 2 + 1) * HEAD_DIM)
                other = slice((1 - hh % 2) * HEAD_DIM, (2 - hh % 2) * HEAD_DIM)
                qaug_ref[hh, own, pl.ds(c0, qw)] = q_ts[hh // 2][own].astype(BF16)
                qaug_ref[hh, other, pl.ds(c0, qw)] = other_head_rows
                qaug_ref[hh, PAIR:PAIR + NB_PAD, pl.ds(c0, qw)] = bias
                qaug_ref[hh, PAIR + NB_PAD:2 * PAIR, pl.ds(c0, qw)] = pad_rows
            return carry
        lax.fori_loop(0, nb // 2, query_blocks, 0)


    key_pos = lax.broadcasted_iota(jnp.int32, (blk, blk), 0)
    qry_pos = lax.broadcasted_iota(jnp.int32, (blk, blk), 1)
    always = jnp.int32(1 << 20)

    def scores(qi, qs, t, s_ref, where):
        r0 = pl.multiple_of(t * (2 * blk), 2 * blk)
        k2 = [kaug_ref[pr, pl.ds(r0, 2 * blk), :] for pr in range(n_pairs)]
        sts = [jnp.dot(k2[hh // 2], qs[hh], preferred_element_type=F32) for hh in heads]
        maxima = []
        for hh, st in enumerate(sts):
            if where != "past":
                causal = key_pos <= qry_pos + (0 if where == "own" else jnp.where(t == qi, 0, always))
                st = jnp.concatenate(
                    [jnp.concatenate([jnp.where(causal, st[:blk, :blk], MASK_VALUE), st[:blk, blk:]], axis=1),
                     jnp.concatenate([st[blk:, :blk], jnp.where(causal, st[blk:, blk:], MASK_VALUE)], axis=1)],
                    axis=0)
            s_ref[hh] = st
            maxima.append(jnp.max(st, axis=0, keepdims=True))
        return maxima

    def update(t, state, maxima, s_ref):
        r0 = pl.multiple_of(t * (2 * blk), 2 * blk)
        m_new = [jnp.maximum(state[2 * hh], maxima[hh]) for hh in heads]
        ps = [jnp.exp2(s_ref[hh] - m_new[hh]).astype(BF16) for hh in heads]
        pvs = [jnp.dot(vt_ref[hh, :, pl.ds(r0, 2 * blk)], ps[hh], preferred_element_type=F32)
               for hh in heads]
        out = []
        for hh in heads:
            alpha = jnp.exp2(state[2 * hh] - m_new[hh])
            out += [m_new[hh], state[2 * hh + 1] * alpha + pvs[hh]]
        return out

    init = []
    for hh in heads:
        init += [jnp.full((1, qw), MASK_VALUE, F32), jnp.zeros((VT_ROWS, qw), F32)]
    n_state = len(init)

    def write_rows(which, fin):
        for pr in range(n_pairs):
            out_t = jnp.concatenate(
                [fin[2 * hh + 1][:HEAD_DIM] / fin[2 * hh + 1][HEAD_DIM:HEAD_DIM + 1]
                 for hh in (2 * pr, 2 * pr + 1)], axis=0)
            o_ref[0, which * qw:(which + 1) * qw, pr * PAIR:(pr + 1) * PAIR] = out_t.T.astype(o_ref.dtype)

    def query_columns(qi):
        return [qaug_ref[hh, :, pl.ds(pl.multiple_of(qi * qw, qw), qw)] for hh in heads]

    def run_super_block(which, first_max, bufs, start_next):
        qi = MOBA_QSTEP * step + which
        qs = query_columns(qi)
        trips = (MOBA_QSTEP // 2) * step + which // 2
        b0, b1 = bufs
        last_where = "past" if which % 2 else "any"

        def two_pairs(u, carry):
            t0 = 2 * u
            max1 = scores(qi, qs, t0 + 1, b1, "past")
            state = update(t0, carry[:n_state], carry[n_state:], b0)
            max0 = scores(qi, qs, t0 + 2, b0, last_where)
            state = update(t0 + 1, state, max1, b1)
            return tuple(state + max0)

        carry = lax.fori_loop(0, trips, two_pairs, tuple(init + first_max))
        state, max0 = list(carry[:n_state]), carry[n_state:]
        if which % 2 == 0:
            nxt = start_next(b1)
            write_rows(which, update(qi, state, max0, b0))
            return nxt, (b1, b0)
        own = scores(qi, qs, qi, b1, "own")
        state = update(qi - 1, state, max0, b0)
        nxt = start_next(b0)
        write_rows(which, update(qi, state, own, b1))
        return nxt, (b0, b1)

    bufs = (sa_ref, sb_ref)
    q_first = MOBA_QSTEP * step
    first_max = scores(q_first, query_columns(q_first), 0, sa_ref, "any")
    for which in range(MOBA_QSTEP):
        if which + 1 < MOBA_QSTEP:
            q_next = MOBA_QSTEP * step + which + 1
            start_next = lambda s_ref, q_next=q_next: scores(q_next, query_columns(q_next), 0, s_ref, "past")
        else:
            start_next = lambda s_ref: None
        first_max, bufs = run_super_block(which, first_max, bufs, start_next)


def _moba(q, k, v):
    b, s, _ = q.shape
    nb = s // MOBA_BLOCK
    width = MOBA_PAIRS * PAIR
    n_heads = 2 * MOBA_PAIRS
    seq_spec = pl.BlockSpec((1, s, width), lambda i, p, j: (i, 0, p))
    return pl.pallas_call(
        _moba_kernel,
        grid=(b, N_PAIRS // MOBA_PAIRS, nb // (2 * MOBA_QSTEP)),
        in_specs=[seq_spec, seq_spec, seq_spec],
        out_specs=pl.BlockSpec((1, 2 * MOBA_QSTEP * MOBA_BLOCK, width), lambda i, p, j: (i, j, p)),
        out_shape=jax.ShapeDtypeStruct((b, s, MIX_WIDTH), BF16),
        scratch_shapes=[pltpu.VMEM((MOBA_PAIRS, s, 2 * LANES), BF16),
                        pltpu.VMEM((n_heads, VT_ROWS, s), BF16),
                        pltpu.VMEM((MOBA_PAIRS, NB_PAD, PAIR), F32),
                        pltpu.VMEM((n_heads, 2 * LANES, s), BF16),
                        pltpu.VMEM((n_heads, 2 * MOBA_BLOCK, 2 * MOBA_BLOCK), F32),
                        pltpu.VMEM((n_heads, 2 * MOBA_BLOCK, 2 * MOBA_BLOCK), F32)],
        compiler_params=pltpu.CompilerParams(
            dimension_semantics=("arbitrary", "arbitrary", "arbitrary"),
            vmem_limit_bytes=VMEM_LIMIT),
        name="moba",
    )(q, k, v)


def _outproj_kernel(mix_ref, z_ref, mq_ref, mkv_ref, h_ref, wmix_ref, wmem_ref, hg_ref, fg_ref,
                    o_ref, *, head_norm, final_norm):
    mq = mq_ref[0].astype(F32)
    mk = mkv_ref[0, :, :MEM_WIDTH]
    mv = mkv_ref[0, :, MEM_WIDTH:]
    row = lax.broadcasted_iota(jnp.int32, (MEM_WIDTH, MEM_WIDTH), 0)
    col = lax.broadcasted_iota(jnp.int32, (MEM_WIDTH, MEM_WIDTH), 1)
    lane = lax.broadcasted_iota(jnp.int32, (1, MEM_WIDTH), 1)
    mhs = [((lane >> HEAD_SHIFT) == h).astype(F32) for h in range(MEM_HEADS)]
    logits = [_dot_nt(mq * (mh * (QK_SCALE * LOG2E)), mk) for mh in mhs]
    es = [jnp.exp2(lg - jnp.max(lg, axis=-1, keepdims=True)) for lg in logits]
    inv = [1.0 / jnp.sum(e, axis=-1, keepdims=True) for e in es]
    pvs = [jnp.dot(e.astype(BF16), mv[:, (h // 2) * PAIR:(h // 2 + 1) * PAIR], preferred_element_type=F32)
           for h, e in enumerate(es)]
    pair_masks = _head_masks()
    memo = jnp.concatenate(
        [pvs[2 * pp] * (pair_masks[0] * inv[2 * pp]) + pvs[2 * pp + 1] * (pair_masks[1] * inv[2 * pp + 1])
         for pp in range(MEM_HEADS // 2)], axis=1)
    z = z_ref[0].astype(F32)
    gate = _silu(z)
    mix = mix_ref[0].astype(F32)
    if head_norm:
        same_head = ((row >> HEAD_SHIFT) == (col >> HEAD_SHIFT)).astype(BF16)
        slabs = []
        for sl in range(MIX_WIDTH // MEM_WIDTH):
            ms = mix[:, sl * MEM_WIDTH:(sl + 1) * MEM_WIDTH]
            ss = _dot(ms * ms, same_head) * (1.0 / HEAD_DIM)
            slabs.append(ms * lax.rsqrt(ss + NORM_EPS))
        mix = jnp.concatenate(slabs, axis=1) * hg_ref[...]
    y_mix = mix * gate[:, :MIX_WIDTH]
    y_mem = memo * gate[:, MIX_WIDTH:]
    out = h_ref[0] + _dot(y_mix, wmix_ref[...]) + _dot(y_mem, wmem_ref[...])
    if final_norm:
        out = out * _rms_scale(out) * fg_ref[...]
    o_ref[0] = out


def _outproj(mix, z, mq, mkv, h, wmix, wmem, hg, fg, tm, head_norm, final_norm):
    b, s, _ = h.shape
    tile = lambda n: pl.BlockSpec((1, tm, n), lambda i, j: (i, j, 0))
    return pl.pallas_call(
        functools.partial(_outproj_kernel, head_norm=head_norm, final_norm=final_norm),
        grid=(b, s // tm),
        in_specs=[tile(MIX_WIDTH), tile(GATE_WIDTH), tile(MEM_WIDTH),
                  pl.BlockSpec((1, N_MEM, 2 * MEM_WIDTH), lambda i, j: (i, 0, 0)),
                  tile(D_MODEL),
                  pl.BlockSpec((MIX_WIDTH, D_MODEL), lambda i, j: (0, 0)),
                  pl.BlockSpec((MEM_WIDTH, D_MODEL), lambda i, j: (0, 0)),
                  pl.BlockSpec((1, MIX_WIDTH), lambda i, j: (0, 0)),
                  pl.BlockSpec((1, D_MODEL), lambda i, j: (0, 0))],
        out_specs=tile(D_MODEL),
        out_shape=jax.ShapeDtypeStruct((b, s, D_MODEL), F32),
        compiler_params=pltpu.CompilerParams(vmem_limit_bytes=VMEM_LIMIT),
        name="outproj_final" if final_norm else "outproj",
    )(mix, z, mq, mkv, h, wmix, wmem, hg, fg)


def _layer_boundary_kernel(mix_ref, z_ref, mq_ref, mkv_ref, h_ref, wmix_ref, wmem_ref, hg_ref, fg_ref,
                           g1_ref, pos_ref, invf_ref, wq_ref, wk_ref, wv_ref, wz_ref, wmq_ref,
                           h1_ref, q_ref, k_ref, v_ref, z1_ref, mq1_ref):
    _outproj_kernel(mix_ref, z_ref, mq_ref, mkv_ref, h_ref, wmix_ref, wmem_ref, hg_ref, fg_ref, h1_ref,
                    head_norm=True, final_norm=False)
    _inproj1_kernel(h1_ref, g1_ref, pos_ref, invf_ref, wq_ref, wk_ref, wv_ref, wz_ref, wmq_ref,
                    q_ref, k_ref, v_ref, z1_ref, mq1_ref)


def _layer_boundary(mix, z, mq, mkv, h, wmix, wmem, hg, fg, g1, pos3, invf, wq, wk, wv, wz, wmq, tm):
    b, s, _ = h.shape
    tile = lambda n: pl.BlockSpec((1, tm, n), lambda i, j: (i, j, 0))
    const = lambda r, n: pl.BlockSpec((r, n), lambda i, j: (0, 0))
    widths = (MIX_WIDTH, MIX_WIDTH, MIX_WIDTH, GATE_WIDTH, MEM_WIDTH)
    return pl.pallas_call(
        _layer_boundary_kernel,
        grid=(b, s // tm),
        in_specs=[tile(MIX_WIDTH), tile(GATE_WIDTH), tile(MEM_WIDTH),
                  pl.BlockSpec((1, N_MEM, 2 * MEM_WIDTH), lambda i, j: (i, 0, 0)),
                  tile(D_MODEL), const(MIX_WIDTH, D_MODEL), const(MEM_WIDTH, D_MODEL),
                  const(1, MIX_WIDTH), const(1, D_MODEL),
                  const(1, D_MODEL), tile(1), const(1, LANES)]
                 + [const(D_MODEL, n) for n in widths],
        out_specs=[tile(D_MODEL)] + [tile(n) for n in widths],
        out_shape=[jax.ShapeDtypeStruct((b, s, D_MODEL), F32)]
                  + [jax.ShapeDtypeStruct((b, s, n), BF16) for n in widths],
        compiler_params=pltpu.CompilerParams(vmem_limit_bytes=VMEM_LIMIT),
        name="layer_boundary",
    )(mix, z, mq, mkv, h, wmix, wmem, hg, fg, g1, pos3, invf, wq, wk, wv, wz, wmq)


def _row(v):
    return v.reshape(1, -1).astype(F32)


def kernel(x, mem, positions, norm_0, w_in_0, conv_w_0, a_log_0, dt_bias_0, o_norm_0,
           mem_norm_0, w_mem_kv_0, w_out_0, norm_1, w_in_1, mem_norm_1, w_mem_kv_1,
           w_out_1, final_norm):
    b, s, _ = x.shape
    assert s % (2 * MOBA_QSTEP * MOBA_BLOCK) == 0 and MOBA_TOPK <= s // MOBA_BLOCK <= NB_PAD
    assert s % (DELTA_GROUP * DELTA_CHUNK) == 0
    tm = 512

    mkv0, mkv1 = _memkv(mem, _row(mem_norm_0), w_mem_kv_0.astype(BF16),
                        _row(mem_norm_1), w_mem_kv_1.astype(BF16))

    i1 = 3 * MIX_WIDTH
    i2 = i1 + GATE_WIDTH
    i3 = i2 + MEM_WIDTH
    w0 = w_in_0.astype(BF16)
    wba = jnp.pad(w0[:, i3:], ((0, 0), (0, LANES - 2 * MIX_HEADS)))
    qkv, z0, mq0, ba = _inproj0(x, _row(norm_0), w0[:, :i1], w0[:, i1:i2], w0[:, i2:i3], wba, tm)
    ba_cols = lambda v: _row(jnp.pad(v, (MIX_HEADS, LANES - 2 * MIX_HEADS)))
    o0 = _delta_mixer(qkv, ba, conv_w_0.astype(F32), ba_cols(a_log_0), ba_cols(dt_bias_0))
    wo0 = w_out_0.astype(BF16)
    w1 = w_in_1.astype(BF16)
    half = HEAD_DIM // 2
    inv_freq = ROPE_THETA ** (-jnp.arange(half, dtype=F32) * (2.0 / HEAD_DIM))
    invf = _row(jnp.tile(inv_freq, LANES // half))
    h1, q1, k1, v1, z1, mq1 = _layer_boundary(
        o0, z0, mq0, mkv0, x, wo0[:MIX_WIDTH], wo0[MIX_WIDTH:],
        _row(jnp.tile(o_norm_0, MIX_HEADS)), _row(final_norm),
        _row(norm_1), positions.reshape(b, s, 1), invf,
        w1[:, :MIX_WIDTH], w1[:, MIX_WIDTH:2 * MIX_WIDTH], w1[:, 2 * MIX_WIDTH:i1],
        w1[:, i1:i2], w1[:, i2:], tm)
    o1 = _moba(q1, k1, v1)
    wo1 = w_out_1.astype(BF16)
    return _outproj(o1, z1, mq1, mkv1, h1, wo1[:MIX_WIDTH], wo1[MIX_WIDTH:],
                    jnp.ones((1, MIX_WIDTH), F32), _row(final_norm),
                    tm, head_norm=False, final_norm=True)
```
